```python
import math
import jax
import jax.numpy as jnp
from jax import lax

D_MODEL = 1024
BATCH = 16
SEQ = 2048
DEPTH = 4

N_EVEN = (DEPTH + 1) // 2
N_ODD = DEPTH // 2
N_VRES = max(N_EVEN - 1, 0)

MIX_HALF = D_MODEL // 2

RW_WIDTH = MIX_HALF
RW_HEAD_DIM = 64
RW_HEADS = RW_WIDTH // RW_HEAD_DIM
RW_DECAY_RANK = 64
RW_ICLR_RANK = 64
RW_GATE_RANK = 128
RW_VRES_RANK = 32
RW_SHIFT_COLS = 3 * RW_WIDTH + RW_DECAY_RANK + RW_ICLR_RANK + RW_GATE_RANK
RW_SPLITS = (RW_WIDTH, 2 * RW_WIDTH, 3 * RW_WIDTH, 3 * RW_WIDTH + RW_DECAY_RANK,
             3 * RW_WIDTH + RW_DECAY_RANK + RW_ICLR_RANK)
RW_LN_EPS = 64e-5

POOL_WIDTH = MIX_HALF
POOL_WINDOWS = (2, 4, 8, 16)
POOL_GROUPS = len(POOL_WINDOWS)
POOL_GROUP_DIM = POOL_WIDTH // POOL_GROUPS

EVEN_IN = RW_SHIFT_COLS + POOL_WIDTH
EVEN_MIX = RW_WIDTH + POOL_WIDTH

DA_WIDTH = MIX_HALF
DA_HEAD_DIM = 64
DA_VALUE_DIM = 2 * DA_HEAD_DIM
DA_HEADS = DA_WIDTH // DA_VALUE_DIM
DA_QK_COLS = DA_HEADS * 2 * DA_HEAD_DIM
ROPE_THETA = 10000.0
Q_BLOCK = 128
SUBLN_EPS = 1e-5

S5_WIDTH = MIX_HALF
S5_GROUP_DIM = 16
S5_GROUPS = S5_WIDTH // S5_GROUP_DIM
S5_STATE = 64

ODD_IN = 2 * DA_QK_COLS + DA_WIDTH + S5_WIDTH
ODD_MIX = DA_WIDTH + S5_WIDTH

MOE_GROUPS = 4
MOE_EXPERTS_PER_GROUP = 8
MOE_EXPERTS = MOE_GROUPS * MOE_EXPERTS_PER_GROUP
MOE_TOP_K = 2
MOE_HIDDEN = D_MODEL // 2
MOE_BLOCK = 128

RMS_EPS = 1e-6

kernel_name = 'hybrid_rwkv7_pool_diffattn_s5_hmoe'


def rms_norm(x, g, eps=RMS_EPS):
    xf = x.astype(jnp.float32)
    y = xf * lax.rsqrt(jnp.mean(xf * xf, axis=-1, keepdims=True) + eps)
    return (y * g.astype(jnp.float32)).astype(x.dtype)


def token_shift(u):
    return jnp.pad(u, ((0, 0), (1, 0), (0, 0)))[:, :-1]


def rwkv7_recurrence(r, w, k, v, a, b):
    bsz, _, n_heads, n = r.shape
    seq = tuple(jnp.moveaxis(t.astype(jnp.float32), 1, 0) for t in (r, w, k, v, a, b))

    def step(state, inp):
        r_t, w_t, k_t, v_t, a_t, b_t = inp
        sa = jnp.einsum('bhij,bhj->bhi', state, a_t)
        state = (state * w_t[:, :, None, :] + sa[..., None] * b_t[:, :, None, :]
                 + v_t[..., None] * k_t[:, :, None, :])
        return state, jnp.einsum('bhij,bhj->bhi', state, r_t)

    s0 = jnp.zeros((bsz, n_heads, n, n), jnp.float32)
    _, y = lax.scan(step, s0, seq)
    return jnp.moveaxis(y, 0, 1)


def multiscale_pool(u, pool_w, pool_scale):
    bsz, t_len, _ = u.shape
    uf = u.astype(jnp.float32).reshape(bsz, t_len, POOL_GROUPS, POOL_GROUP_DIM)
    csum = jnp.cumsum(uf, axis=1)
    n_seen = jnp.arange(1, t_len + 1, dtype=jnp.float32)
    diffs = []
    for gi, win in enumerate(POOL_WINDOWS):
        c = csum[:, :, gi]
        lagged = jnp.pad(c, ((0, 0), (win, 0), (0, 0)))[:, :t_len]
        mean = (c - lagged) / jnp.minimum(n_seen, float(win))[None, :, None]
        diffs.append(mean - uf[:, :, gi])
    d = jnp.stack(diffs, axis=2).astype(u.dtype)
    y = jnp.einsum('btgc,gcd->btgd', d, pool_w)
    return y.reshape(bsz, t_len, POOL_WIDTH) * pool_scale


def even_mixer(h, v_first, w_in, mu, w0, w2, a0, a2, g2, k_k, k_a, r_k, ln_g, ln_b, vres,
               pool_w, pool_scale, w_out):
    bsz, t_len, _ = h.shape
    u = h @ w_in
    u_rw, u_pool = u[..., :RW_SHIFT_COLS], u[..., RW_SHIFT_COLS:]
    m = u_rw + (token_shift(u_rw) - u_rw) * mu
    r, k, v, dw, da, dg = jnp.split(m, RW_SPLITS, axis=-1)
    w = -jax.nn.softplus(-(w0 + jnp.tanh(dw) @ w2)) - 0.5
    decay = jnp.exp(-jnp.exp(w.astype(jnp.float32)))
    a = jax.nn.sigmoid(a0 + da @ a2)
    g = jax.nn.sigmoid(dg) @ g2
    if v_first is None:
        v_first = v
    else:
        v0, v1, v2 = vres
        v = v + (v_first - v) * jax.nn.sigmoid(v0 + (v @ v1) @ v2)

    def heads(t):
        return t.reshape(bsz, t_len, RW_HEADS, RW_HEAD_DIM)

    kk = heads(k * k_k).astype(jnp.float32)
    kk = kk / jnp.maximum(jnp.sqrt(jnp.sum(kk * kk, axis=-1, keepdims=True)), 1e-12)
    k = k * (1.0 + (a - 1.0) * k_a)
    rh, kh, vh, ah = heads(r), heads(k), heads(v), heads(a)
    y = rwkv7_recurrence(rh, heads(decay), kh, vh, -kk, kk * ah.astype(jnp.float32))
    mean = jnp.mean(y, axis=-1, keepdims=True)
    var = jnp.mean(jnp.square(y - mean), axis=-1, keepdims=True)
    y = ((y - mean) * lax.rsqrt(var + RW_LN_EPS)).reshape(bsz, t_len, RW_WIDTH)
    y = (y * ln_g + ln_b).astype(h.dtype)
    bonus = jnp.sum(rh * kh * r_k, axis=-1, keepdims=True) * vh
    o_rw = (y + bonus.reshape(bsz, t_len, RW_WIDTH)) * g
    o_pool = multiscale_pool(u_pool, pool_w, pool_scale)
    return jnp.concatenate([o_rw, o_pool], axis=-1) @ w_out, v_first


def rope_tables(t_len, dim):
    inv_freq = ROPE_THETA ** (-jnp.arange(0, dim, 2, dtype=jnp.float32) / dim)
    ang = jnp.arange(t_len, dtype=jnp.float32)[:, None] * inv_freq[None, :]
    return jnp.cos(ang), jnp.sin(ang)


def apply_rope(x, cos, sin):
    xf = x.astype(jnp.float32)
    x1, x2 = jnp.split(xf, 2, axis=-1)
    c = cos[:, None, None, :]
    s = sin[:, None, None, :]
    return jnp.concatenate([x1 * c - x2 * s, x2 * c + x1 * s], axis=-1).astype(x.dtype)


def diff_attention(q, k, v, lam):
    t_len = q.shape[1]
    scale = DA_HEAD_DIM ** -0.5
    pos = jnp.arange(t_len)
    outs = []
    for s in range(0, t_len, Q_BLOCK):
        e = s + Q_BLOCK
        sc = jnp.einsum('bqhmd,bkhmd->bhmqk', q[:, s:e], k[:, :e]).astype(jnp.float32) * scale
        mask = pos[s:e, None] >= pos[None, :e]
        p = jax.nn.softmax(jnp.where(mask, sc, -jnp.inf), axis=-1)
        attn = p[:, :, 0] - lam * p[:, :, 1]
        outs.append(jnp.einsum('bhqk,bkhd->bqhd', attn.astype(v.dtype), v[:, :e]))
    return jnp.concatenate(outs, axis=1)


def s5_mixer(u, a_re, a_im, log_step, b_re, b_im, c_re, c_im, d_skip, w_glu):
    bsz, t_len, _ = u.shape
    f32 = jnp.float32
    uf = u.astype(f32).reshape(bsz, t_len, S5_GROUPS, S5_GROUP_DIM)
    lam = lax.complex(jnp.minimum(a_re.astype(f32), -1e-4), a_im.astype(f32))
    step = jnp.exp(log_step.astype(f32))
    lam_bar = jnp.exp(lam * step)
    b_bar = ((lam_bar - 1.0) / lam)[..., None] * lax.complex(b_re.astype(f32), b_im.astype(f32))
    c_mat = lax.complex(c_re.astype(f32), c_im.astype(f32))
    bu = jnp.einsum('gpc,btgc->btgp', b_bar, uf.astype(jnp.complex64))
    a_seq = jnp.broadcast_to(lam_bar, bu.shape)

    def combine(left, right):
        a_l, b_l = left
        a_r, b_r = right
        return a_r * a_l, a_r * b_l + b_r

    _, states = lax.associative_scan(combine, (a_seq, bu), axis=1)
    y = jnp.einsum('gcp,btgp->btgc', c_mat, states).real + d_skip.astype(f32) * uf
    z = jax.nn.gelu(y.reshape(bsz, t_len, S5_WIDTH)).astype(u.dtype)
    return z * jax.nn.sigmoid(z @ w_glu)


def odd_mixer(h, layer_idx, w_in, q_norm, k_norm, lam_q1, lam_k1, lam_q2, lam_k2, subln,
              a_re, a_im, log_step, b_re, b_im, c_re, c_im, d_skip, w_glu, w_out):
    bsz, t_len, _ = h.shape
    f32 = jnp.float32
    u = h @ w_in
    q, k, v, u_s5 = jnp.split(u, (DA_QK_COLS, 2 * DA_QK_COLS, 2 * DA_QK_COLS + DA_WIDTH), axis=-1)
    q = q.reshape(bsz, t_len, DA_HEADS, 2, DA_HEAD_DIM)
    k = k.reshape(bsz, t_len, DA_HEADS, 2, DA_HEAD_DIM)
    v = v.reshape(bsz, t_len, DA_HEADS, DA_VALUE_DIM)
    cos, sin = rope_tables(t_len, DA_HEAD_DIM)
    q = apply_rope(rms_norm(q, q_norm), cos, sin)
    k = apply_rope(rms_norm(k, k_norm), cos, sin)
    lam_init = 0.8 - 0.6 * math.exp(-0.3 * layer_idx)
    lam = (jnp.exp(jnp.sum(lam_q1.astype(f32) * lam_k1.astype(f32)))
           - jnp.exp(jnp.sum(lam_q2.astype(f32) * lam_k2.astype(f32))) + lam_init)
    o = diff_attention(q, k, v, lam)
    o = rms_norm(o, subln, SUBLN_EPS) * (1.0 - lam_init)
    o_attn = o.reshape(bsz, t_len, DA_WIDTH)
    o_ssm = s5_mixer(u_s5, a_re, a_im, log_step, b_re, b_im, c_re, c_im, d_skip, w_glu)
    return jnp.concatenate([o_attn, o_ssm], axis=-1) @ w_out


def routed_experts(hf, expert_idx, gates, w_gate, w_up, w_down):
    n_tok, d = hf.shape
    n_slot = n_tok * MOE_TOP_K
    flat_e = expert_idx.reshape(-1)
    flat_tok = jnp.arange(n_slot, dtype=jnp.int32) // MOE_TOP_K
    flat_g = gates.reshape(-1)
    order = jnp.argsort(flat_e)
    sorted_e = flat_e[order]
    counts = jnp.bincount(flat_e, length=MOE_EXPERTS)
    padded = (counts + MOE_BLOCK - 1) // MOE_BLOCK * MOE_BLOCK
    seg_start = jnp.cumsum(counts) - counts
    cum_padded = jnp.cumsum(padded)
    pad_start = cum_padded - padded
    dest = pad_start[sorted_e] + jnp.arange(n_slot, dtype=jnp.int32) - seg_start[sorted_e]
    n_blocks = -(-n_slot // MOE_BLOCK) + MOE_EXPERTS
    n_pad = n_blocks * MOE_BLOCK
    buf_tok = jnp.full((n_pad,), n_tok, jnp.int32).at[dest].set(flat_tok[order])
    buf_gate = jnp.zeros((n_pad,), jnp.float32).at[dest].set(flat_g[order])
    block_start = jnp.arange(n_blocks, dtype=cum_padded.dtype) * MOE_BLOCK
    block_expert = jnp.minimum(jnp.searchsorted(cum_padded, block_start, side='right'),
                               MOE_EXPERTS - 1).astype(jnp.int32)
    h_pad = jnp.concatenate([hf, jnp.zeros((1, d), hf.dtype)], axis=0)

    def run_block(args):
        tok, e, gt = args
        xb = h_pad[tok]
        hidden = jax.nn.silu(xb @ w_gate[e]) * (xb @ w_up[e])
        return (hidden @ w_down[e]) * gt[:, None].astype(xb.dtype)

    ys = lax.map(run_block, (buf_tok.reshape(n_blocks, MOE_BLOCK), block_expert,
                             buf_gate.reshape(n_blocks, MOE_BLOCK)))
    out = jnp.zeros((n_tok + 1, d), hf.dtype).at[buf_tok].add(ys.reshape(n_pad, d))
    return out[:n_tok]


def hier_moe(h, w_group, b_group, w_expert, b_expert, w_gate, w_up, w_down):
    bsz, t_len, d = h.shape
    n_tok = bsz * t_len
    hf = h.reshape(n_tok, d)
    rows = jnp.arange(n_tok)
    g_logits = (hf @ w_group + b_group).astype(jnp.float32)
    g_prob = jax.nn.softmax(g_logits, axis=-1)
    g_top = jnp.argmax(g_logits, axis=-1).astype(jnp.int32)
    g_w = g_prob[rows, g_top]
    e_logits = (hf @ w_expert + b_expert).astype(jnp.float32)
    e_logits = e_logits.reshape(n_tok, MOE_GROUPS, MOE_EXPERTS_PER_GROUP)[rows, g_top]
    e_prob = jax.nn.softmax(e_logits, axis=-1)
    top_p, top_j = lax.top_k(e_prob, MOE_TOP_K)
    gates = g_w[:, None] * top_p / jnp.sum(top_p, axis=-1, keepdims=True)
    expert_idx = g_top[:, None] * MOE_EXPERTS_PER_GROUP + top_j.astype(jnp.int32)
    y = routed_experts(hf, expert_idx, gates, w_gate, w_up, w_down)
    return y.reshape(bsz, t_len, d)


def setup_inputs(seed: int = 0) -> dict:
    key = jax.random.key(seed)
    keys = jax.random.split(key, 64)
    counter = [0]

    def nk():
        counter[0] += 1
        return keys[counter[0] - 1]

    f32 = jnp.float32

    def nrm(shape, scale):
        return scale * jax.random.normal(nk(), shape, f32)

    def gain(shape, base=1.0, noise=0.02):
        return base + noise * jax.random.normal(nk(), shape, f32)

    def unif(shape, lo, hi):
        return jax.random.uniform(nk(), shape, f32, lo, hi)

    d = D_MODEL
    res = (2.0 * DEPTH) ** -0.5
    return {
        'x': nrm((BATCH, SEQ, d), 1.0),
        'norm_mix_g': gain((DEPTH, d)),
        'norm_ffn_g': gain((DEPTH, d)),
        'even_w_in': nrm((N_EVEN, d, EVEN_IN), d ** -0.5),
        'rw_mu': unif((N_EVEN, RW_SHIFT_COLS), 0.0, 1.0),
        'rw_w0': unif((N_EVEN, RW_WIDTH), -6.0, -1.0),
        'rw_w2': nrm((N_EVEN, RW_DECAY_RANK, RW_WIDTH), RW_DECAY_RANK ** -0.5),
        'rw_a0': nrm((N_EVEN, RW_WIDTH), 0.1),
        'rw_a2': nrm((N_EVEN, RW_ICLR_RANK, RW_WIDTH), RW_ICLR_RANK ** -0.5),
        'rw_g2': nrm((N_EVEN, RW_GATE_RANK, RW_WIDTH), RW_GATE_RANK ** -0.5),
        'rw_k_k': gain((N_EVEN, RW_WIDTH), 0.85, 0.05),
        'rw_k_a': gain((N_EVEN, RW_WIDTH), 1.0, 0.05),
        'rw_r_k': nrm((N_EVEN, RW_HEADS, RW_HEAD_DIM), 0.1),
        'rw_ln_g': gain((N_EVEN, RW_WIDTH)),
        'rw_ln_b': nrm((N_EVEN, RW_WIDTH), 0.02),
        'rw_v0': nrm((N_VRES, RW_WIDTH), 0.1),
        'rw_v1': nrm((N_VRES, RW_WIDTH, RW_VRES_RANK), RW_WIDTH ** -0.5),
        'rw_v2': nrm((N_VRES, RW_VRES_RANK, RW_WIDTH), RW_VRES_RANK ** -0.5),
        'pool_w': nrm((N_EVEN, POOL_GROUPS, POOL_GROUP_DIM, POOL_GROUP_DIM), POOL_GROUP_DIM ** -0.5),
        'pool_scale': gain((N_EVEN, POOL_WIDTH)),
        'even_w_out': nrm((N_EVEN, EVEN_MIX, d), EVEN_MIX ** -0.5 * res),
        'odd_w_in': nrm((N_ODD, d, ODD_IN), d ** -0.5),
        'da_q_norm': gain((N_ODD, DA_HEAD_DIM)),
        'da_k_norm': gain((N_ODD, DA_HEAD_DIM)),
        'da_lam_q1': nrm((N_ODD, DA_HEAD_DIM), 0.1),
        'da_lam_k1': nrm((N_ODD, DA_HEAD_DIM), 0.1),
        'da_lam_q2': nrm((N_ODD, DA_HEAD_DIM), 0.1),
        'da_lam_k2': nrm((N_ODD, DA_HEAD_DIM), 0.1),
        'da_subln': gain((N_ODD, DA_VALUE_DIM)),
        's5_a_re': gain((N_ODD, S5_GROUPS, S5_STATE), -0.5, 0.01),
        's5_a_im': math.pi * jnp.arange(S5_STATE, dtype=f32) + nrm((N_ODD, S5_GROUPS, S5_STATE), 0.01),
        's5_log_step': unif((N_ODD, S5_GROUPS, S5_STATE), math.log(1e-3), math.log(1e-1)),
        's5_b_re': nrm((N_ODD, S5_GROUPS, S5_STATE, S5_GROUP_DIM), (2.0 * S5_GROUP_DIM) ** -0.5),
        's5_b_im': nrm((N_ODD, S5_GROUPS, S5_STATE, S5_GROUP_DIM), (2.0 * S5_GROUP_DIM) ** -0.5),
        's5_c_re': nrm((N_ODD, S5_GROUPS, S5_GROUP_DIM, S5_STATE), (2.0 * S5_STATE) ** -0.5),
        's5_c_im': nrm((N_ODD, S5_GROUPS, S5_GROUP_DIM, S5_STATE), (2.0 * S5_STATE) ** -0.5),
        's5_d': nrm((N_ODD, S5_GROUPS, S5_GROUP_DIM), 1.0),
        's5_w_glu': nrm((N_ODD, S5_WIDTH, S5_WIDTH), S5_WIDTH ** -0.5),
        'odd_w_out': nrm((N_ODD, ODD_MIX, d), ODD_MIX ** -0.5 * res),
        'moe_w_group': nrm((DEPTH, d, MOE_GROUPS), d ** -0.5),
        'moe_b_group': nrm((DEPTH, MOE_GROUPS), 0.01),
        'moe_w_expert': nrm((DEPTH, d, MOE_EXPERTS), d ** -0.5),
        'moe_b_expert': nrm((DEPTH, MOE_EXPERTS), 0.01),
        'moe_w_gate': nrm((DEPTH, MOE_EXPERTS, d, MOE_HIDDEN), d ** -0.5),
        'moe_w_up': nrm((DEPTH, MOE_EXPERTS, d, MOE_HIDDEN), d ** -0.5),
        'moe_w_down': nrm((DEPTH, MOE_EXPERTS, MOE_HIDDEN, d), MOE_HIDDEN ** -0.5 * res),
    }


def reference(x, norm_mix_g, norm_ffn_g,
              even_w_in, rw_mu, rw_w0, rw_w2, rw_a0, rw_a2, rw_g2, rw_k_k, rw_k_a, rw_r_k,
              rw_ln_g, rw_ln_b, rw_v0, rw_v1, rw_v2, pool_w, pool_scale, even_w_out,
              odd_w_in, da_q_norm, da_k_norm, da_lam_q1, da_lam_k1, da_lam_q2, da_lam_k2, da_subln,
              s5_a_re, s5_a_im, s5_log_step, s5_b_re, s5_b_im, s5_c_re, s5_c_im, s5_d, s5_w_glu,
              odd_w_out,
              moe_w_group, moe_b_group, moe_w_expert, moe_b_expert, moe_w_gate, moe_w_up, moe_w_down):
    v_first = None
    for layer in range(DEPTH):
        i = layer // 2
        h = rms_norm(x, norm_mix_g[layer])
        if layer % 2 == 0:
            vres = None if v_first is None else (rw_v0[i - 1], rw_v1[i - 1], rw_v2[i - 1])
            mix, v_first = even_mixer(h, v_first, even_w_in[i], rw_mu[i], rw_w0[i], rw_w2[i],
                                      rw_a0[i], rw_a2[i], rw_g2[i], rw_k_k[i], rw_k_a[i], rw_r_k[i],
                                      rw_ln_g[i], rw_ln_b[i], vres, pool_w[i], pool_scale[i],
                                      even_w_out[i])
        else:
            mix = odd_mixer(h, layer, odd_w_in[i], da_q_norm[i], da_k_norm[i], da_lam_q1[i],
                            da_lam_k1[i], da_lam_q2[i], da_lam_k2[i], da_subln[i], s5_a_re[i],
                            s5_a_im[i], s5_log_step[i], s5_b_re[i], s5_b_im[i], s5_c_re[i],
                            s5_c_im[i], s5_d[i], s5_w_glu[i], odd_w_out[i])
        x = x + mix
        h = rms_norm(x, norm_ffn_g[layer])
        x = x + hier_moe(h, moe_w_group[layer], moe_b_group[layer], moe_w_expert[layer],
                         moe_b_expert[layer], moe_w_gate[layer], moe_w_up[layer], moe_w_down[layer])
    return x
```

```python
import functools
import math

import jax
import jax.numpy as jnp
from jax import lax
from jax.experimental import pallas as pl
from jax.experimental.pallas import tpu as pltpu

F32 = jnp.float32
BF16 = jnp.bfloat16

LANES = 128
SUBLANES = 8
VMEM_BYTES_V7X = 64 * 1024 * 1024

D_MODEL = 1024
HEAD = 64
RW_WIDTH = 512
RW_SHIFT_COLS = 3 * RW_WIDTH + 64 + 64 + 128
RW_LN_EPS = 64e-5
POOL_WINDOWS = (2, 4, 8, 16)
POOL_HALO = 16
DA_HEADS = 4
SUBLN_EPS = 1e-5
ROPE_THETA = 10000.0
S5_GROUP_DIM = 16
S5_STATE = 64
S5_TILES = 4
S5_BATCH_GROUP = SUBLANES
MOE_GROUPS = 4
MOE_PER_GROUP = 8
MOE_EXPERTS = 32
MOE_ROWS = 256
RMS_EPS = 1e-6
CHUNK = 64
NEG_BIG = -1e30


def _params(semantics, vmem_mib):
    return pltpu.CompilerParams(dimension_semantics=semantics,
                                vmem_limit_bytes=min(vmem_mib * 1024 * 1024, VMEM_BYTES_V7X - 8 * 1024 * 1024))


def _bdot(a, b):
    return jnp.dot(a.astype(BF16), b.astype(BF16), preferred_element_type=F32)


def _bdot_nt(a, b):
    return lax.dot_general(a.astype(BF16), b.astype(BF16), (((1,), (1,)), ((), ())),
                           preferred_element_type=F32)


def _split_dot(x, m_bf16):
    hi = x.astype(BF16)
    lo = (x - hi.astype(F32)).astype(BF16)
    return (jnp.dot(hi, m_bf16, preferred_element_type=F32)
            + jnp.dot(lo, m_bf16, preferred_element_type=F32))


def _seg_ones(width):
    r = lax.broadcasted_iota(jnp.int32, (LANES, LANES), 0)
    c = lax.broadcasted_iota(jnp.int32, (LANES, LANES), 1)
    sh = int(math.log2(width))
    return ((r >> sh) == (c >> sh)).astype(BF16)


def _segsum(x, seg):
    tiles = [_split_dot(x[:, j * LANES:(j + 1) * LANES], seg) for j in range(x.shape[1] // LANES)]
    return tiles[0] if len(tiles) == 1 else jnp.concatenate(tiles, axis=1)


def _sigmoid(x):
    return 1.0 / (1.0 + jnp.exp(-x))


def _norm_matmul_kernel(x_ref, g_ref, w_ref, o_ref):
    x = x_ref[...]
    ms = jnp.mean(x * x, axis=-1, keepdims=True)
    h = x * lax.rsqrt(ms + RMS_EPS) * g_ref[...]
    o_ref[...] = jnp.dot(h.astype(BF16), w_ref[...], preferred_element_type=F32)


def _norm_matmul(x, g, w_bf16, tm=512):
    n, d = x.shape
    c = w_bf16.shape[1]
    tm = min(tm, n)
    return pl.pallas_call(
        _norm_matmul_kernel,
        out_shape=jax.ShapeDtypeStruct((n, c), F32),
        grid=(n // tm,),
        in_specs=[pl.BlockSpec((tm, d), lambda i: (i, 0)),
                  pl.BlockSpec((1, d), lambda i: (0, 0)),
                  pl.BlockSpec((d, c), lambda i: (0, 0))],
        out_specs=pl.BlockSpec((tm, c), lambda i: (i, 0)),
        compiler_params=_params(("parallel",), 48),
        name="norm_matmul",
    )(x, g.reshape(1, d), w_bf16)


def _even_prep_kernel(has_vres, tb, *refs):
    if has_vres:
        (u_ref, mu_ref, w0_ref, a0_ref, wa_ref, g2_ref, kk_ref, ka_ref, pw_ref, ps_ref,
         vf_ref, v0_ref, v1_ref, v2_ref,
         r_o, lw_o, k_o, v_o, a_o, b_o, g_o, pool_o, carry) = refs
    else:
        (u_ref, mu_ref, w0_ref, a0_ref, wa_ref, g2_ref, kk_ref, ka_ref, pw_ref, ps_ref,
         r_o, lw_o, k_o, v_o, a_o, b_o, g_o, pool_o, carry) = refs
    ti = pl.program_id(1)

    @pl.when(ti == 0)
    def _():
        carry[...] = jnp.zeros_like(carry)

    u = u_ref[0]
    ext = jnp.concatenate([carry[...], u], axis=0)
    carry[...] = u[tb - POOL_HALO:, :]

    p1 = ext[:, RW_SHIFT_COLS:]
    p2 = p1 + pltpu.roll(p1, 1, 0)
    p4 = p2 + pltpu.roll(p2, 2, 0)
    p8 = p4 + pltpu.roll(p4, 4, 0)
    p16 = p8 + pltpu.roll(p8, 8, 0)
    lane = lax.broadcasted_iota(jnp.int32, (1, RW_WIDTH), 1)
    grp = lane >> 7
    sums = jnp.where(grp == 0, p2, jnp.where(grp == 1, p4, jnp.where(grp == 2, p8, p16)))[POOL_HALO:]
    win = jnp.where(grp == 0, 2.0, jnp.where(grp == 1, 4.0, jnp.where(grp == 2, 8.0, 16.0)))
    n_seen = (ti * tb + lax.broadcasted_iota(jnp.int32, (tb, 1), 0) + 1).astype(F32)
    d = sums / jnp.minimum(n_seen, win) - u[:, RW_SHIFT_COLS:]
    pool_o[0] = _bdot(d, pw_ref[...]) * ps_ref[...]

    u_rw = u[:, :RW_SHIFT_COLS]
    prev = pltpu.roll(ext[:, :RW_SHIFT_COLS], 1, 0)[POOL_HALO:]
    m = u_rw + (prev - u_rw) * mu_ref[...]
    r = m[:, :RW_WIDTH]
    k = m[:, RW_WIDTH:2 * RW_WIDTH]
    v = m[:, 2 * RW_WIDTH:3 * RW_WIDTH]
    dwa = m[:, 3 * RW_WIDTH:3 * RW_WIDTH + LANES]
    dg = m[:, 3 * RW_WIDTH + LANES:]
    l128 = lax.broadcasted_iota(jnp.int32, (1, LANES), 1)
    dwa = jnp.where(l128 < HEAD, jnp.tanh(dwa), dwa)
    x12 = _bdot(dwa, wa_ref[...])
    z = -(w0_ref[...] + x12[:, :RW_WIDTH])
    softplus = jnp.maximum(z, 0.0) + jnp.log(1.0 + jnp.exp(-jnp.abs(z)))
    lw = -jnp.exp(-softplus - 0.5)
    a_i = _sigmoid(a0_ref[...] + x12[:, RW_WIDTH:])
    g_o[0] = _bdot(_sigmoid(dg), g2_ref[...])
    if has_vres:
        gate_v = _sigmoid(v0_ref[...] + _bdot(_bdot(v, v1_ref[...]), v2_ref[...]))
        v = v + (vf_ref[0] - v) * gate_v
    kk = k * kk_ref[...]
    ss = _segsum(kk * kk, _seg_ones(HEAD))
    kk = kk / jnp.maximum(jnp.sqrt(ss), 1e-12)
    r_o[0] = r
    lw_o[0] = lw
    k_o[0] = k * (1.0 + (a_i - 1.0) * ka_ref[...])
    v_o[0] = v
    a_o[0] = -kk
    b_o[0] = kk * a_i


def _even_prep(u, prm, v_first, tb=256):
    bsz, t_len, cin = u.shape
    tb = min(tb, t_len)
    has_vres = v_first is not None
    row = lambda a: a.reshape(1, -1)
    full = lambda shape: pl.BlockSpec(shape, lambda b, t: (0,) * len(shape))
    seq = pl.BlockSpec((1, tb, RW_WIDTH), lambda b, t: (b, t, 0))
    ins = [u, row(prm["mu"]), row(prm["w0"]), row(prm["a0"]), prm["wa"], prm["g2"], row(prm["k_k"]),
           row(prm["k_a"]), prm["pool_bd"], row(prm["pool_scale"])]
    specs = [pl.BlockSpec((1, tb, cin), lambda b, t: (b, t, 0)), full((1, RW_SHIFT_COLS)),
             full((1, RW_WIDTH)), full((1, RW_WIDTH)), full((LANES, 2 * RW_WIDTH)),
             full((LANES, RW_WIDTH)), full((1, RW_WIDTH)), full((1, RW_WIDTH)),
             full((RW_WIDTH, RW_WIDTH)), full((1, RW_WIDTH))]
    if has_vres:
        ins += [v_first, row(prm["v0"]), prm["v1"], prm["v2"]]
        specs += [seq, full((1, RW_WIDTH)), full((RW_WIDTH, LANES)), full((LANES, RW_WIDTH))]
    out = jax.ShapeDtypeStruct((bsz, t_len, RW_WIDTH), F32)
    return pl.pallas_call(
        functools.partial(_even_prep_kernel, has_vres, tb),
        out_shape=(out,) * 8,
        grid=(bsz, t_len // tb),
        in_specs=specs,
        out_specs=(seq,) * 8,
        scratch_shapes=[pltpu.VMEM((POOL_HALO, cin), F32)],
        compiler_params=_params(("parallel", "arbitrary"), 48),
        name="even_prep",
    )(*ins)


def _rwkv_chunk_kernel(r_ref, lw_ref, k_ref, v_ref, a_ref, b_ref, y_ref, s_ref):
    ci = pl.program_id(1)

    @pl.when(ci == 0)
    def _():
        s_ref[...] = jnp.zeros_like(s_ref)

    L = CHUNK
    tri = (lax.broadcasted_iota(jnp.int32, (L, L), 0) >= lax.broadcasted_iota(jnp.int32, (L, L), 1)).astype(BF16)
    lw = lw_ref[0]
    c = _split_dot_lhs(tri, lw)
    e_pos = jnp.exp(c)
    e_neg = jnp.exp(-c)
    a_all = jnp.exp(c - lw) * a_ref[0]
    b_all = b_ref[0] * e_neg
    k_all = k_ref[0] * e_neg
    r_all = r_ref[0] * e_pos
    v_all = v_ref[0]
    g_all = e_pos[L - 1:L, :]

    lane = lax.broadcasted_iota(jnp.int32, (1, LANES), 1)
    m_a = lane < HEAD
    t_idx = lax.broadcasted_iota(jnp.int32, (L, LANES), 0)
    s_idx = lax.broadcasted_iota(jnp.int32, (L, LANES), 1) & (HEAD - 1)
    strict = t_idx > s_idx
    incl = t_idx >= s_idx
    r128 = lax.broadcasted_iota(jnp.int32, (LANES, LANES), 0)
    c128 = lax.broadcasted_iota(jnp.int32, (LANES, LANES), 1)
    eye = (r128 == c128).astype(F32)
    same_head = (r128 >> 6) == (c128 >> 6)

    def only_a(x):
        return jnp.where(m_a, x, 0.0)

    def only_b(x):
        return jnp.where(m_a, 0.0, x)

    def stack(x):
        return jnp.concatenate([only_a(x), only_b(x)], axis=0)

    def stack_sw(x):
        return jnp.concatenate([only_b(x), only_a(x)], axis=0)

    for p in range(RW_WIDTH // LANES):
        sl = slice(p * LANES, (p + 1) * LANES)
        at, bt, kt, rt, vv, gl = a_all[:, sl], b_all[:, sl], k_all[:, sl], r_all[:, sl], v_all[:, sl], g_all[:, sl]
        bk = jnp.concatenate([bt, kt], axis=0)
        kb = jnp.concatenate([kt, bt], axis=0)
        s_a = _bdot_nt(jnp.concatenate([only_a(at), only_a(rt)], axis=0), bk)
        s_b = _bdot_nt(jnp.concatenate([only_b(at), only_b(rt)], axis=0), kb)
        m_ha = jnp.where(strict, s_a[:L], 0.0)
        n_ha = jnp.where(incl, s_a[L:], 0.0)
        m_hb = jnp.where(strict, s_b[:L], 0.0)
        n_hb = jnp.where(incl, s_b[L:], 0.0)
        bdm = jnp.concatenate([only_a(m_ha), only_b(m_hb)], axis=0)
        t_inv = eye + bdm
        m_pow = bdm
        for _ in range(int(math.log2(L)) - 1):
            m_pow = _bdot(m_pow, m_pow)
            t_inv = t_inv + _bdot(t_inv, m_pow)
        t_pk = t_inv[:L] + t_inv[L:]
        ak_sw = jnp.where(m_a, m_hb, m_ha)
        w = _bdot(ak_sw, stack_sw(vv))
        au = _bdot(t_pk, jnp.concatenate([stack(at), stack(w)], axis=1))
        a_hat, u_hat = au[:, :LANES], au[:, LANES:]
        n_rb = jnp.where(m_a, n_ha, n_hb)
        n_rk_sw = jnp.where(m_a, n_hb, n_ha)
        rhs = jnp.concatenate([
            jnp.concatenate([stack(u_hat), stack(a_hat)], axis=1),
            jnp.concatenate([stack_sw(vv), jnp.zeros((LANES, LANES), F32)], axis=1)], axis=0)
        yr = _bdot(jnp.concatenate([n_rb, n_rk_sw], axis=1), rhs)
        y_hat = yr[:, :LANES]
        r_hat = rt + yr[:, LANES:]
        g_p = _bdot(a_hat.T, bt)
        g_q = _bdot(jnp.concatenate([u_hat, vv], axis=0).T, bk)
        p_bd = (eye + jnp.where(same_head, g_p, 0.0)) * gl
        q_pk = jnp.where(m_a, g_q[:L], g_q[L:]) * gl
        s0 = s_ref[p]
        y_ref[0, :, sl] = _bdot_nt(r_hat, stack(s0)) + y_hat
        s_ref[p] = _bdot(s0, p_bd) + q_pk


def _split_dot_lhs(m_bf16, x):
    hi = x.astype(BF16)
    lo = (x - hi.astype(F32)).astype(BF16)
    return (jnp.dot(m_bf16, hi, preferred_element_type=F32)
            + jnp.dot(m_bf16, lo, preferred_element_type=F32))


def _rwkv_chunk(r, lw, k, v, a, b):
    bsz, t_len, _ = r.shape
    seq = pl.BlockSpec((1, CHUNK, RW_WIDTH), lambda bi, ci: (bi, ci, 0))
    return pl.pallas_call(
        _rwkv_chunk_kernel,
        out_shape=jax.ShapeDtypeStruct((bsz, t_len, RW_WIDTH), F32),
        grid=(bsz, t_len // CHUNK),
        in_specs=[seq] * 6,
        out_specs=seq,
        scratch_shapes=[pltpu.VMEM((RW_WIDTH // LANES, HEAD, LANES), F32)],
        compiler_params=_params(("parallel", "arbitrary"), 32),
        name="rwkv_chunk",
    )(r, lw, k, v, a, b)


def _even_out_kernel(x_ref, y_ref, r_ref, k_ref, v_ref, g_ref, p_ref, lng_ref, lnb_ref, rk_ref, wo_ref, o_ref):
    seg = _seg_ones(HEAD)
    y = y_ref[...]
    mean = _segsum(y, seg) * (1.0 / HEAD)
    yc = y - mean
    var = _segsum(yc * yc, seg) * (1.0 / HEAD)
    yn = yc * lax.rsqrt(var + RW_LN_EPS) * lng_ref[...] + lnb_ref[...]
    bonus = _segsum(r_ref[...] * k_ref[...] * rk_ref[...], seg) * v_ref[...]
    o_rw = (yn + bonus) * g_ref[...]
    cat = jnp.concatenate([o_rw, p_ref[...]], axis=1)
    o_ref[...] = x_ref[...] + _bdot(cat, wo_ref[...])


def _even_out(x, y, r, k, v, g, pool, prm, tm=512):
    n, d = x.shape
    tm = min(tm, n)
    half = pl.BlockSpec((tm, RW_WIDTH), lambda i: (i, 0))
    vec = pl.BlockSpec((1, RW_WIDTH), lambda i: (0, 0))
    fl = lambda a: a.reshape(n, RW_WIDTH)
    return pl.pallas_call(
        _even_out_kernel,
        out_shape=jax.ShapeDtypeStruct((n, d), F32),
        grid=(n // tm,),
        in_specs=[pl.BlockSpec((tm, d), lambda i: (i, 0))] + [half] * 6 + [vec] * 3
                 + [pl.BlockSpec((d, d), lambda i: (0, 0))],
        out_specs=pl.BlockSpec((tm, d), lambda i: (i, 0)),
        compiler_params=_params(("parallel",), 48),
        name="even_out",
    )(x, fl(y), fl(r), fl(k), fl(v), fl(g), fl(pool), prm["ln_g"].reshape(1, -1), prm["ln_b"].reshape(1, -1),
      prm["r_k"].reshape(1, -1), prm["w_out"])


def _qk_prep_kernel(u_ref, gq_ref, gk_ref, cos_ref, sin_ref, q_o, k_o):
    x = u_ref[0]
    cos = cos_ref[...]
    sin = sin_ref[...]
    seg = _seg_ones(HEAD)
    lane = lax.broadcasted_iota(jnp.int32, (1, LANES), 1)
    first = (lane & (HEAD - 1)) < HEAD // 2
    nq = RW_WIDTH // LANES
    for j in range(2 * nq):
        xt = x[:, j * LANES:(j + 1) * LANES]
        ms = _split_dot(xt * xt, seg) * (1.0 / HEAD)
        gain = gq_ref[...] if j < nq else gk_ref[...]
        xn = xt * lax.rsqrt(ms + RMS_EPS) * gain
        partner = jnp.where(first, pltpu.roll(xn, LANES - HEAD // 2, 1), pltpu.roll(xn, HEAD // 2, 1))
        out = xn * cos + partner * sin
        if j < nq:
            q_o[0, :, j * LANES:(j + 1) * LANES] = (out * (HEAD ** -0.5)).astype(BF16)
        else:
            k_o[0, :, (j - nq) * LANES:(j - nq + 1) * LANES] = out.astype(BF16)


def _qk_prep(u, q_norm, k_norm, tb=256):
    bsz, t_len, _ = u.shape
    tb = min(tb, t_len)
    inv_freq = ROPE_THETA ** (-jnp.arange(0, HEAD, 2, dtype=F32) / HEAD)
    ang = jnp.arange(t_len, dtype=F32)[:, None] * inv_freq[None, :]
    cos = jnp.tile(jnp.cos(ang), (1, LANES // (HEAD // 2)))
    sin_half = jnp.concatenate([-jnp.sin(ang), jnp.sin(ang)], axis=1)
    sin = jnp.tile(sin_half, (1, LANES // HEAD))
    tile2 = lambda g: jnp.tile(g, LANES // HEAD).reshape(1, LANES)
    out = jax.ShapeDtypeStruct((bsz, t_len, RW_WIDTH), BF16)
    tab = pl.BlockSpec((tb, LANES), lambda b, t: (t, 0))
    vec = pl.BlockSpec((1, LANES), lambda b, t: (0, 0))
    seq = pl.BlockSpec((1, tb, RW_WIDTH), lambda b, t: (b, t, 0))
    return pl.pallas_call(
        _qk_prep_kernel,
        out_shape=(out, out),
        grid=(bsz, t_len // tb),
        in_specs=[pl.BlockSpec((1, tb, 2 * RW_WIDTH), lambda b, t: (b, t, 0)), vec, vec, tab, tab],
        out_specs=(seq, seq),
        compiler_params=_params(("parallel", "parallel"), 32),
        name="qk_prep",
    )(u, tile2(q_norm), tile2(k_norm), cos, sin)


def _diff_attn_kernel(tq, out_scale, q_ref, k_ref, v_ref, lam_ref, sub_ref, o_ref):
    qi = pl.program_id(2)
    q = q_ref[0]
    k = k_ref[0]
    t_len = k.shape[0]
    lane = lax.broadcasted_iota(jnp.int32, (1, LANES), 1)
    m_a = lane < HEAD
    zero = jnp.zeros_like(q)
    s1 = lax.dot_general(jnp.where(m_a, q, zero), k, (((1,), (1,)), ((), ())), preferred_element_type=F32)
    s2 = lax.dot_general(jnp.where(m_a, zero, q), k, (((1,), (1,)), ((), ())), preferred_element_type=F32)
    qpos = qi * tq + lax.broadcasted_iota(jnp.int32, (tq, t_len), 0)
    kpos = lax.broadcasted_iota(jnp.int32, (tq, t_len), 1)
    ok = qpos >= kpos

    def softmax(s):
        s = jnp.where(ok, s, NEG_BIG)
        e = jnp.exp(s - jnp.max(s, axis=-1, keepdims=True))
        return e / jnp.sum(e, axis=-1, keepdims=True)

    attn = softmax(s1) - lam_ref[...] * softmax(s2)
    o = _bdot(attn, v_ref[0])
    ms = jnp.mean(o * o, axis=-1, keepdims=True)
    o_ref[0] = o * lax.rsqrt(ms + SUBLN_EPS) * sub_ref[...] * out_scale


def _diff_attn(q, k, u, lam, subln, out_scale, tq=256):
    bsz, t_len, _ = q.shape
    tq = min(tq, t_len)
    v_off = 2 * RW_WIDTH // LANES
    return pl.pallas_call(
        functools.partial(_diff_attn_kernel, tq, out_scale),
        out_shape=jax.ShapeDtypeStruct((bsz, t_len, RW_WIDTH), F32),
        grid=(bsz, DA_HEADS, t_len // tq),
        in_specs=[pl.BlockSpec((1, tq, LANES), lambda b, h, i: (b, i, h)),
                  pl.BlockSpec((1, t_len, LANES), lambda b, h, i: (b, 0, h)),
                  pl.BlockSpec((1, t_len, LANES), lambda b, h, i: (b, 0, v_off + h)),
                  pl.BlockSpec((1, 1), lambda b, h, i: (0, 0)),
                  pl.BlockSpec((1, LANES), lambda b, h, i: (0, 0))],
        out_specs=pl.BlockSpec((1, tq, LANES), lambda b, h, i: (b, i, h)),
        compiler_params=_params(("parallel", "parallel", "arbitrary"), 48),
        name="diff_attn",
    )(q, k, u, lam.reshape(1, 1), subln.reshape(1, LANES))


def _s5_kernel(tb, u_ref, bt_ref, lre_ref, lim_ref, ct_ref, d_ref, wg_ref, o_ref, xs, st):
    ti = pl.program_id(1)

    @pl.when(ti == 0)
    def _():
        st[...] = jnp.zeros_like(st)

    rows = tb * SUBLANES
    half = RW_WIDTH
    u = u_ref[0].reshape(rows, RW_WIDTH)
    for j in range(S5_TILES):
        xs[:, 2 * half * j:2 * half * (j + 1)] = _bdot(u[:, j * LANES:(j + 1) * LANES], bt_ref[j])
    for j in range(S5_TILES):
        re = slice(2 * half * j, 2 * half * j + half)
        im = slice(2 * half * j + half, 2 * half * (j + 1))
        lr = jnp.broadcast_to(lre_ref[:, half * j:half * (j + 1)], (SUBLANES, half))
        li = jnp.broadcast_to(lim_ref[:, half * j:half * (j + 1)], (SUBLANES, half))

        def step(t, carry, re=re, im=im, lr=lr, li=li):
            xr, xi = carry
            row = pl.multiple_of(t * SUBLANES, SUBLANES)
            nr = lr * xr - li * xi + xs[pl.ds(row, SUBLANES), re]
            ni = lr * xi + li * xr + xs[pl.ds(row, SUBLANES), im]
            xs[pl.ds(row, SUBLANES), re] = nr
            xs[pl.ds(row, SUBLANES), im] = ni
            return nr, ni

        xr, xi = lax.fori_loop(0, tb, step, (st[:, re], st[:, im]), unroll=4)
        st[:, re] = xr
        st[:, im] = xi
    y = jnp.concatenate([_bdot(xs[:, 2 * half * j:2 * half * (j + 1)], ct_ref[j]) for j in range(S5_TILES)], axis=1)
    y = y + d_ref[...] * u
    z = 0.5 * y * (1.0 + jnp.tanh(math.sqrt(2.0 / math.pi) * (y + 0.044715 * (y * y * y))))
    o = z * _sigmoid(_bdot(z, wg_ref[...]))
    o_ref[0] = o.reshape(tb, SUBLANES, RW_WIDTH)


def _s5(u5, prm, tb=128):
    ng, t_len, _, _ = u5.shape
    tb = min(tb, t_len)
    full = lambda shape: pl.BlockSpec(shape, lambda g, t: (0,) * len(shape))
    n_state = 2 * RW_WIDTH * S5_TILES
    return pl.pallas_call(
        functools.partial(_s5_kernel, tb),
        out_shape=jax.ShapeDtypeStruct(u5.shape, F32),
        grid=(ng, t_len // tb),
        in_specs=[pl.BlockSpec((1, tb, SUBLANES, RW_WIDTH), lambda g, t: (g, t, 0, 0)),
                  full((S5_TILES, LANES, 2 * RW_WIDTH)), full((1, n_state // 2)), full((1, n_state // 2)),
                  full((S5_TILES, 2 * RW_WIDTH, LANES)), full((1, RW_WIDTH)), full((RW_WIDTH, RW_WIDTH))],
        out_specs=pl.BlockSpec((1, tb, SUBLANES, RW_WIDTH), lambda g, t: (g, t, 0, 0)),
        scratch_shapes=[pltpu.VMEM((tb * SUBLANES, n_state), F32), pltpu.VMEM((SUBLANES, n_state), F32)],
        compiler_params=_params(("parallel", "arbitrary"), 48),
        name="s5_scan",
    )(u5, prm["bt"], prm["lam_re"], prm["lam_im"], prm["ct"], prm["d"].reshape(1, -1), prm["w_glu"])


def _odd_out_kernel(x_ref, a_ref, s_ref, wo_ref, o_ref):
    cat = jnp.concatenate([a_ref[...], s_ref[...]], axis=1)
    o_ref[...] = x_ref[...] + _bdot(cat, wo_ref[...])


def _odd_out(x, o_attn, o_ssm, w_out, tm=512):
    n, d = x.shape
    tm = min(tm, n)
    half = pl.BlockSpec((tm, RW_WIDTH), lambda i: (i, 0))
    return pl.pallas_call(
        _odd_out_kernel,
        out_shape=jax.ShapeDtypeStruct((n, d), F32),
        grid=(n // tm,),
        in_specs=[pl.BlockSpec((tm, d), lambda i: (i, 0)), half, half, pl.BlockSpec((d, d), lambda i: (0, 0))],
        out_specs=pl.BlockSpec((tm, d), lambda i: (i, 0)),
        compiler_params=_params(("parallel",), 32),
        name="odd_out",
    )(x, o_attn.reshape(n, RW_WIDTH), o_ssm.reshape(n, RW_WIDTH), w_out)


def _router_kernel(x_ref, g_ref, whi_ref, wlo_ref, b_ref, h_o, route_o):
    x = x_ref[...]
    ms = jnp.mean(x * x, axis=-1, keepdims=True)
    h = x * lax.rsqrt(ms + RMS_EPS) * g_ref[...]
    h_o[...] = h
    h_hi = h.astype(BF16)
    h_lo = (h - h_hi.astype(F32)).astype(BF16)
    logits = (jnp.dot(h_hi, whi_ref[...], preferred_element_type=F32)
              + jnp.dot(h_hi, wlo_ref[...], preferred_element_type=F32)
              + jnp.dot(h_lo, whi_ref[...], preferred_element_type=F32)) + b_ref[...]
    lane = lax.broadcasted_iota(jnp.int32, logits.shape, 1).astype(F32)
    far = float(LANES)
    is_g = lane < MOE_GROUPS
    g_max = jnp.max(jnp.where(is_g, logits, NEG_BIG), axis=-1, keepdims=True)
    g_sum = jnp.sum(jnp.where(is_g, jnp.exp(jnp.minimum(logits - g_max, 0.0)), 0.0), axis=-1, keepdims=True)
    g_top = jnp.min(jnp.where(is_g & (logits == g_max), lane, far), axis=-1, keepdims=True)
    lo = MOE_GROUPS + MOE_PER_GROUP * g_top
    in_grp = (lane >= lo) & (lane < lo + MOE_PER_GROUP)
    e1 = jnp.max(jnp.where(in_grp, logits, NEG_BIG), axis=-1, keepdims=True)
    i1 = jnp.min(jnp.where(in_grp & (logits == e1), lane, far), axis=-1, keepdims=True)
    rest = in_grp & (lane != i1)
    e2 = jnp.max(jnp.where(rest, logits, NEG_BIG), axis=-1, keepdims=True)
    i2 = jnp.min(jnp.where(rest & (logits == e2), lane, far), axis=-1, keepdims=True)
    ratio = jnp.exp(e2 - e1)
    gate1 = 1.0 / (g_sum * (1.0 + ratio))
    gate2 = gate1 * ratio
    route_o[...] = jnp.where(lane == 0, i1 - MOE_GROUPS,
                             jnp.where(lane == 1, i2 - MOE_GROUPS,
                                       jnp.where(lane == 2, gate1, jnp.where(lane == 3, gate2, 0.0))))


def _router(x, g, w_hi, w_lo, bias, tm=512):
    n, d = x.shape
    tm = min(tm, n)
    return pl.pallas_call(
        _router_kernel,
        out_shape=(jax.ShapeDtypeStruct((n, d), F32), jax.ShapeDtypeStruct((n, LANES), F32)),
        grid=(n // tm,),
        in_specs=[pl.BlockSpec((tm, d), lambda i: (i, 0)), pl.BlockSpec((1, d), lambda i: (0, 0)),
                  pl.BlockSpec((d, LANES), lambda i: (0, 0)), pl.BlockSpec((d, LANES), lambda i: (0, 0)),
                  pl.BlockSpec((1, LANES), lambda i: (0, 0))],
        out_specs=(pl.BlockSpec((tm, d), lambda i: (i, 0)), pl.BlockSpec((tm, LANES), lambda i: (i, 0))),
        compiler_params=_params(("parallel",), 32),
        name="moe_router",
    )(x, g.reshape(1, d), w_hi, w_lo, bias)


def _row_copy(table_ref, row, dst_ref, i, sem):
    return pltpu.make_async_copy(table_ref.at[pl.ds(row, 1)], dst_ref.at[pl.ds(i, 1)], sem)


def _gather_kernel(rows, idx_ref, tab_ref, o_ref, sem):
    def issue(i, c):
        _row_copy(tab_ref, idx_ref[0, 0, i], o_ref, i, sem.at[0]).start()
        return c

    lax.fori_loop(0, rows, issue, 0)

    def drain(i, c):
        _row_copy(tab_ref, 0, o_ref, i, sem.at[0]).wait()
        return c

    lax.fori_loop(0, rows, drain, 0)


def _gather_rows(table, idx, rows=MOE_ROWS):
    n_out = idx.shape[0]
    d = table.shape[1]
    nb = n_out // rows
    return pl.pallas_call(
        functools.partial(_gather_kernel, rows),
        out_shape=jax.ShapeDtypeStruct((n_out, d), table.dtype),
        grid=(nb,),
        in_specs=[pl.BlockSpec((1, 1, rows), lambda i: (i, 0, 0), memory_space=pltpu.SMEM),
                  pl.BlockSpec(memory_space=pl.ANY)],
        out_specs=pl.BlockSpec((rows, d), lambda i: (i, 0)),
        scratch_shapes=[pltpu.SemaphoreType.DMA((1,))],
        compiler_params=_params(("arbitrary",), 16),
        name="moe_gather",
    )(idx.reshape(nb, 1, rows), table)


def _moe_mlp_kernel(be_ref, x_ref, gate_ref, wg_ref, wu_ref, wd_ref, o_ref):
    del be_ref
    x = x_ref[...].astype(BF16)
    hg = jnp.dot(x, wg_ref[0], preferred_element_type=F32)
    hu = jnp.dot(x, wu_ref[0], preferred_element_type=F32)
    hid = hg * _sigmoid(hg) * hu
    o_ref[...] = jnp.dot(hid.astype(BF16), wd_ref[0], preferred_element_type=F32) * gate_ref[...]


def _moe_mlp(xs, gates, block_expert, w_gate, w_up, w_down):
    n_pad, d = xs.shape
    hid = w_gate.shape[2]
    nb = n_pad // MOE_ROWS
    grid_spec = pltpu.PrefetchScalarGridSpec(
        num_scalar_prefetch=1,
        grid=(nb,),
        in_specs=[pl.BlockSpec((MOE_ROWS, d), lambda i, be: (i, 0)),
                  pl.BlockSpec((MOE_ROWS, 1), lambda i, be: (i, 0)),
                  pl.BlockSpec((1, d, hid), lambda i, be: (be[i], 0, 0)),
                  pl.BlockSpec((1, d, hid), lambda i, be: (be[i], 0, 0)),
                  pl.BlockSpec((1, hid, d), lambda i, be: (be[i], 0, 0))],
        out_specs=pl.BlockSpec((MOE_ROWS, d), lambda i, be: (i, 0)),
    )
    return pl.pallas_call(
        _moe_mlp_kernel,
        out_shape=jax.ShapeDtypeStruct((n_pad, d), F32),
        grid_spec=grid_spec,
        compiler_params=_params(("arbitrary",), 32),
        name="moe_mlp",
    )(block_expert, xs, gates.reshape(n_pad, 1), w_gate, w_up, w_down)


def _combine_kernel(rows, pos_ref, x_ref, ys_ref, o_ref, buf0, buf1, sem):
    def issue(i, c):
        _row_copy(ys_ref, pos_ref[0, 0, 2 * i], buf0, i, sem.at[0]).start()
        _row_copy(ys_ref, pos_ref[0, 0, 2 * i + 1], buf1, i, sem.at[0]).start()
        return c

    lax.fori_loop(0, rows, issue, 0)

    def drain(i, c):
        _row_copy(ys_ref, 0, buf0, i, sem.at[0]).wait()
        _row_copy(ys_ref, 0, buf1, i, sem.at[0]).wait()
        return c

    lax.fori_loop(0, rows, drain, 0)
    o_ref[...] = x_ref[...] + buf0[...] + buf1[...]


def _combine(x, ys, pos, rows=256):
    n, d = x.shape
    rows = min(rows, n)
    nb = n // rows
    return pl.pallas_call(
        functools.partial(_combine_kernel, rows),
        out_shape=jax.ShapeDtypeStruct((n, d), F32),
        grid=(nb,),
        in_specs=[pl.BlockSpec((1, 1, 2 * rows), lambda i: (i, 0, 0), memory_space=pltpu.SMEM),
                  pl.BlockSpec((rows, d), lambda i: (i, 0)),
                  pl.BlockSpec(memory_space=pl.ANY)],
        out_specs=pl.BlockSpec((rows, d), lambda i: (i, 0)),
        scratch_shapes=[pltpu.VMEM((rows, d), F32), pltpu.VMEM((rows, d), F32), pltpu.SemaphoreType.DMA((1,))],
        compiler_params=_params(("arbitrary",), 16),
        name="moe_combine",
    )(pos.reshape(nb, 1, 2 * rows), x, ys)


def _hier_moe(x, g, w_group, b_group, w_expert, b_expert, w_gate, w_up, w_down):
    n, d = x.shape
    w_route = jnp.zeros((d, LANES), F32).at[:, :MOE_GROUPS].set(w_group)
    w_route = w_route.at[:, MOE_GROUPS:MOE_GROUPS + MOE_EXPERTS].set(w_expert)
    w_hi = w_route.astype(BF16)
    w_lo = (w_route - w_hi.astype(F32)).astype(BF16)
    bias = jnp.zeros((1, LANES), F32).at[0, :MOE_GROUPS].set(b_group)
    bias = bias.at[0, MOE_GROUPS:MOE_GROUPS + MOE_EXPERTS].set(b_expert)
    h, route = _router(x, g, w_hi, w_lo, bias)

    flat_e = route[:, :2].astype(jnp.int32).reshape(-1)
    flat_g = route[:, 2:4].reshape(-1)
    n_slot = 2 * n
    onehot = (flat_e[:, None] == jnp.arange(MOE_EXPERTS, dtype=jnp.int32)[None, :]).astype(jnp.int32)
    csum = jnp.cumsum(onehot, axis=0)
    rank = jnp.sum(csum * onehot, axis=1) - 1
    counts = csum[-1]
    padded = (counts + MOE_ROWS - 1) // MOE_ROWS * MOE_ROWS
    cum_padded = jnp.cumsum(padded)
    dest = (cum_padded - padded)[flat_e] + rank
    n_blocks = n_slot // MOE_ROWS + MOE_EXPERTS
    n_pad = n_blocks * MOE_ROWS
    buf_tok = jnp.zeros((n_pad,), jnp.int32).at[dest].set(jnp.arange(n_slot, dtype=jnp.int32) // 2)
    buf_gate = jnp.zeros((n_pad,), F32).at[dest].set(flat_g)
    block_start = jnp.arange(n_blocks, dtype=jnp.int32) * MOE_ROWS
    block_expert = jnp.minimum(jnp.searchsorted(cum_padded, block_start, side="right"),
                               MOE_EXPERTS - 1).astype(jnp.int32)

    xs = _gather_rows(h, buf_tok)
    ys = _moe_mlp(xs, buf_gate, block_expert, w_gate.astype(BF16), w_up.astype(BF16), w_down.astype(BF16))
    return _combine(x, ys, dest)


def _even_layer(x, bsz, t_len, norm_g, w_in, prm, v_first):
    n = bsz * t_len
    u = _norm_matmul(x, norm_g, w_in.astype(BF16)).reshape(bsz, t_len, -1)
    r, lw, k, v, a, b, g, pool = _even_prep(u, prm, v_first)
    y = _rwkv_chunk(r, lw, k, v, a, b)
    x = _even_out(x, y, r, k, v, g, pool, prm)
    return x, v


def _odd_layer(x, bsz, t_len, layer_idx, norm_g, w_in, prm):
    n = bsz * t_len
    u = _norm_matmul(x, norm_g, w_in.astype(BF16)).reshape(bsz, t_len, -1)
    q, k = _qk_prep(u, prm["q_norm"], prm["k_norm"])
    lam_init = 0.8 - 0.6 * math.exp(-0.3 * layer_idx)
    lam = (jnp.exp(jnp.sum(prm["lam_q1"] * prm["lam_k1"])) - jnp.exp(jnp.sum(prm["lam_q2"] * prm["lam_k2"]))
           + lam_init)
    o_attn = _diff_attn(q, k, u, lam, prm["subln"], 1.0 - lam_init)
    ng = bsz // S5_BATCH_GROUP
    u5 = u[:, :, 3 * RW_WIDTH:].reshape(ng, S5_BATCH_GROUP, t_len, RW_WIDTH).transpose(0, 2, 1, 3)
    o5 = _s5(u5, prm)
    o_ssm = o5.transpose(0, 2, 1, 3).reshape(bsz, t_len, RW_WIDTH)
    return _odd_out(x, o_attn, o_ssm, prm["w_out"])


def _s5_params(a_re, a_im, log_step, b_re, b_im, c_re, c_im, d_skip, w_glu):
    lam = lax.complex(jnp.minimum(a_re, -1e-4), a_im)
    lam_bar = jnp.exp(lam * jnp.exp(log_step))
    b_bar = ((lam_bar - 1.0) / lam)[..., None] * lax.complex(b_re, b_im)
    gpt = LANES // S5_GROUP_DIM
    eye = jnp.eye(gpt, dtype=F32)

    def in_map(part):
        p = part.reshape(S5_TILES, gpt, S5_STATE, S5_GROUP_DIM)
        return jnp.einsum("jgpc,gh->jgchp", p, eye).reshape(S5_TILES, LANES, gpt * S5_STATE)

    def out_map(part):
        p = part.reshape(S5_TILES, gpt, S5_GROUP_DIM, S5_STATE)
        return jnp.einsum("jgcp,gh->jgphc", p, eye).reshape(S5_TILES, gpt * S5_STATE, LANES)

    bt = jnp.concatenate([in_map(jnp.real(b_bar)), in_map(jnp.imag(b_bar))], axis=2).astype(BF16)
    ct = jnp.concatenate([out_map(c_re), -out_map(c_im)], axis=1).astype(BF16)
    return {"bt": bt, "ct": ct, "lam_re": jnp.real(lam_bar).reshape(1, -1), "lam_im": jnp.imag(lam_bar).reshape(1, -1),
            "d": d_skip, "w_glu": w_glu.astype(BF16)}


def _block_diag(blocks):
    g, r, c = blocks.shape
    return jnp.einsum("grc,gh->grhc", blocks, jnp.eye(g, dtype=blocks.dtype)).reshape(g * r, g * c)


def kernel(x, norm_mix_g, norm_ffn_g,
           even_w_in, rw_mu, rw_w0, rw_w2, rw_a0, rw_a2, rw_g2, rw_k_k, rw_k_a, rw_r_k,
           rw_ln_g, rw_ln_b, rw_v0, rw_v1, rw_v2, pool_w, pool_scale, even_w_out,
           odd_w_in, da_q_norm, da_k_norm, da_lam_q1, da_lam_k1, da_lam_q2, da_lam_k2, da_subln,
           s5_a_re, s5_a_im, s5_log_step, s5_b_re, s5_b_im, s5_c_re, s5_c_im, s5_d, s5_w_glu,
           odd_w_out,
           moe_w_group, moe_b_group, moe_w_expert, moe_b_expert, moe_w_gate, moe_w_up, moe_w_down):
    bsz, t_len, d = x.shape
    depth = norm_mix_g.shape[0]
    xf = x.reshape(bsz * t_len, d)
    v_first = None
    for layer in range(depth):
        i = layer // 2
        if layer % 2 == 0:
            rank = rw_w2.shape[1]
            wa = jnp.zeros((LANES, 2 * RW_WIDTH), F32)
            wa = wa.at[:rank, :RW_WIDTH].set(rw_w2[i]).at[rank:, RW_WIDTH:].set(rw_a2[i])
            prm = {"mu": rw_mu[i], "w0": rw_w0[i], "a0": rw_a0[i], "wa": wa.astype(BF16), "g2": rw_g2[i].astype(BF16),
                   "k_k": rw_k_k[i], "k_a": rw_k_a[i], "r_k": rw_r_k[i], "ln_g": rw_ln_g[i], "ln_b": rw_ln_b[i],
                   "pool_bd": _block_diag(pool_w[i]).astype(BF16), "pool_scale": pool_scale[i],
                   "w_out": even_w_out[i].astype(BF16)}
            if v_first is not None:
                vr = rw_v1.shape[2]
                prm["v0"] = rw_v0[i - 1]
                prm["v1"] = jnp.zeros((RW_WIDTH, LANES), F32).at[:, :vr].set(rw_v1[i - 1]).astype(BF16)
                prm["v2"] = jnp.zeros((LANES, RW_WIDTH), F32).at[:vr, :].set(rw_v2[i - 1]).astype(BF16)
            xf, v_new = _even_layer(xf, bsz, t_len, norm_mix_g[layer], even_w_in[i], prm, v_first)
            if v_first is None:
                v_first = v_new
        else:
            prm = _s5_params(s5_a_re[i], s5_a_im[i], s5_log_step[i], s5_b_re[i], s5_b_im[i], s5_c_re[i],
                             s5_c_im[i], s5_d[i].reshape(-1), s5_w_glu[i])
            prm.update({"q_norm": da_q_norm[i], "k_norm": da_k_norm[i], "lam_q1": da_lam_q1[i],
                        "lam_k1": da_lam_k1[i], "lam_q2": da_lam_q2[i], "lam_k2": da_lam_k2[i],
                        "subln": da_subln[i], "w_out": odd_w_out[i].astype(BF16)})
            xf = _odd_layer(xf, bsz, t_len, layer, norm_mix_g[layer], odd_w_in[i], prm)
        xf = _hier_moe(xf, norm_ffn_g[layer], moe_w_group[layer], moe_b_group[layer], moe_w_expert[layer],
                       moe_b_expert[layer], moe_w_gate[layer], moe_w_up[layer], moe_w_down[layer])
    return xf.reshape(bsz, t_len, d)
```

```python
import functools
import math

import jax
import jax.numpy as jnp
from jax import lax
from jax.experimental import pallas as pl
from jax.experimental.pallas import tpu as pltpu

F32 = jnp.float32
BF16 = jnp.bfloat16

LANES = 128
SUBLANES = 8
VMEM_BYTES_V7X = 64 * 1024 * 1024

D_MODEL = 1024
HEAD = 64
RW_WIDTH = 512
RW_SHIFT_COLS = 3 * RW_WIDTH + 64 + 64 + 128
RW_LN_EPS = 64e-5
POOL_WINDOWS = (2, 4, 8, 16)
POOL_HALO = 16
DA_HEADS = 4
SUBLN_EPS = 1e-5
ROPE_THETA = 10000.0
S5_GROUP_DIM = 16
S5_STATE = 64
S5_TILES = 4
S5_BATCH_GROUP = SUBLANES
MOE_GROUPS = 4
MOE_PER_GROUP = 8
MOE_EXPERTS = 32
MOE_ROWS = 256
RMS_EPS = 1e-6
CHUNK = 64
NEG_BIG = -1e30


def _params(semantics, vmem_mib):
    return pltpu.CompilerParams(dimension_semantics=semantics,
                                vmem_limit_bytes=min(vmem_mib * 1024 * 1024, VMEM_BYTES_V7X - 8 * 1024 * 1024))


def _bdot(a, b):
    return jnp.dot(a.astype(BF16), b.astype(BF16), preferred_element_type=F32)


def _bdot_nt(a, b):
    return lax.dot_general(a.astype(BF16), b.astype(BF16), (((1,), (1,)), ((), ())),
                           preferred_element_type=F32)


def _split_dot(x, m_bf16):
    hi = x.astype(BF16)
    lo = (x - hi.astype(F32)).astype(BF16)
    return (jnp.dot(hi, m_bf16, preferred_element_type=F32)
            + jnp.dot(lo, m_bf16, preferred_element_type=F32))


def _seg_ones(width):
    r = lax.broadcasted_iota(jnp.int32, (LANES, LANES), 0)
    c = lax.broadcasted_iota(jnp.int32, (LANES, LANES), 1)
    sh = int(math.log2(width))
    return ((r >> sh) == (c >> sh)).astype(BF16)


def _segsum(x, seg):
    tiles = [_split_dot(x[:, j * LANES:(j + 1) * LANES], seg) for j in range(x.shape[1] // LANES)]
    return tiles[0] if len(tiles) == 1 else jnp.concatenate(tiles, axis=1)


def _sigmoid(x):
    return 1.0 / (1.0 + jnp.exp(-x))


def _norm_matmul_kernel(x_ref, g_ref, w_ref, o_ref):
    x = x_ref[...]
    ms = jnp.mean(x * x, axis=-1, keepdims=True)
    h = x * lax.rsqrt(ms + RMS_EPS) * g_ref[...]
    o_ref[...] = jnp.dot(h.astype(BF16), w_ref[...], preferred_element_type=F32)


def _norm_matmul(x, g, w_bf16, tm=512):
    n, d = x.shape
    c = w_bf16.shape[1]
    tm = min(tm, n)
    return pl.pallas_call(
        _norm_matmul_kernel,
        out_shape=jax.ShapeDtypeStruct((n, c), F32),
        grid=(n // tm,),
        in_specs=[pl.BlockSpec((tm, d), lambda i: (i, 0)),
                  pl.BlockSpec((1, d), lambda i: (0, 0)),
                  pl.BlockSpec((d, c), lambda i: (0, 0))],
        out_specs=pl.BlockSpec((tm, c), lambda i: (i, 0)),
        compiler_params=_params(("parallel",), 48),
        name="norm_matmul",
    )(x, g.reshape(1, d), w_bf16)


def _even_prep_kernel(has_vres, tb, *refs):
    if has_vres:
        (u_ref, mu_ref, w0_ref, a0_ref, wa_ref, g2_ref, kk_ref, ka_ref, pw_ref, ps_ref,
         vf_ref, v0_ref, v1_ref, v2_ref,
         r_o, lw_o, k_o, v_o, a_o, b_o, g_o, pool_o, carry) = refs
    else:
        (u_ref, mu_ref, w0_ref, a0_ref, wa_ref, g2_ref, kk_ref, ka_ref, pw_ref, ps_ref,
         r_o, lw_o, k_o, v_o, a_o, b_o, g_o, pool_o, carry) = refs
    ti = pl.program_id(1)

    @pl.when(ti == 0)
    def _():
        carry[...] = jnp.zeros_like(carry)

    u = u_ref[0]
    ext = jnp.concatenate([carry[...], u], axis=0)
    carry[...] = u[tb - POOL_HALO:, :]

    p1 = ext[:, RW_SHIFT_COLS:]
    p2 = p1 + pltpu.roll(p1, 1, 0)
    p4 = p2 + pltpu.roll(p2, 2, 0)
    p8 = p4 + pltpu.roll(p4, 4, 0)
    p16 = p8 + pltpu.roll(p8, 8, 0)
    lane = lax.broadcasted_iota(jnp.int32, (1, RW_WIDTH), 1)
    grp = lane >> 7
    sums = jnp.where(grp == 0, p2, jnp.where(grp == 1, p4, jnp.where(grp == 2, p8, p16)))[POOL_HALO:]
    win = jnp.where(grp == 0, 2.0, jnp.where(grp == 1, 4.0, jnp.where(grp == 2, 8.0, 16.0)))
    n_seen = (ti * tb + lax.broadcasted_iota(jnp.int32, (tb, 1), 0) + 1).astype(F32)
    d = sums / jnp.minimum(n_seen, win) - u[:, RW_SHIFT_COLS:]
    pool_o[0] = _bdot(d, pw_ref[...]) * ps_ref[...]

    u_rw = u[:, :RW_SHIFT_COLS]
    prev = pltpu.roll(ext[:, :RW_SHIFT_COLS], 1, 0)[POOL_HALO:]
    m = u_rw + (prev - u_rw) * mu_ref[...]
    r = m[:, :RW_WIDTH]
    k = m[:, RW_WIDTH:2 * RW_WIDTH]
    v = m[:, 2 * RW_WIDTH:3 * RW_WIDTH]
    dwa = m[:, 3 * RW_WIDTH:3 * RW_WIDTH + LANES]
    dg = m[:, 3 * RW_WIDTH + LANES:]
    l128 = lax.broadcasted_iota(jnp.int32, (1, LANES), 1)
    dwa = jnp.where(l128 < HEAD, jnp.tanh(dwa), dwa)
    x12 = _bdot(dwa, wa_ref[...])
    z = -(w0_ref[...] + x12[:, :RW_WIDTH])
    softplus = jnp.maximum(z, 0.0) + jnp.log(1.0 + jnp.exp(-jnp.abs(z)))
    lw = -jnp.exp(-softplus - 0.5)
    a_i = _sigmoid(a0_ref[...] + x12[:, RW_WIDTH:])
    g_o[0] = _bdot(_sigmoid(dg), g2_ref[...])
    if has_vres:
        gate_v = _sigmoid(v0_ref[...] + _bdot(_bdot(v, v1_ref[...]), v2_ref[...]))
        v = v + (vf_ref[0] - v) * gate_v
    kk = k * kk_ref[...]
    ss = _segsum(kk * kk, _seg_ones(HEAD))
    kk = kk / jnp.maximum(jnp.sqrt(ss), 1e-12)
    r_o[0] = r
    lw_o[0] = lw
    k_o[0] = k * (1.0 + (a_i - 1.0) * ka_ref[...])
    v_o[0] = v
    a_o[0] = -kk
    b_o[0] = kk * a_i


def _even_prep(u, prm, v_first, tb=256):
    bsz, t_len, cin = u.shape
    tb = min(tb, t_len)
    has_vres = v_first is not None
    row = lambda a: a.reshape(1, -1)
    full = lambda shape: pl.BlockSpec(shape, lambda b, t: (0,) * len(shape))
    seq = pl.BlockSpec((1, tb, RW_WIDTH), lambda b, t: (b, t, 0))
    ins = [u, row(prm["mu"]), row(prm["w0"]), row(prm["a0"]), prm["wa"], prm["g2"], row(prm["k_k"]),
           row(prm["k_a"]), prm["pool_bd"], row(prm["pool_scale"])]
    specs = [pl.BlockSpec((1, tb, cin), lambda b, t: (b, t, 0)), full((1, RW_SHIFT_COLS)),
             full((1, RW_WIDTH)), full((1, RW_WIDTH)), full((LANES, 2 * RW_WIDTH)),
             full((LANES, RW_WIDTH)), full((1, RW_WIDTH)), full((1, RW_WIDTH)),
             full((RW_WIDTH, RW_WIDTH)), full((1, RW_WIDTH))]
    if has_vres:
        ins += [v_first, row(prm["v0"]), prm["v1"], prm["v2"]]
        specs += [seq, full((1, RW_WIDTH)), full((RW_WIDTH, LANES)), full((LANES, RW_WIDTH))]
    out = jax.ShapeDtypeStruct((bsz, t_len, RW_WIDTH), F32)
    return pl.pallas_call(
        functools.partial(_even_prep_kernel, has_vres, tb),
        out_shape=(out,) * 8,
        grid=(bsz, t_len // tb),
        in_specs=specs,
        out_specs=(seq,) * 8,
        scratch_shapes=[pltpu.VMEM((POOL_HALO, cin), F32)],
        compiler_params=_params(("parallel", "arbitrary"), 48),
        name="even_prep",
    )(*ins)


def _rwkv_chunk_kernel(nb, r_ref, lw_ref, k_ref, v_ref, a_ref, b_ref, y_ref, s_ref):
    ci = pl.program_id(1)

    @pl.when(ci == 0)
    def _():
        s_ref[...] = jnp.zeros_like(s_ref)

    L = CHUNK
    n_pair = RW_WIDTH // LANES
    tri = (lax.broadcasted_iota(jnp.int32, (L, L), 0) >= lax.broadcasted_iota(jnp.int32, (L, L), 1)).astype(BF16)
    lane = lax.broadcasted_iota(jnp.int32, (1, LANES), 1)
    m_a = lane < HEAD
    t_idx = lax.broadcasted_iota(jnp.int32, (L, LANES), 0)
    s_idx = lax.broadcasted_iota(jnp.int32, (L, LANES), 1) & (HEAD - 1)
    strict = t_idx > s_idx
    incl = t_idx >= s_idx
    r128 = lax.broadcasted_iota(jnp.int32, (LANES, LANES), 0)
    c128 = lax.broadcasted_iota(jnp.int32, (LANES, LANES), 1)
    eye = (r128 == c128).astype(F32)
    same_head = (r128 >> 6) == (c128 >> 6)

    def only_a(x):
        return jnp.where(m_a, x, jnp.zeros_like(x))

    def only_b(x):
        return jnp.where(m_a, jnp.zeros_like(x), x)

    def stack(x):
        return jnp.concatenate([only_a(x), only_b(x)], axis=0)

    def stack_sw(x):
        return jnp.concatenate([only_b(x), only_a(x)], axis=0)

    def mm(x, y):
        return jnp.dot(x, y, preferred_element_type=F32)

    def mm_nt(x, y):
        return lax.dot_general(x, y, (((1,), (1,)), ((), ())), preferred_element_type=F32)

    chains = []
    for bi in range(nb):
        lw = lw_ref[bi]
        c = _split_dot_lhs(tri, lw)
        e_pos = jnp.exp(c)
        e_neg = jnp.exp(-c)
        a_all = (jnp.exp(c - lw) * a_ref[bi]).astype(BF16)
        b_all = (b_ref[bi] * e_neg).astype(BF16)
        k_all = (k_ref[bi] * e_neg).astype(BF16)
        r_all = r_ref[bi] * e_pos
        v_all = v_ref[bi].astype(BF16)
        for p in range(n_pair):
            sl = slice(p * LANES, (p + 1) * LANES)
            chains.append({"bi": bi, "p": p, "sl": sl, "at": a_all[:, sl], "bt": b_all[:, sl], "kt": k_all[:, sl],
                           "rt": r_all[:, sl], "vv": v_all[:, sl], "gl": e_pos[L - 1:L, sl]})

    for ch in chains:
        rt16 = ch["rt"].astype(BF16)
        bk = jnp.concatenate([ch["bt"], ch["kt"]], axis=0)
        kb = jnp.concatenate([ch["kt"], ch["bt"]], axis=0)
        s_a = mm_nt(jnp.concatenate([only_a(ch["at"]), only_a(rt16)], axis=0), bk)
        s_b = mm_nt(jnp.concatenate([only_b(ch["at"]), only_b(rt16)], axis=0), kb)
        m_ha = jnp.where(strict, s_a[:L], 0.0)
        n_ha = jnp.where(incl, s_a[L:], 0.0)
        m_hb = jnp.where(strict, s_b[:L], 0.0)
        n_hb = jnp.where(incl, s_b[L:], 0.0)
        ch["bk"] = bk
        ch["bdm"] = jnp.concatenate([only_a(m_ha), only_b(m_hb)], axis=0)
        ch["ak_sw"] = jnp.where(m_a, m_hb, m_ha).astype(BF16)
        ch["n_lhs"] = jnp.concatenate([jnp.where(m_a, n_ha, n_hb),
                                       jnp.where(m_a, n_hb, n_ha)], axis=1).astype(BF16)
        ch["t_inv"] = eye + ch["bdm"]
        ch["m_pow"] = ch["bdm"].astype(BF16)
    for ch in chains:
        ch["w"] = mm(ch["ak_sw"], stack_sw(ch["vv"])).astype(BF16)
    for _ in range(int(math.log2(L)) - 1):
        for ch in chains:
            ch["m_pow"] = mm(ch["m_pow"], ch["m_pow"]).astype(BF16)
        for ch in chains:
            ch["t_inv"] = ch["t_inv"] + mm(ch["t_inv"].astype(BF16), ch["m_pow"])
    for ch in chains:
        t_pk = (ch["t_inv"][:L] + ch["t_inv"][L:]).astype(BF16)
        au = mm(t_pk, jnp.concatenate([stack(ch["at"]), stack(ch["w"])], axis=1))
        ch["a_hat"] = au[:, :LANES].astype(BF16)
        ch["u_hat"] = au[:, LANES:].astype(BF16)
    for ch in chains:
        rhs = jnp.concatenate([
            jnp.concatenate([stack(ch["u_hat"]), stack(ch["a_hat"])], axis=1),
            jnp.concatenate([stack_sw(ch["vv"]), jnp.zeros((LANES, LANES), BF16)], axis=1)], axis=0)
        yr = mm(ch["n_lhs"], rhs)
        ch["y_hat"] = yr[:, :LANES]
        ch["r_hat"] = (ch["rt"] + yr[:, LANES:]).astype(BF16)
        g_p = mm(ch["a_hat"].astype(F32).T.astype(BF16), ch["bt"])
        g_q = mm(jnp.concatenate([ch["u_hat"], ch["vv"]], axis=0).astype(F32).T.astype(BF16), ch["bk"])
        ch["p_bd"] = ((eye + jnp.where(same_head, g_p, 0.0)) * ch["gl"]).astype(BF16)
        ch["q_pk"] = jnp.where(m_a, g_q[:L], g_q[L:]) * ch["gl"]
    for ch in chains:
        s0 = s_ref[ch["bi"], ch["p"]]
        s16 = s0.astype(BF16)
        y_ref[ch["bi"], :, ch["sl"]] = mm_nt(ch["r_hat"], stack(s16)) + ch["y_hat"]
        s_ref[ch["bi"], ch["p"]] = mm(s16, ch["p_bd"]) + ch["q_pk"]


def _split_dot_lhs(m_bf16, x):
    hi = x.astype(BF16)
    lo = (x - hi.astype(F32)).astype(BF16)
    return (jnp.dot(m_bf16, hi, preferred_element_type=F32)
            + jnp.dot(m_bf16, lo, preferred_element_type=F32))


def _rwkv_chunk(r, lw, k, v, a, b, nb=4):
    bsz, t_len, _ = r.shape
    seq = pl.BlockSpec((nb, CHUNK, RW_WIDTH), lambda bi, ci: (bi, ci, 0))
    return pl.pallas_call(
        functools.partial(_rwkv_chunk_kernel, nb),
        out_shape=jax.ShapeDtypeStruct((bsz, t_len, RW_WIDTH), F32),
        grid=(bsz // nb, t_len // CHUNK),
        in_specs=[seq] * 6,
        out_specs=seq,
        scratch_shapes=[pltpu.VMEM((nb, RW_WIDTH // LANES, HEAD, LANES), F32)],
        compiler_params=_params(("parallel", "arbitrary"), 32),
        name="rwkv_chunk",
    )(r, lw, k, v, a, b)


def _even_out_kernel(x_ref, y_ref, r_ref, k_ref, v_ref, g_ref, p_ref, lng_ref, lnb_ref, rk_ref, wo_ref, o_ref):
    seg = _seg_ones(HEAD)
    y = y_ref[...]
    mean = _segsum(y, seg) * (1.0 / HEAD)
    yc = y - mean
    var = _segsum(yc * yc, seg) * (1.0 / HEAD)
    yn = yc * lax.rsqrt(var + RW_LN_EPS) * lng_ref[...] + lnb_ref[...]
    bonus = _segsum(r_ref[...] * k_ref[...] * rk_ref[...], seg) * v_ref[...]
    o_rw = (yn + bonus) * g_ref[...]
    cat = jnp.concatenate([o_rw, p_ref[...]], axis=1)
    o_ref[...] = x_ref[...] + _bdot(cat, wo_ref[...])


def _even_out(x, y, r, k, v, g, pool, prm, tm=512):
    n, d = x.shape
    tm = min(tm, n)
    half = pl.BlockSpec((tm, RW_WIDTH), lambda i: (i, 0))
    vec = pl.BlockSpec((1, RW_WIDTH), lambda i: (0, 0))
    fl = lambda a: a.reshape(n, RW_WIDTH)
    return pl.pallas_call(
        _even_out_kernel,
        out_shape=jax.ShapeDtypeStruct((n, d), F32),
        grid=(n // tm,),
        in_specs=[pl.BlockSpec((tm, d), lambda i: (i, 0))] + [half] * 6 + [vec] * 3
                 + [pl.BlockSpec((d, d), lambda i: (0, 0))],
        out_specs=pl.BlockSpec((tm, d), lambda i: (i, 0)),
        compiler_params=_params(("parallel",), 48),
        name="even_out",
    )(x, fl(y), fl(r), fl(k), fl(v), fl(g), fl(pool), prm["ln_g"].reshape(1, -1), prm["ln_b"].reshape(1, -1),
      prm["r_k"].reshape(1, -1), prm["w_out"])


def _qk_prep_kernel(u_ref, gq_ref, gk_ref, cos_ref, sin_ref, q_o, k_o):
    x = u_ref[0]
    cos = cos_ref[...]
    sin = sin_ref[...]
    seg = _seg_ones(HEAD)
    lane = lax.broadcasted_iota(jnp.int32, (1, LANES), 1)
    first = (lane & (HEAD - 1)) < HEAD // 2
    nq = RW_WIDTH // LANES
    for j in range(2 * nq):
        xt = x[:, j * LANES:(j + 1) * LANES]
        ms = _split_dot(xt * xt, seg) * (1.0 / HEAD)
        gain = gq_ref[...] if j < nq else gk_ref[...]
        xn = xt * lax.rsqrt(ms + RMS_EPS) * gain
        partner = jnp.where(first, pltpu.roll(xn, LANES - HEAD // 2, 1), pltpu.roll(xn, HEAD // 2, 1))
        out = xn * cos + partner * sin
        if j < nq:
            q_o[0, :, j * LANES:(j + 1) * LANES] = (out * (HEAD ** -0.5)).astype(BF16)
        else:
            k_o[0, :, (j - nq) * LANES:(j - nq + 1) * LANES] = out.astype(BF16)


def _qk_prep(u, q_norm, k_norm, tb=256):
    bsz, t_len, _ = u.shape
    tb = min(tb, t_len)
    inv_freq = ROPE_THETA ** (-jnp.arange(0, HEAD, 2, dtype=F32) / HEAD)
    ang = jnp.arange(t_len, dtype=F32)[:, None] * inv_freq[None, :]
    cos = jnp.tile(jnp.cos(ang), (1, LANES // (HEAD // 2)))
    sin_half = jnp.concatenate([-jnp.sin(ang), jnp.sin(ang)], axis=1)
    sin = jnp.tile(sin_half, (1, LANES // HEAD))
    tile2 = lambda g: jnp.tile(g, LANES // HEAD).reshape(1, LANES)
    out = jax.ShapeDtypeStruct((bsz, t_len, RW_WIDTH), BF16)
    tab = pl.BlockSpec((tb, LANES), lambda b, t: (t, 0))
    vec = pl.BlockSpec((1, LANES), lambda b, t: (0, 0))
    seq = pl.BlockSpec((1, tb, RW_WIDTH), lambda b, t: (b, t, 0))
    return pl.pallas_call(
        _qk_prep_kernel,
        out_shape=(out, out),
        grid=(bsz, t_len // tb),
        in_specs=[pl.BlockSpec((1, tb, 2 * RW_WIDTH), lambda b, t: (b, t, 0)), vec, vec, tab, tab],
        out_specs=(seq, seq),
        compiler_params=_params(("parallel", "parallel"), 32),
        name="qk_prep",
    )(u, tile2(q_norm), tile2(k_norm), cos, sin)


def _diff_attn_kernel(tq, out_scale, q_ref, k_ref, v_ref, lam_ref, sub_ref, o_ref):
    qi = pl.program_id(2)
    q = q_ref[0]
    lane = lax.broadcasted_iota(jnp.int32, (1, LANES), 1)
    m_a = lane < HEAD
    zero = jnp.zeros_like(q)
    q_sub = (jnp.where(m_a, q, zero), jnp.where(m_a, zero, q))

    def prefix(n_blk):
        kl = n_blk * tq
        k = k_ref[0, :kl, :]
        ok = ((n_blk - 1) * tq + lax.broadcasted_iota(jnp.int32, (tq, kl), 0)
              >= lax.broadcasted_iota(jnp.int32, (tq, kl), 1))

        def softmax(qs):
            s = lax.dot_general(qs, k, (((1,), (1,)), ((), ())), preferred_element_type=F32)
            s = jnp.where(ok, s, NEG_BIG)
            e = jnp.exp(s - jnp.max(s, axis=-1, keepdims=True))
            return e / jnp.sum(e, axis=-1, keepdims=True)

        attn = softmax(q_sub[0]) - lam_ref[...] * softmax(q_sub[1])
        o = _bdot(attn, v_ref[0, :kl, :])
        ms = jnp.mean(o * o, axis=-1, keepdims=True)
        o_ref[0] = o * lax.rsqrt(ms + SUBLN_EPS) * sub_ref[...] * out_scale

    for blk in range(k_ref.shape[1] // tq):
        pl.when(qi == blk)(functools.partial(prefix, blk + 1))


def _diff_attn(q, k, u, lam, subln, out_scale, tq=256):
    bsz, t_len, _ = q.shape
    tq = min(tq, t_len)
    v_off = 2 * RW_WIDTH // LANES
    return pl.pallas_call(
        functools.partial(_diff_attn_kernel, tq, out_scale),
        out_shape=jax.ShapeDtypeStruct((bsz, t_len, RW_WIDTH), F32),
        grid=(bsz, DA_HEADS, t_len // tq),
        in_specs=[pl.BlockSpec((1, tq, LANES), lambda b, h, i: (b, i, h)),
                  pl.BlockSpec((1, t_len, LANES), lambda b, h, i: (b, 0, h)),
                  pl.BlockSpec((1, t_len, LANES), lambda b, h, i: (b, 0, v_off + h)),
                  pl.BlockSpec((1, 1), lambda b, h, i: (0, 0)),
                  pl.BlockSpec((1, LANES), lambda b, h, i: (0, 0))],
        out_specs=pl.BlockSpec((1, tq, LANES), lambda b, h, i: (b, i, h)),
        compiler_params=_params(("parallel", "parallel", "arbitrary"), 48),
        name="diff_attn",
    )(q, k, u, lam.reshape(1, 1), subln.reshape(1, LANES))


def _s5_kernel(tb, u_ref, bt_ref, lre_ref, lim_ref, ct_ref, d_ref, wg_ref, o_ref, xs, st):
    ti = pl.program_id(1)

    @pl.when(ti == 0)
    def _():
        st[...] = jnp.zeros_like(st)

    rows = tb * SUBLANES
    half = RW_WIDTH
    u = u_ref[0].reshape(rows, RW_WIDTH)
    for j in range(S5_TILES):
        xs[:, 2 * half * j:2 * half * (j + 1)] = _bdot(u[:, j * LANES:(j + 1) * LANES], bt_ref[j])
    for j in range(S5_TILES):
        re = slice(2 * half * j, 2 * half * j + half)
        im = slice(2 * half * j + half, 2 * half * (j + 1))
        lr = jnp.broadcast_to(lre_ref[:, half * j:half * (j + 1)], (SUBLANES, half))
        li = jnp.broadcast_to(lim_ref[:, half * j:half * (j + 1)], (SUBLANES, half))

        def step(t, carry, re=re, im=im, lr=lr, li=li):
            xr, xi = carry
            row = pl.multiple_of(t * SUBLANES, SUBLANES)
            nr = lr * xr - li * xi + xs[pl.ds(row, SUBLANES), re]
            ni = lr * xi + li * xr + xs[pl.ds(row, SUBLANES), im]
            xs[pl.ds(row, SUBLANES), re] = nr
            xs[pl.ds(row, SUBLANES), im] = ni
            return nr, ni

        xr, xi = lax.fori_loop(0, tb, step, (st[:, re], st[:, im]), unroll=4)
        st[:, re] = xr
        st[:, im] = xi
    y = jnp.concatenate([_bdot(xs[:, 2 * half * j:2 * half * (j + 1)], ct_ref[j]) for j in range(S5_TILES)], axis=1)
    y = y + d_ref[...] * u
    z = 0.5 * y * (1.0 + jnp.tanh(math.sqrt(2.0 / math.pi) * (y + 0.044715 * (y * y * y))))
    o = z * _sigmoid(_bdot(z, wg_ref[...]))
    o_ref[0] = o.reshape(tb, SUBLANES, RW_WIDTH)


def _s5(u5, prm, tb=128):
    ng, t_len, _, _ = u5.shape
    tb = min(tb, t_len)
    full = lambda shape: pl.BlockSpec(shape, lambda g, t: (0,) * len(shape))
    n_state = 2 * RW_WIDTH * S5_TILES
    return pl.pallas_call(
        functools.partial(_s5_kernel, tb),
        out_shape=jax.ShapeDtypeStruct(u5.shape, F32),
        grid=(ng, t_len // tb),
        in_specs=[pl.BlockSpec((1, tb, SUBLANES, RW_WIDTH), lambda g, t: (g, t, 0, 0)),
                  full((S5_TILES, LANES, 2 * RW_WIDTH)), full((1, n_state // 2)), full((1, n_state // 2)),
                  full((S5_TILES, 2 * RW_WIDTH, LANES)), full((1, RW_WIDTH)), full((RW_WIDTH, RW_WIDTH))],
        out_specs=pl.BlockSpec((1, tb, SUBLANES, RW_WIDTH), lambda g, t: (g, t, 0, 0)),
        scratch_shapes=[pltpu.VMEM((tb * SUBLANES, n_state), F32), pltpu.VMEM((SUBLANES, n_state), F32)],
        compiler_params=_params(("parallel", "arbitrary"), 48),
        name="s5_scan",
    )(u5, prm["bt"], prm["lam_re"], prm["lam_im"], prm["ct"], prm["d"].reshape(1, -1), prm["w_glu"])


def _odd_out_kernel(x_ref, a_ref, s_ref, wo_ref, o_ref):
    cat = jnp.concatenate([a_ref[...], s_ref[...]], axis=1)
    o_ref[...] = x_ref[...] + _bdot(cat, wo_ref[...])


def _odd_out(x, o_attn, o_ssm, w_out, tm=512):
    n, d = x.shape
    tm = min(tm, n)
    half = pl.BlockSpec((tm, RW_WIDTH), lambda i: (i, 0))
    return pl.pallas_call(
        _odd_out_kernel,
        out_shape=jax.ShapeDtypeStruct((n, d), F32),
        grid=(n // tm,),
        in_specs=[pl.BlockSpec((tm, d), lambda i: (i, 0)), half, half, pl.BlockSpec((d, d), lambda i: (0, 0))],
        out_specs=pl.BlockSpec((tm, d), lambda i: (i, 0)),
        compiler_params=_params(("parallel",), 32),
        name="odd_out",
    )(x, o_attn.reshape(n, RW_WIDTH), o_ssm.reshape(n, RW_WIDTH), w_out)


def _router_kernel(tm, x_ref, g_ref, whi_ref, wlo_ref, b_ref, h_o, route_o, cnt_o, run):
    @pl.when(pl.program_id(0) == 0)
    def _():
        run[...] = jnp.zeros_like(run)

    x = x_ref[...]
    ms = jnp.mean(x * x, axis=-1, keepdims=True)
    h = x * lax.rsqrt(ms + RMS_EPS) * g_ref[...]
    h_o[...] = h
    h_hi = h.astype(BF16)
    h_lo = (h - h_hi.astype(F32)).astype(BF16)
    logits = (jnp.dot(h_hi, whi_ref[...], preferred_element_type=F32)
              + jnp.dot(h_hi, wlo_ref[...], preferred_element_type=F32)
              + jnp.dot(h_lo, whi_ref[...], preferred_element_type=F32)) + b_ref[...]
    lane = lax.broadcasted_iota(jnp.int32, logits.shape, 1).astype(F32)
    far = float(LANES)
    is_g = lane < MOE_GROUPS
    g_max = jnp.max(jnp.where(is_g, logits, NEG_BIG), axis=-1, keepdims=True)
    g_sum = jnp.sum(jnp.where(is_g, jnp.exp(jnp.minimum(logits - g_max, 0.0)), 0.0), axis=-1, keepdims=True)
    g_top = jnp.min(jnp.where(is_g & (logits == g_max), lane, far), axis=-1, keepdims=True)
    lo = MOE_GROUPS + MOE_PER_GROUP * g_top
    in_grp = (lane >= lo) & (lane < lo + MOE_PER_GROUP)
    e1 = jnp.max(jnp.where(in_grp, logits, NEG_BIG), axis=-1, keepdims=True)
    i1 = jnp.min(jnp.where(in_grp & (logits == e1), lane, far), axis=-1, keepdims=True)
    rest = in_grp & (lane != i1)
    e2 = jnp.max(jnp.where(rest, logits, NEG_BIG), axis=-1, keepdims=True)
    i2 = jnp.min(jnp.where(rest & (logits == e2), lane, far), axis=-1, keepdims=True)
    ratio = jnp.exp(e2 - e1)
    gate1 = 1.0 / (g_sum * (1.0 + ratio))
    gate2 = gate1 * ratio
    ex1 = i1 - MOE_GROUPS
    ex2 = i2 - MOE_GROUPS
    oh1 = lane == ex1
    oh2 = lane == ex2
    before = (lax.broadcasted_iota(jnp.int32, (tm, tm), 0) > lax.broadcasted_iota(jnp.int32, (tm, tm), 1)).astype(BF16)
    pre1 = jnp.dot(before, oh1.astype(BF16), preferred_element_type=F32)
    pre2 = jnp.dot(before, oh2.astype(BF16), preferred_element_type=F32)
    tot1 = jnp.sum(oh1.astype(F32), axis=0, keepdims=True)
    tot2 = jnp.sum(oh2.astype(F32), axis=0, keepdims=True)
    base = run[...]
    rank1 = jnp.sum(jnp.where(oh1, base + pre1, 0.0), axis=-1, keepdims=True)
    rank2 = jnp.sum(jnp.where(oh2, base + tot1 + pre2, 0.0), axis=-1, keepdims=True)
    run[...] = base + tot1 + tot2
    cnt_o[...] = base + tot1 + tot2
    route_o[...] = jnp.where(lane == 0, ex1, jnp.where(lane == 1, ex2, jnp.where(lane == 2, gate1, jnp.where(
        lane == 3, gate2, jnp.where(lane == 4, rank1, jnp.where(lane == 5, rank2, 0.0))))))


def _router(x, g, w_hi, w_lo, bias, tm=512):
    n, d = x.shape
    tm = min(tm, n)
    return pl.pallas_call(
        functools.partial(_router_kernel, tm),
        out_shape=(jax.ShapeDtypeStruct((n, d), F32), jax.ShapeDtypeStruct((n, LANES), F32),
                   jax.ShapeDtypeStruct((1, LANES), F32)),
        grid=(n // tm,),
        in_specs=[pl.BlockSpec((tm, d), lambda i: (i, 0)), pl.BlockSpec((1, d), lambda i: (0, 0)),
                  pl.BlockSpec((d, LANES), lambda i: (0, 0)), pl.BlockSpec((d, LANES), lambda i: (0, 0)),
                  pl.BlockSpec((1, LANES), lambda i: (0, 0))],
        out_specs=(pl.BlockSpec((tm, d), lambda i: (i, 0)), pl.BlockSpec((tm, LANES), lambda i: (i, 0)),
                   pl.BlockSpec((1, LANES), lambda i: (0, 0))),
        scratch_shapes=[pltpu.VMEM((1, LANES), F32)],
        compiler_params=_params(("arbitrary",), 32),
        name="moe_router",
    )(x, g.reshape(1, d), w_hi, w_lo, bias)


DMA_UNROLL = 8


def _dispatch_kernel(rows, dest_ref, h_ref, zero_ref, xs_ref, sem):
    del zero_ref

    def issue(i, c):
        for choice in range(2):
            pltpu.make_async_copy(h_ref.at[pl.ds(i, 1)], xs_ref.at[pl.ds(dest_ref[0, 0, 2 * i + choice], 1)],
                                  sem.at[choice]).start(priority=choice)
        return c

    lax.fori_loop(0, rows, issue, 0, unroll=DMA_UNROLL)
    for choice in range(2):
        pltpu.make_async_copy(h_ref, xs_ref.at[pl.ds(0, rows)], sem.at[choice]).wait()


def _dispatch(h, dest, n_pad, rows=256):
    n, d = h.shape
    rows = min(rows, n)
    nb = n // rows
    return pl.pallas_call(
        functools.partial(_dispatch_kernel, rows),
        out_shape=jax.ShapeDtypeStruct((n_pad, d), h.dtype),
        grid=(nb,),
        in_specs=[pl.BlockSpec((1, 1, 2 * rows), lambda i: (i, 0, 0), memory_space=pltpu.SMEM),
                  pl.BlockSpec((rows, d), lambda i: (i, 0)),
                  pl.BlockSpec(memory_space=pl.ANY)],
        out_specs=pl.BlockSpec(memory_space=pl.ANY),
        input_output_aliases={2: 0},
        scratch_shapes=[pltpu.SemaphoreType.DMA((2,))],
        compiler_params=_params(("arbitrary",), 16),
        name="moe_dispatch",
    )(dest.reshape(nb, 1, 2 * rows), h, jnp.zeros((n_pad, d), h.dtype))


def _moe_mlp_kernel(be_ref, x_ref, wg_ref, wu_ref, wd_ref, o_ref, wg_s, wu_s, wd_s):
    i = pl.program_id(0)

    @pl.when((i == 0) | (be_ref[i] != be_ref[jnp.maximum(i - 1, 0)]))
    def _():
        wg_s[...] = wg_ref[0].astype(BF16)
        wu_s[...] = wu_ref[0].astype(BF16)
        wd_s[...] = wd_ref[0].astype(BF16)

    x = x_ref[...].astype(BF16)
    hg = jnp.dot(x, wg_s[...], preferred_element_type=F32)
    hu = jnp.dot(x, wu_s[...], preferred_element_type=F32)
    hid = hg * _sigmoid(hg) * hu
    o_ref[...] = jnp.dot(hid.astype(BF16), wd_s[...], preferred_element_type=F32)


def _moe_mlp(xs, block_expert, w_gate, w_up, w_down):
    n_pad, d = xs.shape
    hid = w_gate.shape[2]
    nb = n_pad // MOE_ROWS
    grid_spec = pltpu.PrefetchScalarGridSpec(
        num_scalar_prefetch=1,
        grid=(nb,),
        in_specs=[pl.BlockSpec((MOE_ROWS, d), lambda i, be: (i, 0)),
                  pl.BlockSpec((1, d, hid), lambda i, be: (be[i], 0, 0)),
                  pl.BlockSpec((1, d, hid), lambda i, be: (be[i], 0, 0)),
                  pl.BlockSpec((1, hid, d), lambda i, be: (be[i], 0, 0))],
        out_specs=pl.BlockSpec((MOE_ROWS, d), lambda i, be: (i, 0)),
        scratch_shapes=[pltpu.VMEM((d, hid), BF16), pltpu.VMEM((d, hid), BF16), pltpu.VMEM((hid, d), BF16)],
    )
    return pl.pallas_call(
        _moe_mlp_kernel,
        out_shape=jax.ShapeDtypeStruct((n_pad, d), F32),
        grid_spec=grid_spec,
        compiler_params=_params(("arbitrary",), 40),
        name="moe_mlp",
    )(block_expert, xs, w_gate, w_up, w_down)


def _combine_kernel(rows, pos_ref, x_ref, route_ref, ys_ref, o_ref, buf0, buf1, sem):
    bufs = (buf0, buf1)

    def issue(i, c):
        for choice in range(2):
            pltpu.make_async_copy(ys_ref.at[pl.ds(pos_ref[0, 0, 2 * i + choice], 1)], bufs[choice].at[pl.ds(i, 1)],
                                  sem.at[choice]).start(priority=choice)
        return c

    lax.fori_loop(0, rows, issue, 0, unroll=DMA_UNROLL)
    for choice in range(2):
        pltpu.make_async_copy(ys_ref.at[pl.ds(0, rows)], bufs[choice], sem.at[choice]).wait()
    route = route_ref[...]
    o_ref[...] = x_ref[...] + route[:, 2:3] * buf0[...] + route[:, 3:4] * buf1[...]


def _combine(x, ys, route, pos, rows=256):
    n, d = x.shape
    rows = min(rows, n)
    nb = n // rows
    return pl.pallas_call(
        functools.partial(_combine_kernel, rows),
        out_shape=jax.ShapeDtypeStruct((n, d), F32),
        grid=(nb,),
        in_specs=[pl.BlockSpec((1, 1, 2 * rows), lambda i: (i, 0, 0), memory_space=pltpu.SMEM),
                  pl.BlockSpec((rows, d), lambda i: (i, 0)),
                  pl.BlockSpec((rows, LANES), lambda i: (i, 0)),
                  pl.BlockSpec(memory_space=pl.ANY)],
        out_specs=pl.BlockSpec((rows, d), lambda i: (i, 0)),
        scratch_shapes=[pltpu.VMEM((rows, d), F32), pltpu.VMEM((rows, d), F32), pltpu.SemaphoreType.DMA((2,))],
        compiler_params=_params(("arbitrary",), 16),
        name="moe_combine",
    )(pos.reshape(nb, 1, 2 * rows), x, route, ys)


def _hier_moe(x, g, w_group, b_group, w_expert, b_expert, w_gate, w_up, w_down):
    n, d = x.shape
    w_route = jnp.zeros((d, LANES), F32).at[:, :MOE_GROUPS].set(w_group)
    w_route = w_route.at[:, MOE_GROUPS:MOE_GROUPS + MOE_EXPERTS].set(w_expert)
    w_hi = w_route.astype(BF16)
    w_lo = (w_route - w_hi.astype(F32)).astype(BF16)
    bias = jnp.zeros((1, LANES), F32).at[0, :MOE_GROUPS].set(b_group)
    bias = bias.at[0, MOE_GROUPS:MOE_GROUPS + MOE_EXPERTS].set(b_expert)
    h, route, counts = _router(x, g, w_hi, w_lo, bias)

    counts = counts[0, :MOE_EXPERTS].astype(jnp.int32)
    padded = (counts + MOE_ROWS - 1) // MOE_ROWS * MOE_ROWS
    cum_padded = jnp.cumsum(padded)
    expert = route[:, 0:2].astype(jnp.int32)
    dest = ((cum_padded - padded)[expert] + route[:, 4:6].astype(jnp.int32)).reshape(-1)
    n_blocks = 2 * n // MOE_ROWS + MOE_EXPERTS
    block_start = jnp.arange(n_blocks, dtype=jnp.int32) * MOE_ROWS
    block_expert = jnp.minimum(jnp.searchsorted(cum_padded, block_start, side="right"),
                               MOE_EXPERTS - 1).astype(jnp.int32)

    xs = _dispatch(h, dest, n_blocks * MOE_ROWS)
    ys = _moe_mlp(xs, block_expert, w_gate, w_up, w_down)
    return _combine(x, ys, route, dest)


def _even_layer(x, bsz, t_len, norm_g, w_in, prm, v_first):
    n = bsz * t_len
    u = _norm_matmul(x, norm_g, w_in.astype(BF16)).reshape(bsz, t_len, -1)
    r, lw, k, v, a, b, g, pool = _even_prep(u, prm, v_first)
    y = _rwkv_chunk(r, lw, k, v, a, b)
    x = _even_out(x, y, r, k, v, g, pool, prm)
    return x, v


def _odd_layer(x, bsz, t_len, layer_idx, norm_g, w_in, prm):
    n = bsz * t_len
    u = _norm_matmul(x, norm_g, w_in.astype(BF16)).reshape(bsz, t_len, -1)
    q, k = _qk_prep(u, prm["q_norm"], prm["k_norm"])
    lam_init = 0.8 - 0.6 * math.exp(-0.3 * layer_idx)
    lam = (jnp.exp(jnp.sum(prm["lam_q1"] * prm["lam_k1"])) - jnp.exp(jnp.sum(prm["lam_q2"] * prm["lam_k2"]))
           + lam_init)
    o_attn = _diff_attn(q, k, u, lam, prm["subln"], 1.0 - lam_init)
    ng = bsz // S5_BATCH_GROUP
    u5 = u[:, :, 3 * RW_WIDTH:].reshape(ng, S5_BATCH_GROUP, t_len, RW_WIDTH).transpose(0, 2, 1, 3)
    o5 = _s5(u5, prm)
    o_ssm = o5.transpose(0, 2, 1, 3).reshape(bsz, t_len, RW_WIDTH)
    return _odd_out(x, o_attn, o_ssm, prm["w_out"])


def _s5_params(a_re, a_im, log_step, b_re, b_im, c_re, c_im, d_skip, w_glu):
    lam = lax.complex(jnp.minimum(a_re, -1e-4), a_im)
    lam_bar = jnp.exp(lam * jnp.exp(log_step))
    b_bar = ((lam_bar - 1.0) / lam)[..., None] * lax.complex(b_re, b_im)
    gpt = LANES // S5_GROUP_DIM
    eye = jnp.eye(gpt, dtype=F32)

    def in_map(part):
        p = part.reshape(S5_TILES, gpt, S5_STATE, S5_GROUP_DIM)
        return jnp.einsum("jgpc,gh->jgchp", p, eye).reshape(S5_TILES, LANES, gpt * S5_STATE)

    def out_map(part):
        p = part.reshape(S5_TILES, gpt, S5_GROUP_DIM, S5_STATE)
        return jnp.einsum("jgcp,gh->jgphc", p, eye).reshape(S5_TILES, gpt * S5_STATE, LANES)

    bt = jnp.concatenate([in_map(jnp.real(b_bar)), in_map(jnp.imag(b_bar))], axis=2).astype(BF16)
    ct = jnp.concatenate([out_map(c_re), -out_map(c_im)], axis=1).astype(BF16)
    return {"bt": bt, "ct": ct, "lam_re": jnp.real(lam_bar).reshape(1, -1), "lam_im": jnp.imag(lam_bar).reshape(1, -1),
            "d": d_skip, "w_glu": w_glu.astype(BF16)}


def _block_diag(blocks):
    g, r, c = blocks.shape
    return jnp.einsum("grc,gh->grhc", blocks, jnp.eye(g, dtype=blocks.dtype)).reshape(g * r, g * c)


def kernel(x, norm_mix_g, norm_ffn_g,
           even_w_in, rw_mu, rw_w0, rw_w2, rw_a0, rw_a2, rw_g2, rw_k_k, rw_k_a, rw_r_k,
           rw_ln_g, rw_ln_b, rw_v0, rw_v1, rw_v2, pool_w, pool_scale, even_w_out,
           odd_w_in, da_q_norm, da_k_norm, da_lam_q1, da_lam_k1, da_lam_q2, da_lam_k2, da_subln,
           s5_a_re, s5_a_im, s5_log_step, s5_b_re, s5_b_im, s5_c_re, s5_c_im, s5_d, s5_w_glu,
           odd_w_out,
           moe_w_group, moe_b_group, moe_w_expert, moe_b_expert, moe_w_gate, moe_w_up, moe_w_down):
    bsz, t_len, d = x.shape
    depth = norm_mix_g.shape[0]
    xf = x.reshape(bsz * t_len, d)
    v_first = None
    for layer in range(depth):
        i = layer // 2
        if layer % 2 == 0:
            rank = rw_w2.shape[1]
            wa = jnp.zeros((LANES, 2 * RW_WIDTH), F32)
            wa = wa.at[:rank, :RW_WIDTH].set(rw_w2[i]).at[rank:, RW_WIDTH:].set(rw_a2[i])
            prm = {"mu": rw_mu[i], "w0": rw_w0[i], "a0": rw_a0[i], "wa": wa.astype(BF16), "g2": rw_g2[i].astype(BF16),
                   "k_k": rw_k_k[i], "k_a": rw_k_a[i], "r_k": rw_r_k[i], "ln_g": rw_ln_g[i], "ln_b": rw_ln_b[i],
                   "pool_bd": _block_diag(pool_w[i]).astype(BF16), "pool_scale": pool_scale[i],
                   "w_out": even_w_out[i].astype(BF16)}
            if v_first is not None:
                vr = rw_v1.shape[2]
                prm["v0"] = rw_v0[i - 1]
                prm["v1"] = jnp.zeros((RW_WIDTH, LANES), F32).at[:, :vr].set(rw_v1[i - 1]).astype(BF16)
                prm["v2"] = jnp.zeros((LANES, RW_WIDTH), F32).at[:vr, :].set(rw_v2[i - 1]).astype(BF16)
            xf, v_new = _even_layer(xf, bsz, t_len, norm_mix_g[layer], even_w_in[i], prm, v_first)
            if v_first is None:
                v_first = v_new
        else:
            prm = _s5_params(s5_a_re[i], s5_a_im[i], s5_log_step[i], s5_b_re[i], s5_b_im[i], s5_c_re[i],
                             s5_c_im[i], s5_d[i].reshape(-1), s5_w_glu[i])
            prm.update({"q_norm": da_q_norm[i], "k_norm": da_k_norm[i], "lam_q1": da_lam_q1[i],
                        "lam_k1": da_lam_k1[i], "lam_q2": da_lam_q2[i], "lam_k2": da_lam_k2[i],
                        "subln": da_subln[i], "w_out": odd_w_out[i].astype(BF16)})
            xf = _odd_layer(xf, bsz, t_len, layer, norm_mix_g[layer], odd_w_in[i], prm)
        xf = _hier_moe(xf, norm_ffn_g[layer], moe_w_group[layer], moe_b_group[layer], moe_w_expert[layer],
                       moe_b_expert[layer], moe_w_gate[layer], moe_w_up[layer], moe_w_down[layer])
    return xf.reshape(bsz, t_len, d)
```

```python
import functools
import math

import jax
import jax.numpy as jnp
from jax import lax
from jax.experimental import pallas as pl
from jax.experimental.pallas import tpu as pltpu

F32 = jnp.float32
BF16 = jnp.bfloat16

LANES = 128
SUBLANES = 8
VMEM_BYTES_V7X = 64 * 1024 * 1024

D_MODEL = 1024
HEAD = 64
RW_WIDTH = 512
RW_SHIFT_COLS = 3 * RW_WIDTH + 64 + 64 + 128
RW_LN_EPS = 64e-5
POOL_WINDOWS = (2, 4, 8, 16)
POOL_HALO = 16
DA_HEADS = 4
SUBLN_EPS = 1e-5
ROPE_THETA = 10000.0
S5_GROUP_DIM = 16
S5_STATE = 64
S5_TILES = 4
S5_BATCH_GROUP = SUBLANES
MOE_GROUPS = 4
MOE_PER_GROUP = 8
MOE_EXPERTS = 32
MOE_ROWS = 256
RMS_EPS = 1e-6
CHUNK = 64
NEG_BIG = -1e30


def _params(semantics, vmem_mib):
    return pltpu.CompilerParams(dimension_semantics=semantics,
                                vmem_limit_bytes=min(vmem_mib * 1024 * 1024, VMEM_BYTES_V7X - 8 * 1024 * 1024))


def _bdot(a, b):
    return jnp.dot(a.astype(BF16), b.astype(BF16), preferred_element_type=F32)


def _bdot_nt(a, b):
    return lax.dot_general(a.astype(BF16), b.astype(BF16), (((1,), (1,)), ((), ())),
                           preferred_element_type=F32)


def _split_dot(x, m_bf16):
    hi = x.astype(BF16)
    lo = (x - hi.astype(F32)).astype(BF16)
    return (jnp.dot(hi, m_bf16, preferred_element_type=F32)
            + jnp.dot(lo, m_bf16, preferred_element_type=F32))


def _seg_ones(width):
    r = lax.broadcasted_iota(jnp.int32, (LANES, LANES), 0)
    c = lax.broadcasted_iota(jnp.int32, (LANES, LANES), 1)
    sh = int(math.log2(width))
    return ((r >> sh) == (c >> sh)).astype(BF16)


def _segsum(x, seg):
    tiles = [_split_dot(x[:, j * LANES:(j + 1) * LANES], seg) for j in range(x.shape[1] // LANES)]
    return tiles[0] if len(tiles) == 1 else jnp.concatenate(tiles, axis=1)


def _sigmoid(x):
    return 1.0 / (1.0 + jnp.exp(-x))


def _norm_matmul_kernel(x_ref, g_ref, w_ref, o_ref):
    x = x_ref[...]
    ms = jnp.mean(x * x, axis=-1, keepdims=True)
    h = x * lax.rsqrt(ms + RMS_EPS) * g_ref[...]
    o_ref[...] = jnp.dot(h.astype(BF16), w_ref[...], preferred_element_type=F32)


def _norm_matmul(x, g, w_bf16, tm=512):
    n, d = x.shape
    c = w_bf16.shape[1]
    tm = min(tm, n)
    return pl.pallas_call(
        _norm_matmul_kernel,
        out_shape=jax.ShapeDtypeStruct((n, c), F32),
        grid=(n // tm,),
        in_specs=[pl.BlockSpec((tm, d), lambda i: (i, 0)),
                  pl.BlockSpec((1, d), lambda i: (0, 0)),
                  pl.BlockSpec((d, c), lambda i: (0, 0))],
        out_specs=pl.BlockSpec((tm, c), lambda i: (i, 0)),
        compiler_params=_params(("parallel",), 48),
        name="norm_matmul",
    )(x, g.reshape(1, d), w_bf16)


def _even_prep_kernel(has_vres, tb, *refs):
    if has_vres:
        (u_ref, mu_ref, w0_ref, a0_ref, wa_ref, g2_ref, kk_ref, ka_ref, pw_ref, ps_ref,
         vf_ref, v0_ref, v1_ref, v2_ref,
         r_o, lw_o, k_o, v_o, a_o, b_o, g_o, pool_o, carry) = refs
    else:
        (u_ref, mu_ref, w0_ref, a0_ref, wa_ref, g2_ref, kk_ref, ka_ref, pw_ref, ps_ref,
         r_o, lw_o, k_o, v_o, a_o, b_o, g_o, pool_o, carry) = refs
    ti = pl.program_id(1)

    @pl.when(ti == 0)
    def _():
        carry[...] = jnp.zeros_like(carry)

    u = u_ref[0]
    ext = jnp.concatenate([carry[...], u], axis=0)
    carry[...] = u[tb - POOL_HALO:, :]

    p1 = ext[:, RW_SHIFT_COLS:]
    p2 = p1 + pltpu.roll(p1, 1, 0)
    p4 = p2 + pltpu.roll(p2, 2, 0)
    p8 = p4 + pltpu.roll(p4, 4, 0)
    p16 = p8 + pltpu.roll(p8, 8, 0)
    lane = lax.broadcasted_iota(jnp.int32, (1, RW_WIDTH), 1)
    grp = lane >> 7
    sums = jnp.where(grp == 0, p2, jnp.where(grp == 1, p4, jnp.where(grp == 2, p8, p16)))[POOL_HALO:]
    win = jnp.where(grp == 0, 2.0, jnp.where(grp == 1, 4.0, jnp.where(grp == 2, 8.0, 16.0)))
    n_seen = (ti * tb + lax.broadcasted_iota(jnp.int32, (tb, 1), 0) + 1).astype(F32)
    d = sums / jnp.minimum(n_seen, win) - u[:, RW_SHIFT_COLS:]
    pool_o[0] = _bdot(d, pw_ref[...]) * ps_ref[...]

    u_rw = u[:, :RW_SHIFT_COLS]
    prev = pltpu.roll(ext[:, :RW_SHIFT_COLS], 1, 0)[POOL_HALO:]
    m = u_rw + (prev - u_rw) * mu_ref[...]
    r = m[:, :RW_WIDTH]
    k = m[:, RW_WIDTH:2 * RW_WIDTH]
    v = m[:, 2 * RW_WIDTH:3 * RW_WIDTH]
    dwa = m[:, 3 * RW_WIDTH:3 * RW_WIDTH + LANES]
    dg = m[:, 3 * RW_WIDTH + LANES:]
    l128 = lax.broadcasted_iota(jnp.int32, (1, LANES), 1)
    dwa = jnp.where(l128 < HEAD, jnp.tanh(dwa), dwa)
    x12 = _bdot(dwa, wa_ref[...])
    z = -(w0_ref[...] + x12[:, :RW_WIDTH])
    softplus = jnp.maximum(z, 0.0) + jnp.log(1.0 + jnp.exp(-jnp.abs(z)))
    lw = -jnp.exp(-softplus - 0.5)
    a_i = _sigmoid(a0_ref[...] + x12[:, RW_WIDTH:])
    g_o[0] = _bdot(_sigmoid(dg), g2_ref[...])
    if has_vres:
        gate_v = _sigmoid(v0_ref[...] + _bdot(_bdot(v, v1_ref[...]), v2_ref[...]))
        v = v + (vf_ref[0] - v) * gate_v
    kk = k * kk_ref[...]
    ss = _segsum(kk * kk, _seg_ones(HEAD))
    kk = kk / jnp.maximum(jnp.sqrt(ss), 1e-12)
    r_o[0] = r
    lw_o[0] = lw
    k_o[0] = k * (1.0 + (a_i - 1.0) * ka_ref[...])
    v_o[0] = v
    a_o[0] = -kk
    b_o[0] = kk * a_i


def _even_prep(u, prm, v_first, tb=256):
    bsz, t_len, cin = u.shape
    tb = min(tb, t_len)
    has_vres = v_first is not None
    row = lambda a: a.reshape(1, -1)
    full = lambda shape: pl.BlockSpec(shape, lambda b, t: (0,) * len(shape))
    seq = pl.BlockSpec((1, tb, RW_WIDTH), lambda b, t: (b, t, 0))
    ins = [u, row(prm["mu"]), row(prm["w0"]), row(prm["a0"]), prm["wa"], prm["g2"], row(prm["k_k"]),
           row(prm["k_a"]), prm["pool_bd"], row(prm["pool_scale"])]
    specs = [pl.BlockSpec((1, tb, cin), lambda b, t: (b, t, 0)), full((1, RW_SHIFT_COLS)),
             full((1, RW_WIDTH)), full((1, RW_WIDTH)), full((LANES, 2 * RW_WIDTH)),
             full((LANES, RW_WIDTH)), full((1, RW_WIDTH)), full((1, RW_WIDTH)),
             full((RW_WIDTH, RW_WIDTH)), full((1, RW_WIDTH))]
    if has_vres:
        ins += [v_first, row(prm["v0"]), prm["v1"], prm["v2"]]
        specs += [seq, full((1, RW_WIDTH)), full((RW_WIDTH, LANES)), full((LANES, RW_WIDTH))]
    out = jax.ShapeDtypeStruct((bsz, t_len, RW_WIDTH), F32)
    return pl.pallas_call(
        functools.partial(_even_prep_kernel, has_vres, tb),
        out_shape=(out,) * 8,
        grid=(bsz, t_len // tb),
        in_specs=specs,
        out_specs=(seq,) * 8,
        scratch_shapes=[pltpu.VMEM((POOL_HALO, cin), F32)],
        compiler_params=_params(("parallel", "arbitrary"), 48),
        name="even_prep",
    )(*ins)


def _rwkv_chunk_kernel(nb, r_ref, lw_ref, k_ref, v_ref, a_ref, b_ref, y_ref, s_ref):
    ci = pl.program_id(1)

    @pl.when(ci == 0)
    def _():
        s_ref[...] = jnp.zeros_like(s_ref)

    L = CHUNK
    n_pair = RW_WIDTH // LANES
    tri = (lax.broadcasted_iota(jnp.int32, (L, L), 0) >= lax.broadcasted_iota(jnp.int32, (L, L), 1)).astype(BF16)
    lane = lax.broadcasted_iota(jnp.int32, (1, LANES), 1)
    m_a = lane < HEAD
    t_idx = lax.broadcasted_iota(jnp.int32, (L, LANES), 0)
    s_idx = lax.broadcasted_iota(jnp.int32, (L, LANES), 1) & (HEAD - 1)
    strict = t_idx > s_idx
    incl = t_idx >= s_idx
    r128 = lax.broadcasted_iota(jnp.int32, (LANES, LANES), 0)
    c128 = lax.broadcasted_iota(jnp.int32, (LANES, LANES), 1)
    eye = (r128 == c128).astype(F32)
    same_head = (r128 >> 6) == (c128 >> 6)

    def only_a(x):
        return jnp.where(m_a, x, jnp.zeros_like(x))

    def only_b(x):
        return jnp.where(m_a, jnp.zeros_like(x), x)

    def stack(x):
        return jnp.concatenate([only_a(x), only_b(x)], axis=0)

    def stack_sw(x):
        return jnp.concatenate([only_b(x), only_a(x)], axis=0)

    def mm(x, y):
        return jnp.dot(x, y, preferred_element_type=F32)

    def mm_nt(x, y):
        return lax.dot_general(x, y, (((1,), (1,)), ((), ())), preferred_element_type=F32)

    chains = []
    for bi in range(nb):
        lw = lw_ref[bi]
        c = _split_dot_lhs(tri, lw)
        e_pos = jnp.exp(c)
        e_neg = jnp.exp(-c)
        a_all = (jnp.exp(c - lw) * a_ref[bi]).astype(BF16)
        b_all = (b_ref[bi] * e_neg).astype(BF16)
        k_all = (k_ref[bi] * e_neg).astype(BF16)
        r_all = r_ref[bi] * e_pos
        v_all = v_ref[bi].astype(BF16)
        for p in range(n_pair):
            sl = slice(p * LANES, (p + 1) * LANES)
            chains.append({"bi": bi, "p": p, "sl": sl, "at": a_all[:, sl], "bt": b_all[:, sl], "kt": k_all[:, sl],
                           "rt": r_all[:, sl], "vv": v_all[:, sl], "gl": e_pos[L - 1:L, sl]})

    for ch in chains:
        rt16 = ch["rt"].astype(BF16)
        bk = jnp.concatenate([ch["bt"], ch["kt"]], axis=0)
        kb = jnp.concatenate([ch["kt"], ch["bt"]], axis=0)
        s_a = mm_nt(jnp.concatenate([only_a(ch["at"]), only_a(rt16)], axis=0), bk)
        s_b = mm_nt(jnp.concatenate([only_b(ch["at"]), only_b(rt16)], axis=0), kb)
        m_ha = jnp.where(strict, s_a[:L], 0.0)
        n_ha = jnp.where(incl, s_a[L:], 0.0)
        m_hb = jnp.where(strict, s_b[:L], 0.0)
        n_hb = jnp.where(incl, s_b[L:], 0.0)
        ch["bk"] = bk
        ch["bdm"] = jnp.concatenate([only_a(m_ha), only_b(m_hb)], axis=0)
        ch["ak_sw"] = jnp.where(m_a, m_hb, m_ha).astype(BF16)
        ch["n_lhs"] = jnp.concatenate([jnp.where(m_a, n_ha, n_hb),
                                       jnp.where(m_a, n_hb, n_ha)], axis=1).astype(BF16)
        ch["t_inv"] = eye + ch["bdm"]
        ch["m_pow"] = ch["bdm"].astype(BF16)
    for ch in chains:
        ch["w"] = mm(ch["ak_sw"], stack_sw(ch["vv"])).astype(BF16)
    for _ in range(int(math.log2(L)) - 1):
        for ch in chains:
            ch["m_pow"] = mm(ch["m_pow"], ch["m_pow"]).astype(BF16)
        for ch in chains:
            ch["t_inv"] = ch["t_inv"] + mm(ch["t_inv"].astype(BF16), ch["m_pow"])
    for ch in chains:
        t_pk = (ch["t_inv"][:L] + ch["t_inv"][L:]).astype(BF16)
        au = mm(t_pk, jnp.concatenate([stack(ch["at"]), stack(ch["w"])], axis=1))
        ch["a_hat"] = au[:, :LANES].astype(BF16)
        ch["u_hat"] = au[:, LANES:].astype(BF16)
    for ch in chains:
        rhs = jnp.concatenate([
            jnp.concatenate([stack(ch["u_hat"]), stack(ch["a_hat"])], axis=1),
            jnp.concatenate([stack_sw(ch["vv"]), jnp.zeros((LANES, LANES), BF16)], axis=1)], axis=0)
        yr = mm(ch["n_lhs"], rhs)
        ch["y_hat"] = yr[:, :LANES]
        ch["r_hat"] = (ch["rt"] + yr[:, LANES:]).astype(BF16)
        g_p = mm(ch["a_hat"].astype(F32).T.astype(BF16), ch["bt"])
        g_q = mm(jnp.concatenate([ch["u_hat"], ch["vv"]], axis=0).astype(F32).T.astype(BF16), ch["bk"])
        ch["p_bd"] = ((eye + jnp.where(same_head, g_p, 0.0)) * ch["gl"]).astype(BF16)
        ch["q_pk"] = jnp.where(m_a, g_q[:L], g_q[L:]) * ch["gl"]
    for ch in chains:
        s0 = s_ref[ch["bi"], ch["p"]]
        s16 = s0.astype(BF16)
        y_ref[ch["bi"], :, ch["sl"]] = mm_nt(ch["r_hat"], stack(s16)) + ch["y_hat"]
        s_ref[ch["bi"], ch["p"]] = mm(s16, ch["p_bd"]) + ch["q_pk"]


def _split_dot_lhs(m_bf16, x):
    hi = x.astype(BF16)
    lo = (x - hi.astype(F32)).astype(BF16)
    return (jnp.dot(m_bf16, hi, preferred_element_type=F32)
            + jnp.dot(m_bf16, lo, preferred_element_type=F32))


def _rwkv_chunk(r, lw, k, v, a, b, nb=4):
    bsz, t_len, _ = r.shape
    seq = pl.BlockSpec((nb, CHUNK, RW_WIDTH), lambda bi, ci: (bi, ci, 0))
    return pl.pallas_call(
        functools.partial(_rwkv_chunk_kernel, nb),
        out_shape=jax.ShapeDtypeStruct((bsz, t_len, RW_WIDTH), F32),
        grid=(bsz // nb, t_len // CHUNK),
        in_specs=[seq] * 6,
        out_specs=seq,
        scratch_shapes=[pltpu.VMEM((nb, RW_WIDTH // LANES, HEAD, LANES), F32)],
        compiler_params=_params(("parallel", "arbitrary"), 32),
        name="rwkv_chunk",
    )(r, lw, k, v, a, b)


def _even_out_kernel(x_ref, y_ref, r_ref, k_ref, v_ref, g_ref, p_ref, lng_ref, lnb_ref, rk_ref, wo_ref, o_ref):
    seg = _seg_ones(HEAD)
    y = y_ref[...]
    mean = _segsum(y, seg) * (1.0 / HEAD)
    yc = y - mean
    var = _segsum(yc * yc, seg) * (1.0 / HEAD)
    yn = yc * lax.rsqrt(var + RW_LN_EPS) * lng_ref[...] + lnb_ref[...]
    bonus = _segsum(r_ref[...] * k_ref[...] * rk_ref[...], seg) * v_ref[...]
    o_rw = (yn + bonus) * g_ref[...]
    cat = jnp.concatenate([o_rw, p_ref[...]], axis=1)
    o_ref[...] = x_ref[...] + _bdot(cat, wo_ref[...])


def _even_out(x, y, r, k, v, g, pool, prm, tm=512):
    n, d = x.shape
    tm = min(tm, n)
    half = pl.BlockSpec((tm, RW_WIDTH), lambda i: (i, 0))
    vec = pl.BlockSpec((1, RW_WIDTH), lambda i: (0, 0))
    fl = lambda a: a.reshape(n, RW_WIDTH)
    return pl.pallas_call(
        _even_out_kernel,
        out_shape=jax.ShapeDtypeStruct((n, d), F32),
        grid=(n // tm,),
        in_specs=[pl.BlockSpec((tm, d), lambda i: (i, 0))] + [half] * 6 + [vec] * 3
                 + [pl.BlockSpec((d, d), lambda i: (0, 0))],
        out_specs=pl.BlockSpec((tm, d), lambda i: (i, 0)),
        compiler_params=_params(("parallel",), 48),
        name="even_out",
    )(x, fl(y), fl(r), fl(k), fl(v), fl(g), fl(pool), prm["ln_g"].reshape(1, -1), prm["ln_b"].reshape(1, -1),
      prm["r_k"].reshape(1, -1), prm["w_out"])


def _qk_prep_kernel(u_ref, gq_ref, gk_ref, cos_ref, sin_ref, q_o, k_o):
    x = u_ref[0]
    cos = cos_ref[...]
    sin = sin_ref[...]
    seg = _seg_ones(HEAD)
    lane = lax.broadcasted_iota(jnp.int32, (1, LANES), 1)
    first = (lane & (HEAD - 1)) < HEAD // 2
    nq = RW_WIDTH // LANES
    for j in range(2 * nq):
        xt = x[:, j * LANES:(j + 1) * LANES]
        ms = _split_dot(xt * xt, seg) * (1.0 / HEAD)
        gain = gq_ref[...] if j < nq else gk_ref[...]
        xn = xt * lax.rsqrt(ms + RMS_EPS) * gain
        partner = jnp.where(first, pltpu.roll(xn, LANES - HEAD // 2, 1), pltpu.roll(xn, HEAD // 2, 1))
        out = xn * cos + partner * sin
        if j < nq:
            q_o[0, :, j * LANES:(j + 1) * LANES] = (out * (HEAD ** -0.5)).astype(BF16)
        else:
            k_o[0, :, (j - nq) * LANES:(j - nq + 1) * LANES] = out.astype(BF16)


def _qk_prep(u, q_norm, k_norm, tb=256):
    bsz, t_len, _ = u.shape
    tb = min(tb, t_len)
    inv_freq = ROPE_THETA ** (-jnp.arange(0, HEAD, 2, dtype=F32) / HEAD)
    ang = jnp.arange(t_len, dtype=F32)[:, None] * inv_freq[None, :]
    cos = jnp.tile(jnp.cos(ang), (1, LANES // (HEAD // 2)))
    sin_half = jnp.concatenate([-jnp.sin(ang), jnp.sin(ang)], axis=1)
    sin = jnp.tile(sin_half, (1, LANES // HEAD))
    tile2 = lambda g: jnp.tile(g, LANES // HEAD).reshape(1, LANES)
    out = jax.ShapeDtypeStruct((bsz, t_len, RW_WIDTH), BF16)
    tab = pl.BlockSpec((tb, LANES), lambda b, t: (t, 0))
    vec = pl.BlockSpec((1, LANES), lambda b, t: (0, 0))
    seq = pl.BlockSpec((1, tb, RW_WIDTH), lambda b, t: (b, t, 0))
    return pl.pallas_call(
        _qk_prep_kernel,
        out_shape=(out, out),
        grid=(bsz, t_len // tb),
        in_specs=[pl.BlockSpec((1, tb, 2 * RW_WIDTH), lambda b, t: (b, t, 0)), vec, vec, tab, tab],
        out_specs=(seq, seq),
        compiler_params=_params(("parallel", "parallel"), 32),
        name="qk_prep",
    )(u, tile2(q_norm), tile2(k_norm), cos, sin)


def _diff_attn_kernel(tq, nh, out_scale, q_ref, k_ref, v_ref, lam_ref, sub_ref, o_ref):
    qi = pl.program_id(2)
    lane = lax.broadcasted_iota(jnp.int32, (1, LANES), 1)
    m_a = lane < HEAD

    def prefix(n_blk):
        kl = n_blk * tq
        on_diag = lax.broadcasted_iota(jnp.int32, (tq, tq), 0) >= lax.broadcasted_iota(jnp.int32, (tq, tq), 1)
        qs, ks, vs = [], [], []
        for h in range(nh):
            sl = slice(h * LANES, (h + 1) * LANES)
            q = q_ref[0, :, sl]
            zero = jnp.zeros_like(q)
            qs += [jnp.where(m_a, q, zero), jnp.where(m_a, zero, q)]
            ks += [k_ref[0, :kl, sl]] * 2
            vs += [v_ref[0, :kl, sl].astype(BF16)] * 2
        s = [lax.dot_general(q, k, (((1,), (1,)), ((), ())), preferred_element_type=F32) for q, k in zip(qs, ks)]
        last = [jnp.where(on_diag, x[:, kl - tq:], NEG_BIG) for x in s]
        if n_blk > 1:
            last = [jnp.concatenate([x[:, :kl - tq], y], axis=1) for x, y in zip(s, last)]
        top = [jnp.max(x, axis=-1, keepdims=True) for x in last]
        e = [jnp.exp(x - m) for x, m in zip(last, top)]
        den = [jnp.sum(x, axis=-1, keepdims=True) for x in e]
        pv = [jnp.dot(x.astype(BF16), v, preferred_element_type=F32) for x, v in zip(e, vs)]
        for h in range(nh):
            sl = slice(h * LANES, (h + 1) * LANES)
            o = pv[2 * h] / den[2 * h] - lam_ref[...] * (pv[2 * h + 1] / den[2 * h + 1])
            ms = jnp.mean(o * o, axis=-1, keepdims=True)
            o_ref[0, :, sl] = o * lax.rsqrt(ms + SUBLN_EPS) * sub_ref[...] * out_scale

    for blk in range(k_ref.shape[1] // tq):
        pl.when(qi == blk)(functools.partial(prefix, blk + 1))


def _diff_attn(q, k, u, lam, subln, out_scale, tq=256, nh=2):
    bsz, t_len, _ = q.shape
    tq = min(tq, t_len)
    wide = nh * LANES
    v_off = 2 * RW_WIDTH // wide
    return pl.pallas_call(
        functools.partial(_diff_attn_kernel, tq, nh, out_scale),
        out_shape=jax.ShapeDtypeStruct((bsz, t_len, RW_WIDTH), F32),
        grid=(bsz, DA_HEADS // nh, t_len // tq),
        in_specs=[pl.BlockSpec((1, tq, wide), lambda b, h, i: (b, i, h)),
                  pl.BlockSpec((1, t_len, wide), lambda b, h, i: (b, 0, h)),
                  pl.BlockSpec((1, t_len, wide), lambda b, h, i: (b, 0, v_off + h)),
                  pl.BlockSpec((1, 1), lambda b, h, i: (0, 0)),
                  pl.BlockSpec((1, LANES), lambda b, h, i: (0, 0))],
        out_specs=pl.BlockSpec((1, tq, wide), lambda b, h, i: (b, i, h)),
        compiler_params=_params(("parallel", "parallel", "arbitrary"), 48),
        name="diff_attn",
    )(q, k, u, lam.reshape(1, 1), subln.reshape(1, LANES))


def _s5_kernel(tb, u_ref, bt_ref, lre_ref, lim_ref, ct_ref, d_ref, wg_ref, o_ref, xs, st):
    ti = pl.program_id(1)

    @pl.when(ti == 0)
    def _():
        st[...] = jnp.zeros_like(st)

    rows = tb * SUBLANES
    half = RW_WIDTH
    per = half // LANES
    u = u_ref[...].reshape(rows, RW_WIDTH)
    for j in range(S5_TILES):
        bu = _bdot(u[:, j * LANES:(j + 1) * LANES], bt_ref[j])
        for s in range(2 * per):
            xs[2 * per * j + s] = bu[:, s * LANES:(s + 1) * LANES]
    for j in range(S5_TILES):
        lam = [jnp.broadcast_to(ref[:, half * j + s * LANES:half * j + (s + 1) * LANES], (SUBLANES, LANES))
               for ref in (lre_ref, lim_ref) for s in range(per)]

        def step(t, carry, j=j, lam=lam):
            at_t = pl.ds(t, SUBLANES, stride=tb)
            out = []
            for s in range(per):
                xr, xi, lr, li = carry[s], carry[per + s], lam[s], lam[per + s]
                out.append(lr * xr - li * xi + xs[2 * per * j + s, at_t, :])
            for s in range(per):
                xr, xi, lr, li = carry[s], carry[per + s], lam[s], lam[per + s]
                out.append(lr * xi + li * xr + xs[2 * per * j + per + s, at_t, :])
            for s in range(2 * per):
                xs[2 * per * j + s, at_t, :] = out[s]
            return tuple(out)

        init = tuple(st[2 * per * j + s] for s in range(2 * per))
        fin = lax.fori_loop(0, tb, step, init, unroll=4)
        for s in range(2 * per):
            st[2 * per * j + s] = fin[s]
    y = jnp.concatenate(
        [_bdot(jnp.concatenate([xs[2 * per * j + s] for s in range(2 * per)], axis=1), ct_ref[j])
         for j in range(S5_TILES)], axis=1)
    y = y + d_ref[...] * u
    z = 0.5 * y * (1.0 + jnp.tanh(math.sqrt(2.0 / math.pi) * (y + 0.044715 * (y * y * y))))
    o = z * _sigmoid(_bdot(z, wg_ref[...]))
    o_ref[...] = o.reshape(SUBLANES, tb, RW_WIDTH)


def _s5(u, prm, tb=128):
    bsz, t_len, cin = u.shape
    tb = min(tb, t_len)
    full = lambda shape: pl.BlockSpec(shape, lambda g, t: (0,) * len(shape))
    n_state = 2 * RW_WIDTH * S5_TILES
    last = cin // RW_WIDTH - 1
    return pl.pallas_call(
        functools.partial(_s5_kernel, tb),
        out_shape=jax.ShapeDtypeStruct((bsz, t_len, RW_WIDTH), F32),
        grid=(bsz // S5_BATCH_GROUP, t_len // tb),
        in_specs=[pl.BlockSpec((S5_BATCH_GROUP, tb, RW_WIDTH), lambda g, t: (g, t, last)),
                  full((S5_TILES, LANES, 2 * RW_WIDTH)), full((1, n_state // 2)), full((1, n_state // 2)),
                  full((S5_TILES, 2 * RW_WIDTH, LANES)), full((1, RW_WIDTH)), full((RW_WIDTH, RW_WIDTH))],
        out_specs=pl.BlockSpec((S5_BATCH_GROUP, tb, RW_WIDTH), lambda g, t: (g, t, 0)),
        scratch_shapes=[pltpu.VMEM((n_state // LANES, tb * SUBLANES, LANES), F32),
                        pltpu.VMEM((n_state // LANES, SUBLANES, LANES), F32)],
        compiler_params=_params(("parallel", "arbitrary"), 48),
        name="s5_scan",
    )(u, prm["bt"], prm["lam_re"], prm["lam_im"], prm["ct"], prm["d"].reshape(1, -1), prm["w_glu"])


def _odd_out_kernel(x_ref, a_ref, s_ref, wo_ref, o_ref):
    cat = jnp.concatenate([a_ref[...], s_ref[...]], axis=1)
    o_ref[...] = x_ref[...] + _bdot(cat, wo_ref[...])


def _odd_out(x, o_attn, o_ssm, w_out, tm=512):
    n, d = x.shape
    tm = min(tm, n)
    half = pl.BlockSpec((tm, RW_WIDTH), lambda i: (i, 0))
    return pl.pallas_call(
        _odd_out_kernel,
        out_shape=jax.ShapeDtypeStruct((n, d), F32),
        grid=(n // tm,),
        in_specs=[pl.BlockSpec((tm, d), lambda i: (i, 0)), half, half, pl.BlockSpec((d, d), lambda i: (0, 0))],
        out_specs=pl.BlockSpec((tm, d), lambda i: (i, 0)),
        compiler_params=_params(("parallel",), 32),
        name="odd_out",
    )(x, o_attn.reshape(n, RW_WIDTH), o_ssm.reshape(n, RW_WIDTH), w_out)


def _store_token_tiles(ref, x, n_tok):
    for c in range(SUBLANES):
        ref[pl.ds(c, n_tok, stride=SUBLANES), :] = x[:, c * LANES:(c + 1) * LANES]


def _load_token_tiles(ref, n_tok):
    return jnp.concatenate([ref[pl.ds(c, n_tok, stride=SUBLANES), :] for c in range(SUBLANES)], axis=1)


def _router_kernel(tm, x_ref, g_ref, whi_ref, wlo_ref, b_ref, h_o, route_o, cnt_o, run):
    @pl.when(pl.program_id(0) == 0)
    def _():
        run[...] = jnp.zeros_like(run)

    x = x_ref[...]
    ms = jnp.mean(x * x, axis=-1, keepdims=True)
    h = x * lax.rsqrt(ms + RMS_EPS) * g_ref[...]
    _store_token_tiles(h_o, h, tm)
    h_hi = h.astype(BF16)
    h_lo = (h - h_hi.astype(F32)).astype(BF16)
    logits = (jnp.dot(h_hi, whi_ref[...], preferred_element_type=F32)
              + jnp.dot(h_hi, wlo_ref[...], preferred_element_type=F32)
              + jnp.dot(h_lo, whi_ref[...], preferred_element_type=F32)) + b_ref[...]
    lane = lax.broadcasted_iota(jnp.int32, logits.shape, 1).astype(F32)
    far = float(LANES)
    is_g = lane < MOE_GROUPS
    g_max = jnp.max(jnp.where(is_g, logits, NEG_BIG), axis=-1, keepdims=True)
    g_sum = jnp.sum(jnp.where(is_g, jnp.exp(jnp.minimum(logits - g_max, 0.0)), 0.0), axis=-1, keepdims=True)
    g_top = jnp.min(jnp.where(is_g & (logits == g_max), lane, far), axis=-1, keepdims=True)
    lo = MOE_GROUPS + MOE_PER_GROUP * g_top
    in_grp = (lane >= lo) & (lane < lo + MOE_PER_GROUP)
    e1 = jnp.max(jnp.where(in_grp, logits, NEG_BIG), axis=-1, keepdims=True)
    i1 = jnp.min(jnp.where(in_grp & (logits == e1), lane, far), axis=-1, keepdims=True)
    rest = in_grp & (lane != i1)
    e2 = jnp.max(jnp.where(rest, logits, NEG_BIG), axis=-1, keepdims=True)
    i2 = jnp.min(jnp.where(rest & (logits == e2), lane, far), axis=-1, keepdims=True)
    ratio = jnp.exp(e2 - e1)
    gate1 = 1.0 / (g_sum * (1.0 + ratio))
    gate2 = gate1 * ratio
    ex1 = i1 - MOE_GROUPS
    ex2 = i2 - MOE_GROUPS
    oh1 = lane == ex1
    oh2 = lane == ex2
    before = (lax.broadcasted_iota(jnp.int32, (tm, tm), 0) > lax.broadcasted_iota(jnp.int32, (tm, tm), 1)).astype(BF16)
    pre1 = jnp.dot(before, oh1.astype(BF16), preferred_element_type=F32)
    pre2 = jnp.dot(before, oh2.astype(BF16), preferred_element_type=F32)
    tot1 = jnp.sum(oh1.astype(F32), axis=0, keepdims=True)
    tot2 = jnp.sum(oh2.astype(F32), axis=0, keepdims=True)
    base = run[...]
    rank1 = jnp.sum(jnp.where(oh1, base + pre1, 0.0), axis=-1, keepdims=True)
    rank2 = jnp.sum(jnp.where(oh2, base + tot1 + pre2, 0.0), axis=-1, keepdims=True)
    run[...] = base + tot1 + tot2
    cnt_o[...] = base + tot1 + tot2
    route_o[...] = jnp.where(lane == 0, ex1, jnp.where(lane == 1, ex2, jnp.where(lane == 2, gate1, jnp.where(
        lane == 3, gate2, jnp.where(lane == 4, rank1, jnp.where(lane == 5, rank2, 0.0))))))


def _router(x, g, w_hi, w_lo, bias, tm=512):
    n, d = x.shape
    assert d == SUBLANES * LANES
    tm = min(tm, n)
    return pl.pallas_call(
        functools.partial(_router_kernel, tm),
        out_shape=(jax.ShapeDtypeStruct((n * SUBLANES, LANES), F32), jax.ShapeDtypeStruct((n, LANES), F32),
                   jax.ShapeDtypeStruct((1, LANES), F32)),
        grid=(n // tm,),
        in_specs=[pl.BlockSpec((tm, d), lambda i: (i, 0)), pl.BlockSpec((1, d), lambda i: (0, 0)),
                  pl.BlockSpec((d, LANES), lambda i: (0, 0)), pl.BlockSpec((d, LANES), lambda i: (0, 0)),
                  pl.BlockSpec((1, LANES), lambda i: (0, 0))],
        out_specs=(pl.BlockSpec((tm * SUBLANES, LANES), lambda i: (i, 0)), pl.BlockSpec((tm, LANES), lambda i: (i, 0)),
                   pl.BlockSpec((1, LANES), lambda i: (0, 0))),
        scratch_shapes=[pltpu.VMEM((1, LANES), F32)],
        compiler_params=_params(("arbitrary",), 32),
        name="moe_router",
    )(x, g.reshape(1, d), w_hi, w_lo, bias)


DMA_UNROLL = 8


def _tile_at(ref, tok):
    return ref.at[pl.ds(pl.multiple_of(tok * SUBLANES, SUBLANES), SUBLANES)]


def _dispatch_kernel(rows, dest_ref, zblk_ref, h_ref, xs_ref, zbuf, sem, zsem):
    @pl.when(pl.program_id(0) == 0)
    def _():
        zbuf[...] = jnp.zeros_like(zbuf)

        def zero_copy(j):
            start = pl.multiple_of(jnp.maximum(zblk_ref[0, j], 0) * (MOE_ROWS * SUBLANES), MOE_ROWS * SUBLANES)
            return pltpu.make_async_copy(zbuf, xs_ref.at[pl.ds(start, MOE_ROWS * SUBLANES)], zsem.at[0])

        def z_issue(j, c):
            pl.when(zblk_ref[0, j] >= 0)(lambda: zero_copy(j).start())
            return c

        def z_drain(j, c):
            pl.when(zblk_ref[0, j] >= 0)(lambda: zero_copy(j).wait())
            return c

        lax.fori_loop(0, zblk_ref.shape[1], z_issue, 0)
        lax.fori_loop(0, zblk_ref.shape[1], z_drain, 0)

    def issue(i, c):
        for choice in range(2):
            pltpu.make_async_copy(_tile_at(h_ref, i), _tile_at(xs_ref, dest_ref[0, 0, 2 * i + choice]),
                                  sem.at[choice]).start(priority=choice)
        return c

    lax.fori_loop(0, rows, issue, 0, unroll=DMA_UNROLL)
    for choice in range(2):
        pltpu.make_async_copy(h_ref, xs_ref.at[pl.ds(0, rows * SUBLANES)], sem.at[choice]).wait()


def _dispatch(h_tiles, dest, zero_blocks, n_pad, rows=256):
    n = h_tiles.shape[0] // SUBLANES
    rows = min(rows, n)
    nb = n // rows
    return pl.pallas_call(
        functools.partial(_dispatch_kernel, rows),
        out_shape=jax.ShapeDtypeStruct((n_pad * SUBLANES, LANES), h_tiles.dtype),
        grid=(nb,),
        in_specs=[pl.BlockSpec((1, 1, 2 * rows), lambda i: (i, 0, 0), memory_space=pltpu.SMEM),
                  pl.BlockSpec((1, zero_blocks.shape[0]), lambda i: (0, 0), memory_space=pltpu.SMEM),
                  pl.BlockSpec((rows * SUBLANES, LANES), lambda i: (i, 0))],
        out_specs=pl.BlockSpec(memory_space=pl.ANY),
        scratch_shapes=[pltpu.VMEM((MOE_ROWS * SUBLANES, LANES), h_tiles.dtype),
                        pltpu.SemaphoreType.DMA((2,)), pltpu.SemaphoreType.DMA((1,))],
        compiler_params=_params(("arbitrary",), 16),
        name="moe_dispatch",
    )(dest.reshape(nb, 1, 2 * rows), zero_blocks.reshape(1, -1), h_tiles)


def _moe_mlp_kernel(be_ref, x_ref, wg_ref, wu_ref, wd_ref, o_ref, wg_s, wu_s, wd_s):
    i = pl.program_id(0)

    @pl.when((i == 0) | (be_ref[i] != be_ref[jnp.maximum(i - 1, 0)]))
    def _():
        wg_s[...] = wg_ref[0].astype(BF16)
        wu_s[...] = wu_ref[0].astype(BF16)
        wd_s[...] = wd_ref[0].astype(BF16)

    x = _load_token_tiles(x_ref, MOE_ROWS).astype(BF16)
    hg = jnp.dot(x, wg_s[...], preferred_element_type=F32)
    hu = jnp.dot(x, wu_s[...], preferred_element_type=F32)
    hid = hg * _sigmoid(hg) * hu
    _store_token_tiles(o_ref, jnp.dot(hid.astype(BF16), wd_s[...], preferred_element_type=F32), MOE_ROWS)


def _moe_mlp(xs, block_expert, w_gate, w_up, w_down):
    n_pad = xs.shape[0] // SUBLANES
    _, d, hid = w_gate.shape
    nb = n_pad // MOE_ROWS
    tiles = pl.BlockSpec((MOE_ROWS * SUBLANES, LANES), lambda i, be: (i, 0))
    grid_spec = pltpu.PrefetchScalarGridSpec(
        num_scalar_prefetch=1,
        grid=(nb,),
        in_specs=[tiles,
                  pl.BlockSpec((1, d, hid), lambda i, be: (be[i], 0, 0)),
                  pl.BlockSpec((1, d, hid), lambda i, be: (be[i], 0, 0)),
                  pl.BlockSpec((1, hid, d), lambda i, be: (be[i], 0, 0))],
        out_specs=tiles,
        scratch_shapes=[pltpu.VMEM((d, hid), BF16), pltpu.VMEM((d, hid), BF16), pltpu.VMEM((hid, d), BF16)],
    )
    return pl.pallas_call(
        _moe_mlp_kernel,
        out_shape=jax.ShapeDtypeStruct(xs.shape, F32),
        grid_spec=grid_spec,
        compiler_params=_params(("arbitrary",), 40),
        name="moe_mlp",
    )(block_expert, xs, w_gate, w_up, w_down)


def _combine_kernel(rows, pos_ref, x_ref, route_ref, ys_ref, o_ref, buf0, buf1, sem):
    bufs = (buf0, buf1)

    def issue(i, c):
        for choice in range(2):
            pltpu.make_async_copy(_tile_at(ys_ref, pos_ref[0, 0, 2 * i + choice]), _tile_at(bufs[choice], i),
                                  sem.at[choice]).start(priority=choice)
        return c

    lax.fori_loop(0, rows, issue, 0, unroll=DMA_UNROLL)
    for choice in range(2):
        pltpu.make_async_copy(ys_ref.at[pl.ds(0, rows * SUBLANES)], bufs[choice], sem.at[choice]).wait()
    route = route_ref[...]
    o_ref[...] = (x_ref[...] + route[:, 2:3] * _load_token_tiles(buf0, rows)
                  + route[:, 3:4] * _load_token_tiles(buf1, rows))


def _combine(x, ys, route, pos, rows=256):
    n, d = x.shape
    rows = min(rows, n)
    nb = n // rows
    buf = pltpu.VMEM((rows * SUBLANES, LANES), F32)
    return pl.pallas_call(
        functools.partial(_combine_kernel, rows),
        out_shape=jax.ShapeDtypeStruct((n, d), F32),
        grid=(nb,),
        in_specs=[pl.BlockSpec((1, 1, 2 * rows), lambda i: (i, 0, 0), memory_space=pltpu.SMEM),
                  pl.BlockSpec((rows, d), lambda i: (i, 0)),
                  pl.BlockSpec((rows, LANES), lambda i: (i, 0)),
                  pl.BlockSpec(memory_space=pl.ANY)],
        out_specs=pl.BlockSpec((rows, d), lambda i: (i, 0)),
        scratch_shapes=[buf, buf, pltpu.SemaphoreType.DMA((2,))],
        compiler_params=_params(("arbitrary",), 16),
        name="moe_combine",
    )(pos.reshape(nb, 1, 2 * rows), x, route, ys)


def _hier_moe(x, g, w_group, b_group, w_expert, b_expert, w_gate, w_up, w_down):
    n, d = x.shape
    w_route = jnp.zeros((d, LANES), F32).at[:, :MOE_GROUPS].set(w_group)
    w_route = w_route.at[:, MOE_GROUPS:MOE_GROUPS + MOE_EXPERTS].set(w_expert)
    w_hi = w_route.astype(BF16)
    w_lo = (w_route - w_hi.astype(F32)).astype(BF16)
    bias = jnp.zeros((1, LANES), F32).at[0, :MOE_GROUPS].set(b_group)
    bias = bias.at[0, MOE_GROUPS:MOE_GROUPS + MOE_EXPERTS].set(b_expert)
    h, route, counts = _router(x, g, w_hi, w_lo, bias)

    counts = counts[0, :MOE_EXPERTS].astype(jnp.int32)
    padded = (counts + MOE_ROWS - 1) // MOE_ROWS * MOE_ROWS
    cum_padded = jnp.cumsum(padded)
    expert = route[:, 0:2].astype(jnp.int32)
    dest = ((cum_padded - padded)[expert] + route[:, 4:6].astype(jnp.int32)).reshape(-1)
    n_blocks = 2 * n // MOE_ROWS + MOE_EXPERTS
    block_start = jnp.arange(n_blocks, dtype=jnp.int32) * MOE_ROWS
    block_expert = jnp.minimum(jnp.sum((block_start[:, None] >= cum_padded[None, :]).astype(jnp.int32), axis=1),
                               MOE_EXPERTS - 1)

    last_block = cum_padded // MOE_ROWS - 1
    last_block = jnp.where((padded > 0) & (last_block < n_blocks - MOE_EXPERTS), last_block, -1)
    zero_blocks = jnp.concatenate([last_block, jnp.arange(n_blocks - MOE_EXPERTS, n_blocks, dtype=jnp.int32)])
    xs = _dispatch(h, dest, zero_blocks.astype(jnp.int32), n_blocks * MOE_ROWS)
    ys = _moe_mlp(xs, block_expert, w_gate, w_up, w_down)
    return _combine(x, ys, route, dest)


def _even_layer(x, bsz, t_len, norm_g, w_in, prm, v_first):
    n = bsz * t_len
    u = _norm_matmul(x, norm_g, w_in.astype(BF16)).reshape(bsz, t_len, -1)
    r, lw, k, v, a, b, g, pool = _even_prep(u, prm, v_first)
    y = _rwkv_chunk(r, lw, k, v, a, b)
    x = _even_out(x, y, r, k, v, g, pool, prm)
    return x, v


def _odd_layer(x, bsz, t_len, layer_idx, norm_g, w_in, prm):
    n = bsz * t_len
    u = _norm_matmul(x, norm_g, w_in.astype(BF16)).reshape(bsz, t_len, -1)
    q, k = _qk_prep(u, prm["q_norm"], prm["k_norm"])
    lam_init = 0.8 - 0.6 * math.exp(-0.3 * layer_idx)
    lam = (jnp.exp(jnp.sum(prm["lam_q1"] * prm["lam_k1"])) - jnp.exp(jnp.sum(prm["lam_q2"] * prm["lam_k2"]))
           + lam_init)
    o_attn = _diff_attn(q, k, u, lam, prm["subln"], 1.0 - lam_init)
    return _odd_out(x, o_attn, _s5(u, prm), prm["w_out"])


def _s5_params(a_re, a_im, log_step, b_re, b_im, c_re, c_im, d_skip, w_glu):
    lam = lax.complex(jnp.minimum(a_re, -1e-4), a_im)
    lam_bar = jnp.exp(lam * jnp.exp(log_step))
    b_bar = ((lam_bar - 1.0) / lam)[..., None] * lax.complex(b_re, b_im)
    gpt = LANES // S5_GROUP_DIM
    eye = jnp.eye(gpt, dtype=F32)

    def in_map(part):
        p = part.reshape(S5_TILES, gpt, S5_STATE, S5_GROUP_DIM)
        return jnp.einsum("jgpc,gh->jgchp", p, eye).reshape(S5_TILES, LANES, gpt * S5_STATE)

    def out_map(part):
        p = part.reshape(S5_TILES, gpt, S5_GROUP_DIM, S5_STATE)
        return jnp.einsum("jgcp,gh->jgphc", p, eye).reshape(S5_TILES, gpt * S5_STATE, LANES)

    bt = jnp.concatenate([in_map(jnp.real(b_bar)), in_map(jnp.imag(b_bar))], axis=2).astype(BF16)
    ct = jnp.concatenate([out_map(c_re), -out_map(c_im)], axis=1).astype(BF16)
    return {"bt": bt, "ct": ct, "lam_re": jnp.real(lam_bar).reshape(1, -1), "lam_im": jnp.imag(lam_bar).reshape(1, -1),
            "d": d_skip, "w_glu": w_glu.astype(BF16)}


def _block_diag(blocks):
    g, r, c = blocks.shape
    return jnp.einsum("grc,gh->grhc", blocks, jnp.eye(g, dtype=blocks.dtype)).reshape(g * r, g * c)


def kernel(x, norm_mix_g, norm_ffn_g,
           even_w_in, rw_mu, rw_w0, rw_w2, rw_a0, rw_a2, rw_g2, rw_k_k, rw_k_a, rw_r_k,
           rw_ln_g, rw_ln_b, rw_v0, rw_v1, rw_v2, pool_w, pool_scale, even_w_out,
           odd_w_in, da_q_norm, da_k_norm, da_lam_q1, da_lam_k1, da_lam_q2, da_lam_k2, da_subln,
           s5_a_re, s5_a_im, s5_log_step, s5_b_re, s5_b_im, s5_c_re, s5_c_im, s5_d, s5_w_glu,
           odd_w_out,
           moe_w_group, moe_b_group, moe_w_expert, moe_b_expert, moe_w_gate, moe_w_up, moe_w_down):
    bsz, t_len, d = x.shape
    depth = norm_mix_g.shape[0]
    xf = x.reshape(bsz * t_len, d)
    v_first = None
    for layer in range(depth):
        i = layer // 2
        if layer % 2 == 0:
            rank = rw_w2.shape[1]
            wa = jnp.zeros((LANES, 2 * RW_WIDTH), F32)
            wa = wa.at[:rank, :RW_WIDTH].set(rw_w2[i]).at[rank:, RW_WIDTH:].set(rw_a2[i])
            prm = {"mu": rw_mu[i], "w0": rw_w0[i], "a0": rw_a0[i], "wa": wa.astype(BF16), "g2": rw_g2[i].astype(BF16),
                   "k_k": rw_k_k[i], "k_a": rw_k_a[i], "r_k": rw_r_k[i], "ln_g": rw_ln_g[i], "ln_b": rw_ln_b[i],
                   "pool_bd": _block_diag(pool_w[i]).astype(BF16), "pool_scale": pool_scale[i],
                   "w_out": even_w_out[i].astype(BF16)}
            if v_first is not None:
                vr = rw_v1.shape[2]
                prm["v0"] = rw_v0[i - 1]
                prm["v1"] = jnp.zeros((RW_WIDTH, LANES), F32).at[:, :vr].set(rw_v1[i - 1]).astype(BF16)
                prm["v2"] = jnp.zeros((LANES, RW_WIDTH), F32).at[:vr, :].set(rw_v2[i - 1]).astype(BF16)
            xf, v_new = _even_layer(xf, bsz, t_len, norm_mix_g[layer], even_w_in[i], prm, v_first)
            if v_first is None:
                v_first = v_new
        else:
            prm = _s5_params(s5_a_re[i], s5_a_im[i], s5_log_step[i], s5_b_re[i], s5_b_im[i], s5_c_re[i],
                             s5_c_im[i], s5_d[i].reshape(-1), s5_w_glu[i])
            prm.update({"q_norm": da_q_norm[i], "k_norm": da_k_norm[i], "lam_q1": da_lam_q1[i],
                        "lam_k1": da_lam_k1[i], "lam_q2": da_lam_q2[i], "lam_k2": da_lam_k2[i],
                        "subln": da_subln[i], "w_out": odd_w_out[i].astype(BF16)})
            xf = _odd_layer(xf, bsz, t_len, layer, norm_mix_g[layer], odd_w_in[i], prm)
        xf = _hier_moe(xf, norm_ffn_g[layer], moe_w_group[layer], moe_b_group[layer], moe_w_expert[layer],
                       moe_b_expert[layer], moe_w_gate[layer], moe_w_up[layer], moe_w_down[layer])
    return xf.reshape(bsz, t_len, d)
```

```python
import functools
import math

import jax
import jax.numpy as jnp
from jax import lax
from jax.experimental import pallas as pl
from jax.experimental.pallas import tpu as pltpu

F32 = jnp.float32
BF16 = jnp.bfloat16

LANES = 128
SUBLANES = 8
VMEM_BYTES_V7X = 64 * 1024 * 1024

D_MODEL = 1024
HEAD = 64
RW_WIDTH = 512
RW_SHIFT_COLS = 3 * RW_WIDTH + 64 + 64 + 128
RW_LN_EPS = 64e-5
POOL_WINDOWS = (2, 4, 8, 16)
POOL_HALO = 16
DA_HEADS = 4
SUBLN_EPS = 1e-5
ROPE_THETA = 10000.0
S5_GROUP_DIM = 16
S5_STATE = 64
S5_TILES = 4
S5_BATCH_GROUP = SUBLANES
MOE_GROUPS = 4
MOE_PER_GROUP = 8
MOE_EXPERTS = 32
MOE_ROWS = 256
RMS_EPS = 1e-6
CHUNK = 64
NEG_BIG = -1e30


def _params(semantics, vmem_mib):
    return pltpu.CompilerParams(dimension_semantics=semantics,
                                vmem_limit_bytes=min(vmem_mib * 1024 * 1024, VMEM_BYTES_V7X - 8 * 1024 * 1024))


def _bdot(a, b):
    return jnp.dot(a.astype(BF16), b.astype(BF16), preferred_element_type=F32)


def _bdot_nt(a, b):
    return lax.dot_general(a.astype(BF16), b.astype(BF16), (((1,), (1,)), ((), ())),
                           preferred_element_type=F32)


def _split_dot(x, m_bf16):
    hi = x.astype(BF16)
    lo = (x - hi.astype(F32)).astype(BF16)
    return (jnp.dot(hi, m_bf16, preferred_element_type=F32)
            + jnp.dot(lo, m_bf16, preferred_element_type=F32))


def _seg_ones(width):
    r = lax.broadcasted_iota(jnp.int32, (LANES, LANES), 0)
    c = lax.broadcasted_iota(jnp.int32, (LANES, LANES), 1)
    sh = int(math.log2(width))
    return ((r >> sh) == (c >> sh)).astype(BF16)


def _segsum(x, seg):
    tiles = [_split_dot(x[:, j * LANES:(j + 1) * LANES], seg) for j in range(x.shape[1] // LANES)]
    return tiles[0] if len(tiles) == 1 else jnp.concatenate(tiles, axis=1)


def _sigmoid(x):
    return 1.0 / (1.0 + jnp.exp(-x))


def _norm_matmul_kernel(x_ref, g_ref, w_ref, o_ref):
    x = x_ref[...]
    ms = jnp.mean(x * x, axis=-1, keepdims=True)
    h = x * lax.rsqrt(ms + RMS_EPS) * g_ref[...]
    o_ref[...] = jnp.dot(h.astype(BF16), w_ref[...], preferred_element_type=F32)


def _norm_matmul(x, g, w_bf16, tm=512):
    n, d = x.shape
    c = w_bf16.shape[1]
    tm = min(tm, n)
    return pl.pallas_call(
        _norm_matmul_kernel,
        out_shape=jax.ShapeDtypeStruct((n, c), F32),
        grid=(n // tm,),
        in_specs=[pl.BlockSpec((tm, d), lambda i: (i, 0)),
                  pl.BlockSpec((1, d), lambda i: (0, 0)),
                  pl.BlockSpec((d, c), lambda i: (0, 0))],
        out_specs=pl.BlockSpec((tm, c), lambda i: (i, 0)),
        compiler_params=_params(("parallel",), 48),
        name="norm_matmul",
    )(x, g.reshape(1, d), w_bf16)


def _even_prep_kernel(has_vres, tb, *refs):
    if has_vres:
        (u_ref, mu_ref, w0_ref, a0_ref, wa_ref, g2_ref, kk_ref, ka_ref, pw_ref, ps_ref,
         vf_ref, v0_ref, v1_ref, v2_ref,
         r_o, lw_o, k_o, v_o, a_o, b_o, g_o, pool_o, carry) = refs
    else:
        (u_ref, mu_ref, w0_ref, a0_ref, wa_ref, g2_ref, kk_ref, ka_ref, pw_ref, ps_ref,
         r_o, lw_o, k_o, v_o, a_o, b_o, g_o, pool_o, carry) = refs
    ti = pl.program_id(1)

    @pl.when(ti == 0)
    def _():
        carry[...] = jnp.zeros_like(carry)

    u = u_ref[0]
    ext = jnp.concatenate([carry[...], u], axis=0)
    carry[...] = u[tb - POOL_HALO:, :]

    p1 = ext[:, RW_SHIFT_COLS:]
    p2 = p1 + pltpu.roll(p1, 1, 0)
    p4 = p2 + pltpu.roll(p2, 2, 0)
    p8 = p4 + pltpu.roll(p4, 4, 0)
    p16 = p8 + pltpu.roll(p8, 8, 0)
    lane = lax.broadcasted_iota(jnp.int32, (1, RW_WIDTH), 1)
    grp = lane >> 7
    sums = jnp.where(grp == 0, p2, jnp.where(grp == 1, p4, jnp.where(grp == 2, p8, p16)))[POOL_HALO:]
    win = jnp.where(grp == 0, 2.0, jnp.where(grp == 1, 4.0, jnp.where(grp == 2, 8.0, 16.0)))
    n_seen = (ti * tb + lax.broadcasted_iota(jnp.int32, (tb, 1), 0) + 1).astype(F32)
    d = sums / jnp.minimum(n_seen, win) - u[:, RW_SHIFT_COLS:]
    pool_o[0] = (_bdot(d, pw_ref[...]) * ps_ref[...]).astype(pool_o.dtype)

    u_rw = u[:, :RW_SHIFT_COLS]
    prev = pltpu.roll(ext[:, :RW_SHIFT_COLS], 1, 0)[POOL_HALO:]
    m = u_rw + (prev - u_rw) * mu_ref[...]
    r = m[:, :RW_WIDTH]
    k = m[:, RW_WIDTH:2 * RW_WIDTH]
    v = m[:, 2 * RW_WIDTH:3 * RW_WIDTH]
    dwa = m[:, 3 * RW_WIDTH:3 * RW_WIDTH + LANES]
    dg = m[:, 3 * RW_WIDTH + LANES:]
    l128 = lax.broadcasted_iota(jnp.int32, (1, LANES), 1)
    dwa = jnp.where(l128 < HEAD, jnp.tanh(dwa), dwa)
    x12 = _bdot(dwa, wa_ref[...])
    z = -(w0_ref[...] + x12[:, :RW_WIDTH])
    softplus = jnp.maximum(z, 0.0) + jnp.log(1.0 + jnp.exp(-jnp.abs(z)))
    lw = -jnp.exp(-softplus - 0.5)
    a_i = _sigmoid(a0_ref[...] + x12[:, RW_WIDTH:])
    g_o[0] = _bdot(_sigmoid(dg), g2_ref[...]).astype(g_o.dtype)
    if has_vres:
        gate_v = _sigmoid(v0_ref[...] + _bdot(_bdot(v, v1_ref[...]), v2_ref[...]))
        v = v + (vf_ref[0] - v) * gate_v
    kk = k * kk_ref[...]
    ss = _segsum(kk * kk, _seg_ones(HEAD))
    kk = kk / jnp.maximum(jnp.sqrt(ss), 1e-12)
    r_o[0] = r.astype(r_o.dtype)
    lw_o[0] = lw
    k_o[0] = (k * (1.0 + (a_i - 1.0) * ka_ref[...])).astype(k_o.dtype)
    v_o[0] = v.astype(v_o.dtype)
    a_o[0] = (-kk).astype(a_o.dtype)
    b_o[0] = (kk * a_i).astype(b_o.dtype)


def _even_prep(u, prm, v_first, tb=256):
    bsz, t_len, cin = u.shape
    tb = min(tb, t_len)
    has_vres = v_first is not None
    row = lambda a: a.reshape(1, -1)
    full = lambda shape: pl.BlockSpec(shape, lambda b, t: (0,) * len(shape))
    seq = pl.BlockSpec((1, tb, RW_WIDTH), lambda b, t: (b, t, 0))
    ins = [u, row(prm["mu"]), row(prm["w0"]), row(prm["a0"]), prm["wa"], prm["g2"], row(prm["k_k"]),
           row(prm["k_a"]), prm["pool_bd"], row(prm["pool_scale"])]
    specs = [pl.BlockSpec((1, tb, cin), lambda b, t: (b, t, 0)), full((1, RW_SHIFT_COLS)),
             full((1, RW_WIDTH)), full((1, RW_WIDTH)), full((LANES, 2 * RW_WIDTH)),
             full((LANES, RW_WIDTH)), full((1, RW_WIDTH)), full((1, RW_WIDTH)),
             full((RW_WIDTH, RW_WIDTH)), full((1, RW_WIDTH))]
    if has_vres:
        ins += [v_first, row(prm["v0"]), prm["v1"], prm["v2"]]
        specs += [seq, full((1, RW_WIDTH)), full((RW_WIDTH, LANES)), full((LANES, RW_WIDTH))]
    out = lambda dt: jax.ShapeDtypeStruct((bsz, t_len, RW_WIDTH), dt)
    return pl.pallas_call(
        functools.partial(_even_prep_kernel, has_vres, tb),
        out_shape=(out(BF16), out(F32)) + (out(BF16),) * 6,
        grid=(bsz, t_len // tb),
        in_specs=specs,
        out_specs=(seq,) * 8,
        scratch_shapes=[pltpu.VMEM((POOL_HALO, cin), F32)],
        compiler_params=_params(("parallel", "arbitrary"), 48),
        name="even_prep",
    )(*ins)


def _rwkv_chunk_kernel(nb, r_ref, lw_ref, k_ref, v_ref, a_ref, b_ref, y_ref, s_ref):
    ci = pl.program_id(1)

    @pl.when(ci == 0)
    def _():
        s_ref[...] = jnp.zeros_like(s_ref)

    L = CHUNK
    n_pair = RW_WIDTH // LANES
    tri = (lax.broadcasted_iota(jnp.int32, (L, L), 0) >= lax.broadcasted_iota(jnp.int32, (L, L), 1)).astype(BF16)
    lane = lax.broadcasted_iota(jnp.int32, (1, LANES), 1)
    m_a = lane < HEAD
    t_idx = lax.broadcasted_iota(jnp.int32, (L, LANES), 0)
    s_idx = lax.broadcasted_iota(jnp.int32, (L, LANES), 1) & (HEAD - 1)
    strict = t_idx > s_idx
    incl = t_idx >= s_idx
    r128 = lax.broadcasted_iota(jnp.int32, (LANES, LANES), 0)
    c128 = lax.broadcasted_iota(jnp.int32, (LANES, LANES), 1)
    eye = (r128 == c128).astype(F32)
    same_head = (r128 >> 6) == (c128 >> 6)

    def only_a(x):
        return jnp.where(m_a, x, jnp.zeros_like(x))

    def only_b(x):
        return jnp.where(m_a, jnp.zeros_like(x), x)

    def stack(x):
        return jnp.concatenate([only_a(x), only_b(x)], axis=0)

    def stack_sw(x):
        return jnp.concatenate([only_b(x), only_a(x)], axis=0)

    def mm(x, y):
        return jnp.dot(x, y, preferred_element_type=F32)

    def mm_nt(x, y):
        return lax.dot_general(x, y, (((1,), (1,)), ((), ())), preferred_element_type=F32)

    chains = []
    for bi in range(nb):
        lw = lw_ref[bi]
        c = _split_dot_lhs(tri, lw)
        e_pos = jnp.exp(c)
        e_neg = jnp.exp(-c)
        a_all = (jnp.exp(c - lw) * a_ref[bi]).astype(BF16)
        b_all = (b_ref[bi] * e_neg).astype(BF16)
        k_all = (k_ref[bi] * e_neg).astype(BF16)
        r_all = r_ref[bi] * e_pos
        v_all = v_ref[bi].astype(BF16)
        for p in range(n_pair):
            sl = slice(p * LANES, (p + 1) * LANES)
            chains.append({"bi": bi, "p": p, "sl": sl, "at": a_all[:, sl], "bt": b_all[:, sl], "kt": k_all[:, sl],
                           "rt": r_all[:, sl], "vv": v_all[:, sl], "gl": e_pos[L - 1:L, sl]})

    for ch in chains:
        rt16 = ch["rt"].astype(BF16)
        bk = jnp.concatenate([ch["bt"], ch["kt"]], axis=0)
        kb = jnp.concatenate([ch["kt"], ch["bt"]], axis=0)
        s_a = mm_nt(jnp.concatenate([only_a(ch["at"]), only_a(rt16)], axis=0), bk)
        s_b = mm_nt(jnp.concatenate([only_b(ch["at"]), only_b(rt16)], axis=0), kb)
        m_ha = jnp.where(strict, s_a[:L], 0.0)
        n_ha = jnp.where(incl, s_a[L:], 0.0)
        m_hb = jnp.where(strict, s_b[:L], 0.0)
        n_hb = jnp.where(incl, s_b[L:], 0.0)
        ch["bk"] = bk
        ch["bdm"] = jnp.concatenate([only_a(m_ha), only_b(m_hb)], axis=0)
        ch["ak_sw"] = jnp.where(m_a, m_hb, m_ha).astype(BF16)
        ch["n_lhs"] = jnp.concatenate([jnp.where(m_a, n_ha, n_hb),
                                       jnp.where(m_a, n_hb, n_ha)], axis=1).astype(BF16)
        ch["t_inv"] = eye + ch["bdm"]
        ch["m_pow"] = ch["bdm"].astype(BF16)
    for ch in chains:
        ch["w"] = mm(ch["ak_sw"], stack_sw(ch["vv"])).astype(BF16)
    for _ in range(int(math.log2(L)) - 1):
        for ch in chains:
            ch["m_pow"] = mm(ch["m_pow"], ch["m_pow"]).astype(BF16)
        for ch in chains:
            ch["t_inv"] = ch["t_inv"] + mm(ch["t_inv"].astype(BF16), ch["m_pow"])
    for ch in chains:
        t_pk = (ch["t_inv"][:L] + ch["t_inv"][L:]).astype(BF16)
        au = mm(t_pk, jnp.concatenate([stack(ch["at"]), stack(ch["w"])], axis=1))
        ch["a_hat"] = au[:, :LANES].astype(BF16)
        ch["u_hat"] = au[:, LANES:].astype(BF16)
    for ch in chains:
        rhs = jnp.concatenate([
            jnp.concatenate([stack(ch["u_hat"]), stack(ch["a_hat"])], axis=1),
            jnp.concatenate([stack_sw(ch["vv"]), jnp.zeros((LANES, LANES), BF16)], axis=1)], axis=0)
        yr = mm(ch["n_lhs"], rhs)
        ch["y_hat"] = yr[:, :LANES]
        ch["r_hat"] = (ch["rt"] + yr[:, LANES:]).astype(BF16)
        g_p = mm(ch["a_hat"].astype(F32).T.astype(BF16), ch["bt"])
        g_q = mm(jnp.concatenate([ch["u_hat"], ch["vv"]], axis=0).astype(F32).T.astype(BF16), ch["bk"])
        ch["p_bd"] = ((eye + jnp.where(same_head, g_p, 0.0)) * ch["gl"]).astype(BF16)
        ch["q_pk"] = jnp.where(m_a, g_q[:L], g_q[L:]) * ch["gl"]
    for ch in chains:
        s0 = s_ref[ch["bi"], ch["p"]]
        s16 = s0.astype(BF16)
        y_ref[ch["bi"], :, ch["sl"]] = mm_nt(ch["r_hat"], stack(s16)) + ch["y_hat"]
        s_ref[ch["bi"], ch["p"]] = mm(s16, ch["p_bd"]) + ch["q_pk"]


def _split_dot_lhs(m_bf16, x):
    hi = x.astype(BF16)
    lo = (x - hi.astype(F32)).astype(BF16)
    return (jnp.dot(m_bf16, hi, preferred_element_type=F32)
            + jnp.dot(m_bf16, lo, preferred_element_type=F32))


def _rwkv_chunk(r, lw, k, v, a, b, nb=4):
    bsz, t_len, _ = r.shape
    seq = pl.BlockSpec((nb, CHUNK, RW_WIDTH), lambda bi, ci: (bi, ci, 0))
    return pl.pallas_call(
        functools.partial(_rwkv_chunk_kernel, nb),
        out_shape=jax.ShapeDtypeStruct((bsz, t_len, RW_WIDTH), F32),
        grid=(bsz // nb, t_len // CHUNK),
        in_specs=[seq] * 6,
        out_specs=seq,
        scratch_shapes=[pltpu.VMEM((nb, RW_WIDTH // LANES, HEAD, LANES), F32)],
        compiler_params=_params(("parallel", "arbitrary"), 32),
        name="rwkv_chunk",
    )(r, lw, k, v, a, b)


def _even_out_kernel(x_ref, y_ref, r_ref, k_ref, v_ref, g_ref, p_ref, lng_ref, lnb_ref, rk_ref, wo_ref, o_ref):
    seg = _seg_ones(HEAD)
    y = y_ref[...]
    mean = _segsum(y, seg) * (1.0 / HEAD)
    yc = y - mean
    var = _segsum(yc * yc, seg) * (1.0 / HEAD)
    yn = yc * lax.rsqrt(var + RW_LN_EPS) * lng_ref[...] + lnb_ref[...]
    bonus = _segsum(r_ref[...].astype(F32) * k_ref[...].astype(F32) * rk_ref[...], seg) * v_ref[...].astype(F32)
    o_rw = (yn + bonus) * g_ref[...].astype(F32)
    cat = jnp.concatenate([o_rw.astype(BF16), p_ref[...]], axis=1)
    o_ref[...] = x_ref[...] + jnp.dot(cat, wo_ref[...], preferred_element_type=F32)


def _even_out(x, y, r, k, v, g, pool, prm, tm=512):
    n, d = x.shape
    tm = min(tm, n)
    half = pl.BlockSpec((tm, RW_WIDTH), lambda i: (i, 0))
    vec = pl.BlockSpec((1, RW_WIDTH), lambda i: (0, 0))
    fl = lambda a: a.reshape(n, RW_WIDTH)
    return pl.pallas_call(
        _even_out_kernel,
        out_shape=jax.ShapeDtypeStruct((n, d), F32),
        grid=(n // tm,),
        in_specs=[pl.BlockSpec((tm, d), lambda i: (i, 0))] + [half] * 6 + [vec] * 3
                 + [pl.BlockSpec((d, d), lambda i: (0, 0))],
        out_specs=pl.BlockSpec((tm, d), lambda i: (i, 0)),
        compiler_params=_params(("parallel",), 48),
        name="even_out",
    )(x, fl(y), fl(r), fl(k), fl(v), fl(g), fl(pool), prm["ln_g"].reshape(1, -1), prm["ln_b"].reshape(1, -1),
      prm["r_k"].reshape(1, -1), prm["w_out"])


def _qk_prep_kernel(u_ref, gq_ref, gk_ref, cos_ref, sin_ref, q_o, k_o):
    x = u_ref[0]
    cos = cos_ref[...]
    sin = sin_ref[...]
    seg = _seg_ones(HEAD)
    lane = lax.broadcasted_iota(jnp.int32, (1, LANES), 1)
    first = (lane & (HEAD - 1)) < HEAD // 2
    nq = RW_WIDTH // LANES
    for j in range(2 * nq):
        xt = x[:, j * LANES:(j + 1) * LANES]
        ms = _split_dot(xt * xt, seg) * (1.0 / HEAD)
        gain = gq_ref[...] if j < nq else gk_ref[...]
        xn = xt * lax.rsqrt(ms + RMS_EPS) * gain
        partner = jnp.where(first, pltpu.roll(xn, LANES - HEAD // 2, 1), pltpu.roll(xn, HEAD // 2, 1))
        out = xn * cos + partner * sin
        if j < nq:
            q_o[0, :, j * LANES:(j + 1) * LANES] = (out * (HEAD ** -0.5)).astype(BF16)
        else:
            k_o[0, :, (j - nq) * LANES:(j - nq + 1) * LANES] = out.astype(BF16)


def _qk_prep(u, q_norm, k_norm, tb=256):
    bsz, t_len, _ = u.shape
    tb = min(tb, t_len)
    inv_freq = ROPE_THETA ** (-jnp.arange(0, HEAD, 2, dtype=F32) / HEAD)
    ang = jnp.arange(t_len, dtype=F32)[:, None] * inv_freq[None, :]
    cos = jnp.tile(jnp.cos(ang), (1, LANES // (HEAD // 2)))
    sin_half = jnp.concatenate([-jnp.sin(ang), jnp.sin(ang)], axis=1)
    sin = jnp.tile(sin_half, (1, LANES // HEAD))
    tile2 = lambda g: jnp.tile(g, LANES // HEAD).reshape(1, LANES)
    out = jax.ShapeDtypeStruct((bsz, t_len, RW_WIDTH), BF16)
    tab = pl.BlockSpec((tb, LANES), lambda b, t: (t, 0))
    vec = pl.BlockSpec((1, LANES), lambda b, t: (0, 0))
    seq = pl.BlockSpec((1, tb, RW_WIDTH), lambda b, t: (b, t, 0))
    return pl.pallas_call(
        _qk_prep_kernel,
        out_shape=(out, out),
        grid=(bsz, t_len // tb),
        in_specs=[pl.BlockSpec((1, tb, 2 * RW_WIDTH), lambda b, t: (b, t, 0)), vec, vec, tab, tab],
        out_specs=(seq, seq),
        compiler_params=_params(("parallel", "parallel"), 32),
        name="qk_prep",
    )(u, tile2(q_norm), tile2(k_norm), cos, sin)


def _diff_attn_kernel(tq, nh, out_scale, q_ref, k_ref, v_ref, lam_ref, sub_ref, o_ref):
    qi = pl.program_id(2)
    lane = lax.broadcasted_iota(jnp.int32, (1, LANES), 1)
    m_a = lane < HEAD

    def prefix(n_blk):
        kl = n_blk * tq
        on_diag = lax.broadcasted_iota(jnp.int32, (tq, tq), 0) >= lax.broadcasted_iota(jnp.int32, (tq, tq), 1)
        qs, ks, vs = [], [], []
        for h in range(nh):
            sl = slice(h * LANES, (h + 1) * LANES)
            q = q_ref[0, :, sl]
            zero = jnp.zeros_like(q)
            qs += [jnp.where(m_a, q, zero), jnp.where(m_a, zero, q)]
            ks += [k_ref[0, :kl, sl]] * 2
            vs += [v_ref[0, :kl, sl].astype(BF16)] * 2
        s = [lax.dot_general(q, k, (((1,), (1,)), ((), ())), preferred_element_type=F32) for q, k in zip(qs, ks)]
        last = [jnp.where(on_diag, x[:, kl - tq:], NEG_BIG) for x in s]
        if n_blk > 1:
            last = [jnp.concatenate([x[:, :kl - tq], y], axis=1) for x, y in zip(s, last)]
        top = [jnp.max(x, axis=-1, keepdims=True) for x in last]
        e = [jnp.exp(x - m) for x, m in zip(last, top)]
        den = [jnp.sum(x, axis=-1, keepdims=True) for x in e]
        pv = [jnp.dot(x.astype(BF16), v, preferred_element_type=F32) for x, v in zip(e, vs)]
        for h in range(nh):
            sl = slice(h * LANES, (h + 1) * LANES)
            o = pv[2 * h] / den[2 * h] - lam_ref[...] * (pv[2 * h + 1] / den[2 * h + 1])
            ms = jnp.mean(o * o, axis=-1, keepdims=True)
            o_ref[0, :, sl] = o * lax.rsqrt(ms + SUBLN_EPS) * sub_ref[...] * out_scale

    for blk in range(k_ref.shape[1] // tq):
        pl.when(qi == blk)(functools.partial(prefix, blk + 1))


def _diff_attn(q, k, u, lam, subln, out_scale, tq=256, nh=2):
    bsz, t_len, _ = q.shape
    tq = min(tq, t_len)
    wide = nh * LANES
    v_off = 2 * RW_WIDTH // wide
    return pl.pallas_call(
        functools.partial(_diff_attn_kernel, tq, nh, out_scale),
        out_shape=jax.ShapeDtypeStruct((bsz, t_len, RW_WIDTH), F32),
        grid=(bsz, DA_HEADS // nh, t_len // tq),
        in_specs=[pl.BlockSpec((1, tq, wide), lambda b, h, i: (b, i, h)),
                  pl.BlockSpec((1, t_len, wide), lambda b, h, i: (b, 0, h)),
                  pl.BlockSpec((1, t_len, wide), lambda b, h, i: (b, 0, v_off + h)),
                  pl.BlockSpec((1, 1), lambda b, h, i: (0, 0)),
                  pl.BlockSpec((1, LANES), lambda b, h, i: (0, 0))],
        out_specs=pl.BlockSpec((1, tq, wide), lambda b, h, i: (b, i, h)),
        compiler_params=_params(("parallel", "parallel", "arbitrary"), 48),
        name="diff_attn",
    )(q, k, u, lam.reshape(1, 1), subln.reshape(1, LANES))


def _s5_kernel(tb, u_ref, bt_ref, lre_ref, lim_ref, ct_ref, d_ref, wg_ref, o_ref, xs, st):
    ti = pl.program_id(1)

    @pl.when(ti == 0)
    def _():
        st[...] = jnp.zeros_like(st)

    rows = tb * SUBLANES
    half = RW_WIDTH
    u = u_ref[0].reshape(rows, RW_WIDTH)
    for j in range(S5_TILES):
        xs[:, 2 * half * j:2 * half * (j + 1)] = _bdot(u[:, j * LANES:(j + 1) * LANES], bt_ref[j])
    for j in range(S5_TILES):
        re = slice(2 * half * j, 2 * half * j + half)
        im = slice(2 * half * j + half, 2 * half * (j + 1))
        lr = jnp.broadcast_to(lre_ref[:, half * j:half * (j + 1)], (SUBLANES, half))
        li = jnp.broadcast_to(lim_ref[:, half * j:half * (j + 1)], (SUBLANES, half))

        def step(t, carry, re=re, im=im, lr=lr, li=li):
            xr, xi = carry
            row = pl.multiple_of(t * SUBLANES, SUBLANES)
            nr = lr * xr - li * xi + xs[pl.ds(row, SUBLANES), re]
            ni = lr * xi + li * xr + xs[pl.ds(row, SUBLANES), im]
            xs[pl.ds(row, SUBLANES), re] = nr
            xs[pl.ds(row, SUBLANES), im] = ni
            return nr, ni

        xr, xi = lax.fori_loop(0, tb, step, (st[:, re], st[:, im]), unroll=4)
        st[:, re] = xr
        st[:, im] = xi
    y = jnp.concatenate([_bdot(xs[:, 2 * half * j:2 * half * (j + 1)], ct_ref[j]) for j in range(S5_TILES)], axis=1)
    y = y + d_ref[...] * u
    z = 0.5 * y * (1.0 + jnp.tanh(math.sqrt(2.0 / math.pi) * (y + 0.044715 * (y * y * y))))
    o = z * _sigmoid(_bdot(z, wg_ref[...]))
    o_ref[0] = o.reshape(tb, SUBLANES, RW_WIDTH)


def _s5(u5, prm, tb=128):
    ng, t_len, _, _ = u5.shape
    tb = min(tb, t_len)
    full = lambda shape: pl.BlockSpec(shape, lambda g, t: (0,) * len(shape))
    n_state = 2 * RW_WIDTH * S5_TILES
    return pl.pallas_call(
        functools.partial(_s5_kernel, tb),
        out_shape=jax.ShapeDtypeStruct(u5.shape, F32),
        grid=(ng, t_len // tb),
        in_specs=[pl.BlockSpec((1, tb, SUBLANES, RW_WIDTH), lambda g, t: (g, t, 0, 0)),
                  full((S5_TILES, LANES, 2 * RW_WIDTH)), full((1, n_state // 2)), full((1, n_state // 2)),
                  full((S5_TILES, 2 * RW_WIDTH, LANES)), full((1, RW_WIDTH)), full((RW_WIDTH, RW_WIDTH))],
        out_specs=pl.BlockSpec((1, tb, SUBLANES, RW_WIDTH), lambda g, t: (g, t, 0, 0)),
        scratch_shapes=[pltpu.VMEM((tb * SUBLANES, n_state), F32), pltpu.VMEM((SUBLANES, n_state), F32)],
        compiler_params=_params(("parallel", "arbitrary"), 48),
        name="s5_scan",
    )(u5, prm["bt"], prm["lam_re"], prm["lam_im"], prm["ct"], prm["d"].reshape(1, -1), prm["w_glu"])


def _odd_out_kernel(x_ref, a_ref, s_ref, wo_ref, o_ref):
    cat = jnp.concatenate([a_ref[...], s_ref[...]], axis=1)
    o_ref[...] = x_ref[...] + _bdot(cat, wo_ref[...])


def _odd_out(x, o_attn, o_ssm, w_out, tm=512):
    n, d = x.shape
    tm = min(tm, n)
    half = pl.BlockSpec((tm, RW_WIDTH), lambda i: (i, 0))
    return pl.pallas_call(
        _odd_out_kernel,
        out_shape=jax.ShapeDtypeStruct((n, d), F32),
        grid=(n // tm,),
        in_specs=[pl.BlockSpec((tm, d), lambda i: (i, 0)), half, half, pl.BlockSpec((d, d), lambda i: (0, 0))],
        out_specs=pl.BlockSpec((tm, d), lambda i: (i, 0)),
        compiler_params=_params(("parallel",), 32),
        name="odd_out",
    )(x, o_attn.reshape(n, RW_WIDTH), o_ssm.reshape(n, RW_WIDTH), w_out)


def _store_token_tiles(ref, x, n_tok):
    for c in range(SUBLANES):
        ref[pl.ds(c, n_tok, stride=SUBLANES), :] = x[:, c * LANES:(c + 1) * LANES]


def _load_token_tiles(ref, n_tok):
    return jnp.concatenate([ref[pl.ds(c, n_tok, stride=SUBLANES), :] for c in range(SUBLANES)], axis=1)


def _router_kernel(tm, x_ref, g_ref, whi_ref, wlo_ref, b_ref, h_o, route_o, cnt_o, run):
    @pl.when(pl.program_id(0) == 0)
    def _():
        run[...] = jnp.zeros_like(run)

    x = x_ref[...]
    ms = jnp.mean(x * x, axis=-1, keepdims=True)
    h = x * lax.rsqrt(ms + RMS_EPS) * g_ref[...]
    _store_token_tiles(h_o, h, tm)
    h_hi = h.astype(BF16)
    h_lo = (h - h_hi.astype(F32)).astype(BF16)
    logits = (jnp.dot(h_hi, whi_ref[...], preferred_element_type=F32)
              + jnp.dot(h_hi, wlo_ref[...], preferred_element_type=F32)
              + jnp.dot(h_lo, whi_ref[...], preferred_element_type=F32)) + b_ref[...]
    lane = lax.broadcasted_iota(jnp.int32, logits.shape, 1).astype(F32)
    far = float(LANES)
    is_g = lane < MOE_GROUPS
    g_max = jnp.max(jnp.where(is_g, logits, NEG_BIG), axis=-1, keepdims=True)
    g_sum = jnp.sum(jnp.where(is_g, jnp.exp(jnp.minimum(logits - g_max, 0.0)), 0.0), axis=-1, keepdims=True)
    g_top = jnp.min(jnp.where(is_g & (logits == g_max), lane, far), axis=-1, keepdims=True)
    lo = MOE_GROUPS + MOE_PER_GROUP * g_top
    in_grp = (lane >= lo) & (lane < lo + MOE_PER_GROUP)
    e1 = jnp.max(jnp.where(in_grp, logits, NEG_BIG), axis=-1, keepdims=True)
    i1 = jnp.min(jnp.where(in_grp & (logits == e1), lane, far), axis=-1, keepdims=True)
    rest = in_grp & (lane != i1)
    e2 = jnp.max(jnp.where(rest, logits, NEG_BIG), axis=-1, keepdims=True)
    i2 = jnp.min(jnp.where(rest & (logits == e2), lane, far), axis=-1, keepdims=True)
    ratio = jnp.exp(e2 - e1)
    gate1 = 1.0 / (g_sum * (1.0 + ratio))
    gate2 = gate1 * ratio
    ex1 = i1 - MOE_GROUPS
    ex2 = i2 - MOE_GROUPS
    oh1 = lane == ex1
    oh2 = lane == ex2
    before = (lax.broadcasted_iota(jnp.int32, (tm, tm), 0) > lax.broadcasted_iota(jnp.int32, (tm, tm), 1)).astype(BF16)
    pre1 = jnp.dot(before, oh1.astype(BF16), preferred_element_type=F32)
    pre2 = jnp.dot(before, oh2.astype(BF16), preferred_element_type=F32)
    tot1 = jnp.sum(oh1.astype(F32), axis=0, keepdims=True)
    tot2 = jnp.sum(oh2.astype(F32), axis=0, keepdims=True)
    base = run[...]
    rank1 = jnp.sum(jnp.where(oh1, base + pre1, 0.0), axis=-1, keepdims=True)
    rank2 = jnp.sum(jnp.where(oh2, base + tot1 + pre2, 0.0), axis=-1, keepdims=True)
    run[...] = base + tot1 + tot2
    cnt_o[...] = base + tot1 + tot2
    route_o[...] = jnp.where(lane == 0, ex1, jnp.where(lane == 1, ex2, jnp.where(lane == 2, gate1, jnp.where(
        lane == 3, gate2, jnp.where(lane == 4, rank1, jnp.where(lane == 5, rank2, 0.0))))))


def _router(x, g, w_hi, w_lo, bias, tm=512):
    n, d = x.shape
    assert d == SUBLANES * LANES
    tm = min(tm, n)
    return pl.pallas_call(
        functools.partial(_router_kernel, tm),
        out_shape=(jax.ShapeDtypeStruct((n * SUBLANES, LANES), F32), jax.ShapeDtypeStruct((n, LANES), F32),
                   jax.ShapeDtypeStruct((1, LANES), F32)),
        grid=(n // tm,),
        in_specs=[pl.BlockSpec((tm, d), lambda i: (i, 0)), pl.BlockSpec((1, d), lambda i: (0, 0)),
                  pl.BlockSpec((d, LANES), lambda i: (0, 0)), pl.BlockSpec((d, LANES), lambda i: (0, 0)),
                  pl.BlockSpec((1, LANES), lambda i: (0, 0))],
        out_specs=(pl.BlockSpec((tm * SUBLANES, LANES), lambda i: (i, 0)), pl.BlockSpec((tm, LANES), lambda i: (i, 0)),
                   pl.BlockSpec((1, LANES), lambda i: (0, 0))),
        scratch_shapes=[pltpu.VMEM((1, LANES), F32)],
        compiler_params=_params(("arbitrary",), 32),
        name="moe_router",
    )(x, g.reshape(1, d), w_hi, w_lo, bias)


DMA_UNROLL = 8


def _tile_at(ref, tok):
    return ref.at[pl.ds(pl.multiple_of(tok * SUBLANES, SUBLANES), SUBLANES)]


def _dispatch_kernel(rows, dest_ref, zblk_ref, h_ref, xs_ref, zbuf, sem, zsem):
    @pl.when(pl.program_id(0) == 0)
    def _():
        zbuf[...] = jnp.zeros_like(zbuf)

        def zero_copy(j):
            start = pl.multiple_of(jnp.maximum(zblk_ref[0, j], 0) * (MOE_ROWS * SUBLANES), MOE_ROWS * SUBLANES)
            return pltpu.make_async_copy(zbuf, xs_ref.at[pl.ds(start, MOE_ROWS * SUBLANES)], zsem.at[0])

        def z_issue(j, c):
            pl.when(zblk_ref[0, j] >= 0)(lambda: zero_copy(j).start())
            return c

        def z_drain(j, c):
            pl.when(zblk_ref[0, j] >= 0)(lambda: zero_copy(j).wait())
            return c

        lax.fori_loop(0, zblk_ref.shape[1], z_issue, 0)
        lax.fori_loop(0, zblk_ref.shape[1], z_drain, 0)

    def issue(i, c):
        for choice in range(2):
            pltpu.make_async_copy(_tile_at(h_ref, i), _tile_at(xs_ref, dest_ref[0, 0, 2 * i + choice]),
                                  sem.at[choice]).start(priority=choice)
        return c

    lax.fori_loop(0, rows, issue, 0, unroll=DMA_UNROLL)
    for choice in range(2):
        pltpu.make_async_copy(h_ref, xs_ref.at[pl.ds(0, rows * SUBLANES)], sem.at[choice]).wait()


def _dispatch(h_tiles, dest, zero_blocks, n_pad, rows=512):
    n = h_tiles.shape[0] // SUBLANES
    rows = min(rows, n)
    nb = n // rows
    return pl.pallas_call(
        functools.partial(_dispatch_kernel, rows),
        out_shape=jax.ShapeDtypeStruct((n_pad * SUBLANES, LANES), h_tiles.dtype),
        grid=(nb,),
        in_specs=[pl.BlockSpec((1, 1, 2 * rows), lambda i: (i, 0, 0), memory_space=pltpu.SMEM),
                  pl.BlockSpec((1, zero_blocks.shape[0]), lambda i: (0, 0), memory_space=pltpu.SMEM),
                  pl.BlockSpec((rows * SUBLANES, LANES), lambda i: (i, 0))],
        out_specs=pl.BlockSpec(memory_space=pl.ANY),
        scratch_shapes=[pltpu.VMEM((MOE_ROWS * SUBLANES, LANES), h_tiles.dtype),
                        pltpu.SemaphoreType.DMA((2,)), pltpu.SemaphoreType.DMA((1,))],
        compiler_params=_params(("arbitrary",), 32),
        name="moe_dispatch",
    )(dest.reshape(nb, 1, 2 * rows), zero_blocks.reshape(1, -1), h_tiles)


def _moe_mlp_kernel(be_ref, x_ref, wg_ref, wu_ref, wd_ref, o_ref, wg_s, wu_s, wd_s):
    i = pl.program_id(0)

    @pl.when((i == 0) | (be_ref[i] != be_ref[jnp.maximum(i - 1, 0)]))
    def _():
        wg_s[...] = wg_ref[0, 0].astype(BF16)
        wu_s[...] = wu_ref[0, 0].astype(BF16)
        wd_s[...] = wd_ref[0, 0].astype(BF16)

    x = _load_token_tiles(x_ref, MOE_ROWS).astype(BF16)
    hg = jnp.dot(x, wg_s[...], preferred_element_type=F32)
    hu = jnp.dot(x, wu_s[...], preferred_element_type=F32)
    hid = hg * _sigmoid(hg) * hu
    _store_token_tiles(o_ref, jnp.dot(hid.astype(BF16), wd_s[...], preferred_element_type=F32), MOE_ROWS)


def _moe_mlp(xs, block_expert, w_gate, w_up, w_down, layer):
    n_pad = xs.shape[0] // SUBLANES
    _, _, d, hid = w_gate.shape
    nb = n_pad // MOE_ROWS
    tiles = pl.BlockSpec((MOE_ROWS * SUBLANES, LANES), lambda i, be: (i, 0))
    grid_spec = pltpu.PrefetchScalarGridSpec(
        num_scalar_prefetch=1,
        grid=(nb,),
        in_specs=[tiles,
                  pl.BlockSpec((1, 1, d, hid), lambda i, be: (layer, be[i], 0, 0)),
                  pl.BlockSpec((1, 1, d, hid), lambda i, be: (layer, be[i], 0, 0)),
                  pl.BlockSpec((1, 1, hid, d), lambda i, be: (layer, be[i], 0, 0))],
        out_specs=tiles,
        scratch_shapes=[pltpu.VMEM((d, hid), BF16), pltpu.VMEM((d, hid), BF16), pltpu.VMEM((hid, d), BF16)],
    )
    return pl.pallas_call(
        _moe_mlp_kernel,
        out_shape=jax.ShapeDtypeStruct(xs.shape, F32),
        grid_spec=grid_spec,
        compiler_params=_params(("arbitrary",), 40),
        name="moe_mlp",
    )(block_expert, xs, w_gate, w_up, w_down)


def _combine_kernel(rows, pos_ref, x_ref, route_ref, ys_ref, o_ref, buf0, buf1, sem):
    bufs = (buf0, buf1)

    def issue(i, c):
        for choice in range(2):
            pltpu.make_async_copy(_tile_at(ys_ref, pos_ref[0, 0, 2 * i + choice]), _tile_at(bufs[choice], i),
                                  sem.at[choice]).start(priority=choice)
        return c

    lax.fori_loop(0, rows, issue, 0, unroll=DMA_UNROLL)
    for choice in range(2):
        pltpu.make_async_copy(ys_ref.at[pl.ds(0, rows * SUBLANES)], bufs[choice], sem.at[choice]).wait()
    route = route_ref[...]
    o_ref[...] = (x_ref[...] + route[:, 2:3] * _load_token_tiles(buf0, rows)
                  + route[:, 3:4] * _load_token_tiles(buf1, rows))


def _combine(x, ys, route, pos, rows=512):
    n, d = x.shape
    rows = min(rows, n)
    nb = n // rows
    buf = pltpu.VMEM((rows * SUBLANES, LANES), F32)
    return pl.pallas_call(
        functools.partial(_combine_kernel, rows),
        out_shape=jax.ShapeDtypeStruct((n, d), F32),
        grid=(nb,),
        in_specs=[pl.BlockSpec((1, 1, 2 * rows), lambda i: (i, 0, 0), memory_space=pltpu.SMEM),
                  pl.BlockSpec((rows, d), lambda i: (i, 0)),
                  pl.BlockSpec((rows, LANES), lambda i: (i, 0)),
                  pl.BlockSpec(memory_space=pl.ANY)],
        out_specs=pl.BlockSpec((rows, d), lambda i: (i, 0)),
        scratch_shapes=[buf, buf, pltpu.SemaphoreType.DMA((2,))],
        compiler_params=_params(("arbitrary",), 32),
        name="moe_combine",
    )(pos.reshape(nb, 1, 2 * rows), x, route, ys)


def _hier_moe(x, g, w_group, b_group, w_expert, b_expert, w_gate, w_up, w_down, layer):
    n, d = x.shape
    w_route = jnp.zeros((d, LANES), F32).at[:, :MOE_GROUPS].set(w_group)
    w_route = w_route.at[:, MOE_GROUPS:MOE_GROUPS + MOE_EXPERTS].set(w_expert)
    w_hi = w_route.astype(BF16)
    w_lo = (w_route - w_hi.astype(F32)).astype(BF16)
    bias = jnp.zeros((1, LANES), F32).at[0, :MOE_GROUPS].set(b_group)
    bias = bias.at[0, MOE_GROUPS:MOE_GROUPS + MOE_EXPERTS].set(b_expert)
    h, route, counts = _router(x, g, w_hi, w_lo, bias)

    counts = counts[0, :MOE_EXPERTS].astype(jnp.int32)
    padded = (counts + MOE_ROWS - 1) // MOE_ROWS * MOE_ROWS
    cum_padded = jnp.cumsum(padded)
    expert = route[:, 0:2].astype(jnp.int32)
    dest = ((cum_padded - padded)[expert] + route[:, 4:6].astype(jnp.int32)).reshape(-1)
    n_blocks = 2 * n // MOE_ROWS + MOE_EXPERTS
    block_start = jnp.arange(n_blocks, dtype=jnp.int32) * MOE_ROWS
    block_expert = jnp.minimum(jnp.sum((block_start[:, None] >= cum_padded[None, :]).astype(jnp.int32), axis=1),
                               MOE_EXPERTS - 1)

    last_block = cum_padded // MOE_ROWS - 1
    last_block = jnp.where((padded > 0) & (last_block < n_blocks - MOE_EXPERTS), last_block, -1)
    zero_blocks = jnp.concatenate([last_block, jnp.arange(n_blocks - MOE_EXPERTS, n_blocks, dtype=jnp.int32)])
    xs = _dispatch(h, dest, zero_blocks.astype(jnp.int32), n_blocks * MOE_ROWS)
    ys = _moe_mlp(xs, block_expert, w_gate, w_up, w_down, layer)
    return _combine(x, ys, route, dest)


def _even_layer(x, bsz, t_len, norm_g, w_in, prm, v_first):
    n = bsz * t_len
    u = _norm_matmul(x, norm_g, w_in.astype(BF16)).reshape(bsz, t_len, -1)
    r, lw, k, v, a, b, g, pool = _even_prep(u, prm, v_first)
    y = _rwkv_chunk(r, lw, k, v, a, b)
    x = _even_out(x, y, r, k, v, g, pool, prm)
    return x, v


def _odd_layer(x, bsz, t_len, layer_idx, norm_g, w_in, prm):
    n = bsz * t_len
    u = _norm_matmul(x, norm_g, w_in.astype(BF16)).reshape(bsz, t_len, -1)
    q, k = _qk_prep(u, prm["q_norm"], prm["k_norm"])
    lam_init = 0.8 - 0.6 * math.exp(-0.3 * layer_idx)
    lam = (jnp.exp(jnp.sum(prm["lam_q1"] * prm["lam_k1"])) - jnp.exp(jnp.sum(prm["lam_q2"] * prm["lam_k2"]))
           + lam_init)
    o_attn = _diff_attn(q, k, u, lam, prm["subln"], 1.0 - lam_init)
    ng = bsz // S5_BATCH_GROUP
    u5 = u[:, :, 3 * RW_WIDTH:].reshape(ng, S5_BATCH_GROUP, t_len, RW_WIDTH).transpose(0, 2, 1, 3)
    o_ssm = _s5(u5, prm).transpose(0, 2, 1, 3).reshape(bsz, t_len, RW_WIDTH)
    return _odd_out(x, o_attn, o_ssm, prm["w_out"])


def _s5_params(a_re, a_im, log_step, b_re, b_im, c_re, c_im, d_skip, w_glu):
    lam = lax.complex(jnp.minimum(a_re, -1e-4), a_im)
    lam_bar = jnp.exp(lam * jnp.exp(log_step))
    b_bar = ((lam_bar - 1.0) / lam)[..., None] * lax.complex(b_re, b_im)
    gpt = LANES // S5_GROUP_DIM
    eye = jnp.eye(gpt, dtype=F32)

    def in_map(part):
        p = part.reshape(S5_TILES, gpt, S5_STATE, S5_GROUP_DIM)
        return jnp.einsum("jgpc,gh->jgchp", p, eye).reshape(S5_TILES, LANES, gpt * S5_STATE)

    def out_map(part):
        p = part.reshape(S5_TILES, gpt, S5_GROUP_DIM, S5_STATE)
        return jnp.einsum("jgcp,gh->jgphc", p, eye).reshape(S5_TILES, gpt * S5_STATE, LANES)

    bt = jnp.concatenate([in_map(jnp.real(b_bar)), in_map(jnp.imag(b_bar))], axis=2).astype(BF16)
    ct = jnp.concatenate([out_map(c_re), -out_map(c_im)], axis=1).astype(BF16)
    return {"bt": bt, "ct": ct, "lam_re": jnp.real(lam_bar).reshape(1, -1), "lam_im": jnp.imag(lam_bar).reshape(1, -1),
            "d": d_skip, "w_glu": w_glu.astype(BF16)}


def _block_diag(blocks):
    g, r, c = blocks.shape
    return jnp.einsum("grc,gh->grhc", blocks, jnp.eye(g, dtype=blocks.dtype)).reshape(g * r, g * c)


def kernel(x, norm_mix_g, norm_ffn_g,
           even_w_in, rw_mu, rw_w0, rw_w2, rw_a0, rw_a2, rw_g2, rw_k_k, rw_k_a, rw_r_k,
           rw_ln_g, rw_ln_b, rw_v0, rw_v1, rw_v2, pool_w, pool_scale, even_w_out,
           odd_w_in, da_q_norm, da_k_norm, da_lam_q1, da_lam_k1, da_lam_q2, da_lam_k2, da_subln,
           s5_a_re, s5_a_im, s5_log_step, s5_b_re, s5_b_im, s5_c_re, s5_c_im, s5_d, s5_w_glu,
           odd_w_out,
           moe_w_group, moe_b_group, moe_w_expert, moe_b_expert, moe_w_gate, moe_w_up, moe_w_down):
    bsz, t_len, d = x.shape
    depth = norm_mix_g.shape[0]
    xf = x.reshape(bsz * t_len, d)
    v_first = None
    for layer in range(depth):
        i = layer // 2
        if layer % 2 == 0:
            rank = rw_w2.shape[1]
            wa = jnp.zeros((LANES, 2 * RW_WIDTH), F32)
            wa = wa.at[:rank, :RW_WIDTH].set(rw_w2[i]).at[rank:, RW_WIDTH:].set(rw_a2[i])
            prm = {"mu": rw_mu[i], "w0": rw_w0[i], "a0": rw_a0[i], "wa": wa.astype(BF16), "g2": rw_g2[i].astype(BF16),
                   "k_k": rw_k_k[i], "k_a": rw_k_a[i], "r_k": rw_r_k[i], "ln_g": rw_ln_g[i], "ln_b": rw_ln_b[i],
                   "pool_bd": _block_diag(pool_w[i]).astype(BF16), "pool_scale": pool_scale[i],
                   "w_out": even_w_out[i].astype(BF16)}
            if v_first is not None:
                vr = rw_v1.shape[2]
                prm["v0"] = rw_v0[i - 1]
                prm["v1"] = jnp.zeros((RW_WIDTH, LANES), F32).at[:, :vr].set(rw_v1[i - 1]).astype(BF16)
                prm["v2"] = jnp.zeros((LANES, RW_WIDTH), F32).at[:vr, :].set(rw_v2[i - 1]).astype(BF16)
            xf, v_new = _even_layer(xf, bsz, t_len, norm_mix_g[layer], even_w_in[i], prm, v_first)
            if v_first is None:
                v_first = v_new
        else:
            prm = _s5_params(s5_a_re[i], s5_a_im[i], s5_log_step[i], s5_b_re[i], s5_b_im[i], s5_c_re[i],
                             s5_c_im[i], s5_d[i].reshape(-1), s5_w_glu[i])
            prm.update({"q_norm": da_q_norm[i], "k_norm": da_k_norm[i], "lam_q1": da_lam_q1[i],
                        "lam_k1": da_lam_k1[i], "lam_q2": da_lam_q2[i], "lam_k2": da_lam_k2[i],
                        "subln": da_subln[i], "w_out": odd_w_out[i].astype(BF16)})
            xf = _odd_layer(xf, bsz, t_len, layer, norm_mix_g[layer], odd_w_in[i], prm)
        xf = _hier_moe(xf, norm_ffn_g[layer], moe_w_group[layer], moe_b_group[layer], moe_w_expert[layer],
                       moe_b_expert[layer], moe_w_gate, moe_w_up, moe_w_down, layer)
    return xf.reshape(bsz, t_len, d)
```

```python
import functools
import math

import jax
import jax.numpy as jnp
from jax import lax
from jax.experimental import pallas as pl
from jax.experimental.pallas import tpu as pltpu

F32 = jnp.float32
BF16 = jnp.bfloat16

LANES = 128
SUBLANES = 8
VMEM_BYTES_V7X = 64 * 1024 * 1024

D_MODEL = 1024
HEAD = 64
RW_WIDTH = 512
RW_SHIFT_COLS = 3 * RW_WIDTH + 64 + 64 + 128
RW_LN_EPS = 64e-5
POOL_WINDOWS = (2, 4, 8, 16)
POOL_HALO = 16
DA_HEADS = 4
SUBLN_EPS = 1e-5
ROPE_THETA = 10000.0
S5_GROUP_DIM = 16
S5_STATE = 64
S5_TILES = 4
S5_BATCH_GROUP = SUBLANES
MOE_GROUPS = 4
MOE_PER_GROUP = 8
MOE_EXPERTS = 32
MOE_ROWS = 512
RMS_EPS = 1e-6
CHUNK = 64
NEG_BIG = -1e30


def _params(semantics, vmem_mib):
    return pltpu.CompilerParams(dimension_semantics=semantics,
                                vmem_limit_bytes=min(vmem_mib * 1024 * 1024, VMEM_BYTES_V7X - 8 * 1024 * 1024))


def _bdot(a, b):
    return jnp.dot(a.astype(BF16), b.astype(BF16), preferred_element_type=F32)


def _bdot_nt(a, b):
    return lax.dot_general(a.astype(BF16), b.astype(BF16), (((1,), (1,)), ((), ())),
                           preferred_element_type=F32)


def _split_dot(x, m_bf16):
    hi = x.astype(BF16)
    lo = (x - hi.astype(F32)).astype(BF16)
    return (jnp.dot(hi, m_bf16, preferred_element_type=F32)
            + jnp.dot(lo, m_bf16, preferred_element_type=F32))


def _seg_ones(width):
    r = lax.broadcasted_iota(jnp.int32, (LANES, LANES), 0)
    c = lax.broadcasted_iota(jnp.int32, (LANES, LANES), 1)
    sh = int(math.log2(width))
    return ((r >> sh) == (c >> sh)).astype(BF16)


def _segsum(x, seg):
    tiles = [_split_dot(x[:, j * LANES:(j + 1) * LANES], seg) for j in range(x.shape[1] // LANES)]
    return tiles[0] if len(tiles) == 1 else jnp.concatenate(tiles, axis=1)


def _sigmoid(x):
    return 1.0 / (1.0 + jnp.exp(-x))


def _norm_matmul_kernel(x_ref, g_ref, w_ref, o_ref):
    x = x_ref[...]
    ms = jnp.mean(x * x, axis=-1, keepdims=True)
    h = x * lax.rsqrt(ms + RMS_EPS) * g_ref[...]
    o_ref[...] = jnp.dot(h.astype(BF16), w_ref[...], preferred_element_type=F32)


def _norm_matmul(x, g, w_bf16, tm=512):
    n, d = x.shape
    c = w_bf16.shape[1]
    tm = min(tm, n)
    return pl.pallas_call(
        _norm_matmul_kernel,
        out_shape=jax.ShapeDtypeStruct((n, c), F32),
        grid=(n // tm,),
        in_specs=[pl.BlockSpec((tm, d), lambda i: (i, 0)),
                  pl.BlockSpec((1, d), lambda i: (0, 0)),
                  pl.BlockSpec((d, c), lambda i: (0, 0))],
        out_specs=pl.BlockSpec((tm, c), lambda i: (i, 0)),
        compiler_params=_params(("parallel",), 48),
        name="norm_matmul",
    )(x, g.reshape(1, d), w_bf16)


def _even_prep_kernel(has_vres, tb, *refs):
    if has_vres:
        (u_ref, mu_ref, w0_ref, a0_ref, wa_ref, g2_ref, kk_ref, ka_ref, pw_ref, ps_ref,
         vf_ref, v0_ref, v1_ref, v2_ref,
         r_o, lw_o, k_o, v_o, a_o, b_o, g_o, pool_o, carry) = refs
    else:
        (u_ref, mu_ref, w0_ref, a0_ref, wa_ref, g2_ref, kk_ref, ka_ref, pw_ref, ps_ref,
         r_o, lw_o, k_o, v_o, a_o, b_o, g_o, pool_o, carry) = refs
    ti = pl.program_id(1)

    @pl.when(ti == 0)
    def _():
        carry[...] = jnp.zeros_like(carry)

    u = u_ref[0]
    ext = jnp.concatenate([carry[...], u], axis=0)
    carry[...] = u[tb - POOL_HALO:, :]

    p1 = ext[:, RW_SHIFT_COLS:]
    p2 = p1 + pltpu.roll(p1, 1, 0)
    p4 = p2 + pltpu.roll(p2, 2, 0)
    p8 = p4 + pltpu.roll(p4, 4, 0)
    p16 = p8 + pltpu.roll(p8, 8, 0)
    lane = lax.broadcasted_iota(jnp.int32, (1, RW_WIDTH), 1)
    grp = lane >> 7
    sums = jnp.where(grp == 0, p2, jnp.where(grp == 1, p4, jnp.where(grp == 2, p8, p16)))[POOL_HALO:]
    win = jnp.where(grp == 0, 2.0, jnp.where(grp == 1, 4.0, jnp.where(grp == 2, 8.0, 16.0)))
    n_seen = (ti * tb + lax.broadcasted_iota(jnp.int32, (tb, 1), 0) + 1).astype(F32)
    d = sums / jnp.minimum(n_seen, win) - u[:, RW_SHIFT_COLS:]
    pool_o[0] = (_bdot(d, pw_ref[...]) * ps_ref[...]).astype(pool_o.dtype)

    u_rw = u[:, :RW_SHIFT_COLS]
    prev = pltpu.roll(ext[:, :RW_SHIFT_COLS], 1, 0)[POOL_HALO:]
    m = u_rw + (prev - u_rw) * mu_ref[...]
    r = m[:, :RW_WIDTH]
    k = m[:, RW_WIDTH:2 * RW_WIDTH]
    v = m[:, 2 * RW_WIDTH:3 * RW_WIDTH]
    dwa = m[:, 3 * RW_WIDTH:3 * RW_WIDTH + LANES]
    dg = m[:, 3 * RW_WIDTH + LANES:]
    l128 = lax.broadcasted_iota(jnp.int32, (1, LANES), 1)
    dwa = jnp.where(l128 < HEAD, jnp.tanh(dwa), dwa)
    x12 = _bdot(dwa, wa_ref[...])
    z = -(w0_ref[...] + x12[:, :RW_WIDTH])
    softplus = jnp.maximum(z, 0.0) + jnp.log(1.0 + jnp.exp(-jnp.abs(z)))
    lw = -jnp.exp(-softplus - 0.5)
    a_i = _sigmoid(a0_ref[...] + x12[:, RW_WIDTH:])
    g_o[0] = _bdot(_sigmoid(dg), g2_ref[...]).astype(g_o.dtype)
    if has_vres:
        gate_v = _sigmoid(v0_ref[...] + _bdot(_bdot(v, v1_ref[...]), v2_ref[...]))
        v = v + (vf_ref[0] - v) * gate_v
    kk = k * kk_ref[...]
    ss = _segsum(kk * kk, _seg_ones(HEAD))
    kk = kk / jnp.maximum(jnp.sqrt(ss), 1e-12)
    r_o[0] = r.astype(r_o.dtype)
    lw_o[0] = lw
    k_o[0] = (k * (1.0 + (a_i - 1.0) * ka_ref[...])).astype(k_o.dtype)
    v_o[0] = v.astype(v_o.dtype)
    a_o[0] = (-kk).astype(a_o.dtype)
    b_o[0] = (kk * a_i).astype(b_o.dtype)


def _even_prep(u, prm, v_first, tb=256):
    bsz, t_len, cin = u.shape
    tb = min(tb, t_len)
    has_vres = v_first is not None
    row = lambda a: a.reshape(1, -1)
    full = lambda shape: pl.BlockSpec(shape, lambda b, t: (0,) * len(shape))
    seq = pl.BlockSpec((1, tb, RW_WIDTH), lambda b, t: (b, t, 0))
    ins = [u, row(prm["mu"]), row(prm["w0"]), row(prm["a0"]), prm["wa"], prm["g2"], row(prm["k_k"]),
           row(prm["k_a"]), prm["pool_bd"], row(prm["pool_scale"])]
    specs = [pl.BlockSpec((1, tb, cin), lambda b, t: (b, t, 0)), full((1, RW_SHIFT_COLS)),
             full((1, RW_WIDTH)), full((1, RW_WIDTH)), full((LANES, 2 * RW_WIDTH)),
             full((LANES, RW_WIDTH)), full((1, RW_WIDTH)), full((1, RW_WIDTH)),
             full((RW_WIDTH, RW_WIDTH)), full((1, RW_WIDTH))]
    if has_vres:
        ins += [v_first, row(prm["v0"]), prm["v1"], prm["v2"]]
        specs += [seq, full((1, RW_WIDTH)), full((RW_WIDTH, LANES)), full((LANES, RW_WIDTH))]
    out = lambda dt: jax.ShapeDtypeStruct((bsz, t_len, RW_WIDTH), dt)
    return pl.pallas_call(
        functools.partial(_even_prep_kernel, has_vres, tb),
        out_shape=(out(BF16), out(F32)) + (out(BF16),) * 6,
        grid=(bsz, t_len // tb),
        in_specs=specs,
        out_specs=(seq,) * 8,
        scratch_shapes=[pltpu.VMEM((POOL_HALO, cin), F32)],
        compiler_params=_params(("parallel", "arbitrary"), 48),
        name="even_prep",
    )(*ins)


def _rwkv_chunk_kernel(nb, r_ref, lw_ref, k_ref, v_ref, a_ref, b_ref, y_ref, s_ref):
    ci = pl.program_id(1)

    @pl.when(ci == 0)
    def _():
        s_ref[...] = jnp.zeros_like(s_ref)

    L = CHUNK
    n_pair = RW_WIDTH // LANES
    tri = (lax.broadcasted_iota(jnp.int32, (L, L), 0) >= lax.broadcasted_iota(jnp.int32, (L, L), 1)).astype(BF16)
    lane = lax.broadcasted_iota(jnp.int32, (1, LANES), 1)
    m_a = lane < HEAD
    t_idx = lax.broadcasted_iota(jnp.int32, (L, LANES), 0)
    s_idx = lax.broadcasted_iota(jnp.int32, (L, LANES), 1) & (HEAD - 1)
    strict = t_idx > s_idx
    incl = t_idx >= s_idx
    r128 = lax.broadcasted_iota(jnp.int32, (LANES, LANES), 0)
    c128 = lax.broadcasted_iota(jnp.int32, (LANES, LANES), 1)
    eye = (r128 == c128).astype(F32)
    same_head = (r128 >> 6) == (c128 >> 6)

    def only_a(x):
        return jnp.where(m_a, x, jnp.zeros_like(x))

    def only_b(x):
        return jnp.where(m_a, jnp.zeros_like(x), x)

    def stack(x):
        return jnp.concatenate([only_a(x), only_b(x)], axis=0)

    def stack_sw(x):
        return jnp.concatenate([only_b(x), only_a(x)], axis=0)

    def mm(x, y):
        return jnp.dot(x, y, preferred_element_type=F32)

    def mm_nt(x, y):
        return lax.dot_general(x, y, (((1,), (1,)), ((), ())), preferred_element_type=F32)

    chains = []
    for bi in range(nb):
        lw = lw_ref[bi]
        c = _split_dot_lhs(tri, lw)
        e_pos = jnp.exp(c)
        e_neg = jnp.exp(-c)
        a_all = (jnp.exp(c - lw) * a_ref[bi]).astype(BF16)
        b_all = (b_ref[bi] * e_neg).astype(BF16)
        k_all = (k_ref[bi] * e_neg).astype(BF16)
        r_all = r_ref[bi] * e_pos
        v_all = v_ref[bi].astype(BF16)
        for p in range(n_pair):
            sl = slice(p * LANES, (p + 1) * LANES)
            chains.append({"bi": bi, "p": p, "sl": sl, "at": a_all[:, sl], "bt": b_all[:, sl], "kt": k_all[:, sl],
                           "rt": r_all[:, sl], "vv": v_all[:, sl], "gl": e_pos[L - 1:L, sl]})

    for ch in chains:
        rt16 = ch["rt"].astype(BF16)
        bk = jnp.concatenate([ch["bt"], ch["kt"]], axis=0)
        kb = jnp.concatenate([ch["kt"], ch["bt"]], axis=0)
        s_a = mm_nt(jnp.concatenate([only_a(ch["at"]), only_a(rt16)], axis=0), bk)
        s_b = mm_nt(jnp.concatenate([only_b(ch["at"]), only_b(rt16)], axis=0), kb)
        m_ha = jnp.where(strict, s_a[:L], 0.0)
        n_ha = jnp.where(incl, s_a[L:], 0.0)
        m_hb = jnp.where(strict, s_b[:L], 0.0)
        n_hb = jnp.where(incl, s_b[L:], 0.0)
        ch["bk"] = bk
        ch["bdm"] = jnp.concatenate([only_a(m_ha), only_b(m_hb)], axis=0)
        ch["ak_sw"] = jnp.where(m_a, m_hb, m_ha).astype(BF16)
        ch["n_lhs"] = jnp.concatenate([jnp.where(m_a, n_ha, n_hb),
                                       jnp.where(m_a, n_hb, n_ha)], axis=1).astype(BF16)
        ch["t_inv"] = eye + ch["bdm"]
        ch["m_pow"] = ch["bdm"].astype(BF16)
    for ch in chains:
        ch["w"] = mm(ch["ak_sw"], stack_sw(ch["vv"])).astype(BF16)
    for _ in range(int(math.log2(L)) - 1):
        for ch in chains:
            ch["m_pow"] = mm(ch["m_pow"], ch["m_pow"]).astype(BF16)
        for ch in chains:
            ch["t_inv"] = ch["t_inv"] + mm(ch["t_inv"].astype(BF16), ch["m_pow"])
    for ch in chains:
        t_pk = (ch["t_inv"][:L] + ch["t_inv"][L:]).astype(BF16)
        au = mm(t_pk, jnp.concatenate([stack(ch["at"]), stack(ch["w"])], axis=1))
        ch["a_hat"] = au[:, :LANES].astype(BF16)
        ch["u_hat"] = au[:, LANES:].astype(BF16)
    for ch in chains:
        rhs = jnp.concatenate([
            jnp.concatenate([stack(ch["u_hat"]), stack(ch["a_hat"])], axis=1),
            jnp.concatenate([stack_sw(ch["vv"]), jnp.zeros((LANES, LANES), BF16)], axis=1)], axis=0)
        yr = mm(ch["n_lhs"], rhs)
        ch["y_hat"] = yr[:, :LANES]
        ch["r_hat"] = (ch["rt"] + yr[:, LANES:]).astype(BF16)
        g_p = mm(ch["a_hat"].astype(F32).T.astype(BF16), ch["bt"])
        g_q = mm(jnp.concatenate([ch["u_hat"], ch["vv"]], axis=0).astype(F32).T.astype(BF16), ch["bk"])
        ch["p_bd"] = ((eye + jnp.where(same_head, g_p, 0.0)) * ch["gl"]).astype(BF16)
        ch["q_pk"] = jnp.where(m_a, g_q[:L], g_q[L:]) * ch["gl"]
    for ch in chains:
        s0 = s_ref[ch["bi"], ch["p"]]
        s16 = s0.astype(BF16)
        y_ref[ch["bi"], :, ch["sl"]] = mm_nt(ch["r_hat"], stack(s16)) + ch["y_hat"]
        s_ref[ch["bi"], ch["p"]] = mm(s16, ch["p_bd"]) + ch["q_pk"]


def _split_dot_lhs(m_bf16, x):
    hi = x.astype(BF16)
    lo = (x - hi.astype(F32)).astype(BF16)
    return (jnp.dot(m_bf16, hi, preferred_element_type=F32)
            + jnp.dot(m_bf16, lo, preferred_element_type=F32))


def _rwkv_chunk(r, lw, k, v, a, b, nb=4):
    bsz, t_len, _ = r.shape
    seq = pl.BlockSpec((nb, CHUNK, RW_WIDTH), lambda bi, ci: (bi, ci, 0))
    return pl.pallas_call(
        functools.partial(_rwkv_chunk_kernel, nb),
        out_shape=jax.ShapeDtypeStruct((bsz, t_len, RW_WIDTH), F32),
        grid=(bsz // nb, t_len // CHUNK),
        in_specs=[seq] * 6,
        out_specs=seq,
        scratch_shapes=[pltpu.VMEM((nb, RW_WIDTH // LANES, HEAD, LANES), F32)],
        compiler_params=_params(("parallel", "arbitrary"), 32),
        name="rwkv_chunk",
    )(r, lw, k, v, a, b)


def _even_out_kernel(tm, x_ref, y_ref, r_ref, k_ref, v_ref, g_ref, p_ref, lng_ref, lnb_ref, rk_ref, wo_ref,
                     gf_ref, whi_ref, wlo_ref, rb_ref, o_ref, h_o, route_o, cnt_o, run):
    seg = _seg_ones(HEAD)
    y = y_ref[...]
    mean = _segsum(y, seg) * (1.0 / HEAD)
    yc = y - mean
    var = _segsum(yc * yc, seg) * (1.0 / HEAD)
    yn = yc * lax.rsqrt(var + RW_LN_EPS) * lng_ref[...] + lnb_ref[...]
    bonus = _segsum(r_ref[...].astype(F32) * k_ref[...].astype(F32) * rk_ref[...], seg) * v_ref[...].astype(F32)
    o_rw = (yn + bonus) * g_ref[...].astype(F32)
    cat = jnp.concatenate([o_rw.astype(BF16), p_ref[...]], axis=1)
    x_new = x_ref[...] + jnp.dot(cat, wo_ref[...], preferred_element_type=F32)
    o_ref[...] = x_new
    _route_block(tm, x_new, gf_ref, whi_ref, wlo_ref, rb_ref, h_o, route_o, cnt_o, run)


def _even_out(x, y, r, k, v, g, pool, prm, router, tm=512):
    n, d = x.shape
    tm = min(tm, n)
    half = pl.BlockSpec((tm, RW_WIDTH), lambda i: (i, 0))
    vec = pl.BlockSpec((1, RW_WIDTH), lambda i: (0, 0))
    fl = lambda a: a.reshape(n, RW_WIDTH)
    r_in, r_shape, r_out, r_scratch = _router_specs(tm, d)
    return pl.pallas_call(
        functools.partial(_even_out_kernel, tm),
        out_shape=(jax.ShapeDtypeStruct((n, d), F32),) + r_shape(n),
        grid=(n // tm,),
        in_specs=[pl.BlockSpec((tm, d), lambda i: (i, 0))] + [half] * 6 + [vec] * 3
                 + [pl.BlockSpec((d, d), lambda i: (0, 0))] + r_in,
        out_specs=(pl.BlockSpec((tm, d), lambda i: (i, 0)),) + r_out,
        scratch_shapes=[r_scratch],
        compiler_params=_params(("arbitrary",), 48),
        name="even_out",
    )(x, fl(y), fl(r), fl(k), fl(v), fl(g), fl(pool), prm["ln_g"].reshape(1, -1), prm["ln_b"].reshape(1, -1),
      prm["r_k"].reshape(1, -1), prm["w_out"], *router)


def _qk_prep_kernel(u_ref, gq_ref, gk_ref, cos_ref, sin_ref, q_o, k_o):
    x = u_ref[0]
    cos = cos_ref[...]
    sin = sin_ref[...]
    seg = _seg_ones(HEAD)
    lane = lax.broadcasted_iota(jnp.int32, (1, LANES), 1)
    first = (lane & (HEAD - 1)) < HEAD // 2
    nq = RW_WIDTH // LANES
    for j in range(2 * nq):
        xt = x[:, j * LANES:(j + 1) * LANES]
        ms = _split_dot(xt * xt, seg) * (1.0 / HEAD)
        gain = gq_ref[...] if j < nq else gk_ref[...]
        xn = xt * lax.rsqrt(ms + RMS_EPS) * gain
        partner = jnp.where(first, pltpu.roll(xn, LANES - HEAD // 2, 1), pltpu.roll(xn, HEAD // 2, 1))
        out = xn * cos + partner * sin
        if j < nq:
            q_o[0, :, j * LANES:(j + 1) * LANES] = (out * (HEAD ** -0.5)).astype(BF16)
        else:
            k_o[0, :, (j - nq) * LANES:(j - nq + 1) * LANES] = out.astype(BF16)


def _qk_prep(u, q_norm, k_norm, tb=256):
    bsz, t_len, _ = u.shape
    tb = min(tb, t_len)
    inv_freq = ROPE_THETA ** (-jnp.arange(0, HEAD, 2, dtype=F32) / HEAD)
    ang = jnp.arange(t_len, dtype=F32)[:, None] * inv_freq[None, :]
    cos = jnp.tile(jnp.cos(ang), (1, LANES // (HEAD // 2)))
    sin_half = jnp.concatenate([-jnp.sin(ang), jnp.sin(ang)], axis=1)
    sin = jnp.tile(sin_half, (1, LANES // HEAD))
    tile2 = lambda g: jnp.tile(g, LANES // HEAD).reshape(1, LANES)
    out = jax.ShapeDtypeStruct((bsz, t_len, RW_WIDTH), BF16)
    tab = pl.BlockSpec((tb, LANES), lambda b, t: (t, 0))
    vec = pl.BlockSpec((1, LANES), lambda b, t: (0, 0))
    seq = pl.BlockSpec((1, tb, RW_WIDTH), lambda b, t: (b, t, 0))
    return pl.pallas_call(
        _qk_prep_kernel,
        out_shape=(out, out),
        grid=(bsz, t_len // tb),
        in_specs=[pl.BlockSpec((1, tb, 2 * RW_WIDTH), lambda b, t: (b, t, 0)), vec, vec, tab, tab],
        out_specs=(seq, seq),
        compiler_params=_params(("parallel", "parallel"), 32),
        name="qk_prep",
    )(u, tile2(q_norm), tile2(k_norm), cos, sin)


def _diff_attn_kernel(tq, nh, out_scale, q_ref, k_ref, v_ref, lam_ref, sub_ref, o_ref):
    qi = pl.program_id(2)
    lane = lax.broadcasted_iota(jnp.int32, (1, LANES), 1)
    m_a = lane < HEAD

    def prefix(n_blk):
        kl = n_blk * tq
        on_diag = lax.broadcasted_iota(jnp.int32, (tq, tq), 0) >= lax.broadcasted_iota(jnp.int32, (tq, tq), 1)
        qs, ks, vs = [], [], []
        for h in range(nh):
            sl = slice(h * LANES, (h + 1) * LANES)
            q = q_ref[0, :, sl]
            zero = jnp.zeros_like(q)
            qs += [jnp.where(m_a, q, zero), jnp.where(m_a, zero, q)]
            ks += [k_ref[0, :kl, sl]] * 2
            vs += [v_ref[0, :kl, sl].astype(BF16)] * 2
        s = [lax.dot_general(q, k, (((1,), (1,)), ((), ())), preferred_element_type=F32) for q, k in zip(qs, ks)]
        last = [jnp.where(on_diag, x[:, kl - tq:], NEG_BIG) for x in s]
        if n_blk > 1:
            last = [jnp.concatenate([x[:, :kl - tq], y], axis=1) for x, y in zip(s, last)]
        top = [jnp.max(x, axis=-1, keepdims=True) for x in last]
        e = [jnp.exp(x - m) for x, m in zip(last, top)]
        den = [jnp.sum(x, axis=-1, keepdims=True) for x in e]
        pv = [jnp.dot(x.astype(BF16), v, preferred_element_type=F32) for x, v in zip(e, vs)]
        for h in range(nh):
            sl = slice(h * LANES, (h + 1) * LANES)
            o = pv[2 * h] / den[2 * h] - lam_ref[...] * (pv[2 * h + 1] / den[2 * h + 1])
            ms = jnp.mean(o * o, axis=-1, keepdims=True)
            o_ref[0, :, sl] = o * lax.rsqrt(ms + SUBLN_EPS) * sub_ref[...] * out_scale

    for blk in range(k_ref.shape[1] // tq):
        pl.when(qi == blk)(functools.partial(prefix, blk + 1))


def _diff_attn(q, k, u, lam, subln, out_scale, tq=256, nh=2):
    bsz, t_len, _ = q.shape
    tq = min(tq, t_len)
    wide = nh * LANES
    v_off = 2 * RW_WIDTH // wide
    return pl.pallas_call(
        functools.partial(_diff_attn_kernel, tq, nh, out_scale),
        out_shape=jax.ShapeDtypeStruct((bsz, t_len, RW_WIDTH), F32),
        grid=(bsz, DA_HEADS // nh, t_len // tq),
        in_specs=[pl.BlockSpec((1, tq, wide), lambda b, h, i: (b, i, h)),
                  pl.BlockSpec((1, t_len, wide), lambda b, h, i: (b, 0, h)),
                  pl.BlockSpec((1, t_len, wide), lambda b, h, i: (b, 0, v_off + h)),
                  pl.BlockSpec((1, 1), lambda b, h, i: (0, 0)),
                  pl.BlockSpec((1, LANES), lambda b, h, i: (0, 0))],
        out_specs=pl.BlockSpec((1, tq, wide), lambda b, h, i: (b, i, h)),
        compiler_params=_params(("parallel", "parallel", "arbitrary"), 48),
        name="diff_attn",
    )(q, k, u, lam.reshape(1, 1), subln.reshape(1, LANES))


def _s5_kernel(tb, u_ref, bt_ref, lre_ref, lim_ref, ct_ref, d_ref, wg_ref, o_ref, xs, st):
    ti = pl.program_id(1)

    @pl.when(ti == 0)
    def _():
        st[...] = jnp.zeros_like(st)

    rows = tb * SUBLANES
    half = RW_WIDTH
    u = u_ref[0].reshape(rows, RW_WIDTH)
    for j in range(S5_TILES):
        xs[:, 2 * half * j:2 * half * (j + 1)] = _bdot(u[:, j * LANES:(j + 1) * LANES], bt_ref[j])
    for j in range(S5_TILES):
        re = slice(2 * half * j, 2 * half * j + half)
        im = slice(2 * half * j + half, 2 * half * (j + 1))
        lr = jnp.broadcast_to(lre_ref[:, half * j:half * (j + 1)], (SUBLANES, half))
        li = jnp.broadcast_to(lim_ref[:, half * j:half * (j + 1)], (SUBLANES, half))

        def step(t, carry, re=re, im=im, lr=lr, li=li):
            xr, xi = carry
            row = pl.multiple_of(t * SUBLANES, SUBLANES)
            nr = lr * xr - li * xi + xs[pl.ds(row, SUBLANES), re]
            ni = lr * xi + li * xr + xs[pl.ds(row, SUBLANES), im]
            xs[pl.ds(row, SUBLANES), re] = nr
            xs[pl.ds(row, SUBLANES), im] = ni
            return nr, ni

        xr, xi = lax.fori_loop(0, tb, step, (st[:, re], st[:, im]), unroll=4)
        st[:, re] = xr
        st[:, im] = xi
    y = jnp.concatenate([_bdot(xs[:, 2 * half * j:2 * half * (j + 1)], ct_ref[j]) for j in range(S5_TILES)], axis=1)
    y = y + d_ref[...] * u
    z = 0.5 * y * (1.0 + jnp.tanh(math.sqrt(2.0 / math.pi) * (y + 0.044715 * (y * y * y))))
    o = z * _sigmoid(_bdot(z, wg_ref[...]))
    o_ref[0] = o.reshape(tb, SUBLANES, RW_WIDTH)


def _s5(u5, prm, tb=128):
    ng, t_len, _, _ = u5.shape
    tb = min(tb, t_len)
    full = lambda shape: pl.BlockSpec(shape, lambda g, t: (0,) * len(shape))
    n_state = 2 * RW_WIDTH * S5_TILES
    return pl.pallas_call(
        functools.partial(_s5_kernel, tb),
        out_shape=jax.ShapeDtypeStruct(u5.shape, F32),
        grid=(ng, t_len // tb),
        in_specs=[pl.BlockSpec((1, tb, SUBLANES, RW_WIDTH), lambda g, t: (g, t, 0, 0)),
                  full((S5_TILES, LANES, 2 * RW_WIDTH)), full((1, n_state // 2)), full((1, n_state // 2)),
                  full((S5_TILES, 2 * RW_WIDTH, LANES)), full((1, RW_WIDTH)), full((RW_WIDTH, RW_WIDTH))],
        out_specs=pl.BlockSpec((1, tb, SUBLANES, RW_WIDTH), lambda g, t: (g, t, 0, 0)),
        scratch_shapes=[pltpu.VMEM((tb * SUBLANES, n_state), F32), pltpu.VMEM((SUBLANES, n_state), F32)],
        compiler_params=_params(("parallel", "arbitrary"), 48),
        name="s5_scan",
    )(u5, prm["bt"], prm["lam_re"], prm["lam_im"], prm["ct"], prm["d"].reshape(1, -1), prm["w_glu"])


def _odd_out_kernel(tm, x_ref, a_ref, s_ref, wo_ref, gf_ref, whi_ref, wlo_ref, rb_ref, o_ref, h_o, route_o, cnt_o, run):
    cat = jnp.concatenate([a_ref[...], s_ref[...]], axis=1)
    x_new = x_ref[...] + _bdot(cat, wo_ref[...])
    o_ref[...] = x_new
    _route_block(tm, x_new, gf_ref, whi_ref, wlo_ref, rb_ref, h_o, route_o, cnt_o, run)


def _odd_out(x, o_attn, o_ssm, w_out, router, tm=512):
    n, d = x.shape
    tm = min(tm, n)
    half = pl.BlockSpec((tm, RW_WIDTH), lambda i: (i, 0))
    r_in, r_shape, r_out, r_scratch = _router_specs(tm, d)
    return pl.pallas_call(
        functools.partial(_odd_out_kernel, tm),
        out_shape=(jax.ShapeDtypeStruct((n, d), F32),) + r_shape(n),
        grid=(n // tm,),
        in_specs=[pl.BlockSpec((tm, d), lambda i: (i, 0)), half, half, pl.BlockSpec((d, d), lambda i: (0, 0))] + r_in,
        out_specs=(pl.BlockSpec((tm, d), lambda i: (i, 0)),) + r_out,
        scratch_shapes=[r_scratch],
        compiler_params=_params(("arbitrary",), 40),
        name="odd_out",
    )(x, o_attn.reshape(n, RW_WIDTH), o_ssm.reshape(n, RW_WIDTH), w_out, *router)


def _store_token_tiles(ref, x, n_tok):
    for c in range(SUBLANES):
        ref[pl.ds(c, n_tok, stride=SUBLANES), :] = x[:, c * LANES:(c + 1) * LANES]


def _load_token_tiles(ref, n_tok):
    return jnp.concatenate([ref[pl.ds(c, n_tok, stride=SUBLANES), :] for c in range(SUBLANES)], axis=1)


def _route_block(tm, x, g_ref, whi_ref, wlo_ref, b_ref, h_o, route_o, cnt_o, run):
    @pl.when(pl.program_id(0) == 0)
    def _():
        run[...] = jnp.zeros_like(run)

    ms = jnp.mean(x * x, axis=-1, keepdims=True)
    h = x * lax.rsqrt(ms + RMS_EPS) * g_ref[...]
    _store_token_tiles(h_o, h, tm)
    h_hi = h.astype(BF16)
    h_lo = (h - h_hi.astype(F32)).astype(BF16)
    logits = (jnp.dot(h_hi, whi_ref[...], preferred_element_type=F32)
              + jnp.dot(h_hi, wlo_ref[...], preferred_element_type=F32)
              + jnp.dot(h_lo, whi_ref[...], preferred_element_type=F32)) + b_ref[...]
    lane = lax.broadcasted_iota(jnp.int32, logits.shape, 1).astype(F32)
    far = float(LANES)
    is_g = lane < MOE_GROUPS
    g_max = jnp.max(jnp.where(is_g, logits, NEG_BIG), axis=-1, keepdims=True)
    g_sum = jnp.sum(jnp.where(is_g, jnp.exp(jnp.minimum(logits - g_max, 0.0)), 0.0), axis=-1, keepdims=True)
    g_top = jnp.min(jnp.where(is_g & (logits == g_max), lane, far), axis=-1, keepdims=True)
    lo = MOE_GROUPS + MOE_PER_GROUP * g_top
    in_grp = (lane >= lo) & (lane < lo + MOE_PER_GROUP)
    e1 = jnp.max(jnp.where(in_grp, logits, NEG_BIG), axis=-1, keepdims=True)
    i1 = jnp.min(jnp.where(in_grp & (logits == e1), lane, far), axis=-1, keepdims=True)
    rest = in_grp & (lane != i1)
    e2 = jnp.max(jnp.where(rest, logits, NEG_BIG), axis=-1, keepdims=True)
    i2 = jnp.min(jnp.where(rest & (logits == e2), lane, far), axis=-1, keepdims=True)
    ratio = jnp.exp(e2 - e1)
    gate1 = 1.0 / (g_sum * (1.0 + ratio))
    gate2 = gate1 * ratio
    ex1 = i1 - MOE_GROUPS
    ex2 = i2 - MOE_GROUPS
    oh1 = lane == ex1
    oh2 = lane == ex2
    before = (lax.broadcasted_iota(jnp.int32, (tm, tm), 0) > lax.broadcasted_iota(jnp.int32, (tm, tm), 1)).astype(BF16)
    pre1 = jnp.dot(before, oh1.astype(BF16), preferred_element_type=F32)
    pre2 = jnp.dot(before, oh2.astype(BF16), preferred_element_type=F32)
    tot1 = jnp.sum(oh1.astype(F32), axis=0, keepdims=True)
    tot2 = jnp.sum(oh2.astype(F32), axis=0, keepdims=True)
    base = run[...]
    rank1 = jnp.sum(jnp.where(oh1, base + pre1, 0.0), axis=-1, keepdims=True)
    rank2 = jnp.sum(jnp.where(oh2, base + tot1 + pre2, 0.0), axis=-1, keepdims=True)
    run[...] = base + tot1 + tot2
    cnt_o[...] = base + tot1 + tot2
    route_o[...] = jnp.where(lane == 0, ex1, jnp.where(lane == 1, ex2, jnp.where(lane == 2, gate1, jnp.where(
        lane == 3, gate2, jnp.where(lane == 4, rank1, jnp.where(lane == 5, rank2, 0.0))))))


def _router_operands(g, w_group, b_group, w_expert, b_expert):
    d = g.shape[0]
    assert d == SUBLANES * LANES
    w_route = jnp.zeros((d, LANES), F32).at[:, :MOE_GROUPS].set(w_group)
    w_route = w_route.at[:, MOE_GROUPS:MOE_GROUPS + MOE_EXPERTS].set(w_expert)
    w_hi = w_route.astype(BF16)
    w_lo = (w_route - w_hi.astype(F32)).astype(BF16)
    bias = jnp.zeros((1, LANES), F32).at[0, :MOE_GROUPS].set(b_group)
    bias = bias.at[0, MOE_GROUPS:MOE_GROUPS + MOE_EXPERTS].set(b_expert)
    return g.reshape(1, d), w_hi, w_lo, bias


def _router_specs(tm, d):
    const = lambda shape: pl.BlockSpec(shape, lambda i: (0,) * len(shape))
    in_specs = [const((1, d)), const((d, LANES)), const((d, LANES)), const((1, LANES))]
    out_specs = (pl.BlockSpec((tm * SUBLANES, LANES), lambda i: (i, 0)), pl.BlockSpec((tm, LANES), lambda i: (i, 0)),
                 const((1, LANES)))
    out_shape = lambda n: (jax.ShapeDtypeStruct((n * SUBLANES, LANES), F32), jax.ShapeDtypeStruct((n, LANES), F32),
                           jax.ShapeDtypeStruct((1, LANES), F32))
    return in_specs, out_shape, out_specs, pltpu.VMEM((1, LANES), F32)


DMA_UNROLL = 8


def _tile_at(ref, tok):
    return ref.at[pl.ds(pl.multiple_of(tok * SUBLANES, SUBLANES), SUBLANES)]


def _dispatch_kernel(rows, dest_ref, zblk_ref, h_ref, xs_ref, zbuf, sem, zsem):
    @pl.when(pl.program_id(0) == 0)
    def _():
        zbuf[...] = jnp.zeros_like(zbuf)

        def zero_copy(j):
            start = pl.multiple_of(jnp.maximum(zblk_ref[0, j], 0) * (MOE_ROWS * SUBLANES), MOE_ROWS * SUBLANES)
            return pltpu.make_async_copy(zbuf, xs_ref.at[pl.ds(start, MOE_ROWS * SUBLANES)], zsem.at[0])

        def z_issue(j, c):
            pl.when(zblk_ref[0, j] >= 0)(lambda: zero_copy(j).start())
            return c

        def z_drain(j, c):
            pl.when(zblk_ref[0, j] >= 0)(lambda: zero_copy(j).wait())
            return c

        lax.fori_loop(0, zblk_ref.shape[1], z_issue, 0)
        lax.fori_loop(0, zblk_ref.shape[1], z_drain, 0)

    def issue(i, c):
        for choice in range(2):
            pltpu.make_async_copy(_tile_at(h_ref, i), _tile_at(xs_ref, dest_ref[0, 0, 2 * i + choice]),
                                  sem.at[choice]).start(priority=choice)
        return c

    lax.fori_loop(0, rows, issue, 0, unroll=DMA_UNROLL)
    for choice in range(2):
        pltpu.make_async_copy(h_ref, xs_ref.at[pl.ds(0, rows * SUBLANES)], sem.at[choice]).wait()


def _dispatch(h_tiles, dest, zero_blocks, n_pad, rows=512):
    n = h_tiles.shape[0] // SUBLANES
    rows = min(rows, n)
    nb = n // rows
    return pl.pallas_call(
        functools.partial(_dispatch_kernel, rows),
        out_shape=jax.ShapeDtypeStruct((n_pad * SUBLANES, LANES), h_tiles.dtype),
        grid=(nb,),
        in_specs=[pl.BlockSpec((1, 1, 2 * rows), lambda i: (i, 0, 0), memory_space=pltpu.SMEM),
                  pl.BlockSpec((1, zero_blocks.shape[0]), lambda i: (0, 0), memory_space=pltpu.SMEM),
                  pl.BlockSpec((rows * SUBLANES, LANES), lambda i: (i, 0))],
        out_specs=pl.BlockSpec(memory_space=pl.ANY),
        scratch_shapes=[pltpu.VMEM((MOE_ROWS * SUBLANES, LANES), h_tiles.dtype),
                        pltpu.SemaphoreType.DMA((2,)), pltpu.SemaphoreType.DMA((1,))],
        compiler_params=_params(("arbitrary",), 32),
        name="moe_dispatch",
    )(dest.reshape(nb, 1, 2 * rows), zero_blocks.reshape(1, -1), h_tiles)


def _moe_mlp_kernel(be_ref, used_ref, x_ref, wg_ref, wu_ref, wd_ref, o_ref, wg_s, wu_s, wd_s):
    i = pl.program_id(0)

    @pl.when(i < used_ref[0])
    def _():
        @pl.when((i == 0) | (be_ref[i] != be_ref[jnp.maximum(i - 1, 0)]))
        def _():
            wg_s[...] = wg_ref[0, 0].astype(BF16)
            wu_s[...] = wu_ref[0, 0].astype(BF16)
            wd_s[...] = wd_ref[0, 0].astype(BF16)

        x = _load_token_tiles(x_ref, MOE_ROWS).astype(BF16)
        hg = jnp.dot(x, wg_s[...], preferred_element_type=F32)
        hu = jnp.dot(x, wu_s[...], preferred_element_type=F32)
        hid = hg * _sigmoid(hg) * hu
        _store_token_tiles(o_ref, jnp.dot(hid.astype(BF16), wd_s[...], preferred_element_type=F32), MOE_ROWS)

    @pl.when(i >= used_ref[0])
    def _():
        o_ref[...] = jnp.zeros_like(o_ref)


def _moe_mlp(xs, block_expert, n_used, w_gate, w_up, w_down, layer):
    n_pad = xs.shape[0] // SUBLANES
    _, _, d, hid = w_gate.shape
    nb = n_pad // MOE_ROWS
    live = lambda i, used: jnp.minimum(i, used[0] - 1)
    tiles = pl.BlockSpec((MOE_ROWS * SUBLANES, LANES), lambda i, be, used: (live(i, used), 0))
    weight = lambda shape: pl.BlockSpec(shape, lambda i, be, used: (layer, be[live(i, used)], 0, 0))
    grid_spec = pltpu.PrefetchScalarGridSpec(
        num_scalar_prefetch=2,
        grid=(nb,),
        in_specs=[tiles, weight((1, 1, d, hid)), weight((1, 1, d, hid)), weight((1, 1, hid, d))],
        out_specs=pl.BlockSpec((MOE_ROWS * SUBLANES, LANES), lambda i, be, used: (i, 0)),
        scratch_shapes=[pltpu.VMEM((d, hid), BF16), pltpu.VMEM((d, hid), BF16), pltpu.VMEM((hid, d), BF16)],
    )
    return pl.pallas_call(
        _moe_mlp_kernel,
        out_shape=jax.ShapeDtypeStruct(xs.shape, F32),
        grid_spec=grid_spec,
        compiler_params=_params(("arbitrary",), 48),
        name="moe_mlp",
    )(block_expert, n_used, xs, w_gate, w_up, w_down)


def _combine_kernel(rows, pos_ref, nxt_ref, x_ref, route_ref, ys_ref, o_ref, bufs, sem):
    step = pl.program_id(0)

    def gather(idx_ref, parity):
        def issue(i, c):
            for choice in range(2):
                pltpu.make_async_copy(_tile_at(ys_ref, idx_ref[0, 0, 2 * i + choice]),
                                      _tile_at(bufs.at[parity, choice], i),
                                      sem.at[parity, choice]).start(priority=choice)
            return c

        lax.fori_loop(0, rows, issue, 0, unroll=DMA_UNROLL)

    pl.when(step == 0)(lambda: gather(pos_ref, 0))
    pl.when(step + 1 < pl.num_programs(0))(lambda: gather(nxt_ref, (step + 1) % 2))
    parity = step % 2
    for choice in range(2):
        pltpu.make_async_copy(ys_ref.at[pl.ds(0, rows * SUBLANES)], bufs.at[parity, choice],
                              sem.at[parity, choice]).wait()
    route = route_ref[...]
    o_ref[...] = (x_ref[...] + route[:, 2:3] * _load_token_tiles(bufs.at[parity, 0], rows)
                  + route[:, 3:4] * _load_token_tiles(bufs.at[parity, 1], rows))


def _combine(x, ys, route, pos, rows=512):
    n, d = x.shape
    rows = min(rows, n)
    nb = n // rows
    idx = lambda f: pl.BlockSpec((1, 1, 2 * rows), lambda i: (f(i), 0, 0), memory_space=pltpu.SMEM)
    pos = pos.reshape(nb, 1, 2 * rows)
    return pl.pallas_call(
        functools.partial(_combine_kernel, rows),
        out_shape=jax.ShapeDtypeStruct((n, d), F32),
        grid=(nb,),
        in_specs=[idx(lambda i: i), idx(lambda i: jnp.minimum(i + 1, nb - 1)),
                  pl.BlockSpec((rows, d), lambda i: (i, 0)),
                  pl.BlockSpec((rows, LANES), lambda i: (i, 0)),
                  pl.BlockSpec(memory_space=pl.ANY)],
        out_specs=pl.BlockSpec((rows, d), lambda i: (i, 0)),
        scratch_shapes=[pltpu.VMEM((2, 2, rows * SUBLANES, LANES), F32), pltpu.SemaphoreType.DMA((2, 2))],
        compiler_params=_params(("arbitrary",), 40),
        name="moe_combine",
    )(pos, pos, x, route, ys)


def _hier_moe(x, h, route, counts, w_gate, w_up, w_down, layer):
    n, _ = x.shape
    counts = counts[0, :MOE_EXPERTS].astype(jnp.int32)
    padded = (counts + MOE_ROWS - 1) // MOE_ROWS * MOE_ROWS
    cum_padded = jnp.cumsum(padded)
    expert = route[:, 0:2].astype(jnp.int32)
    dest = ((cum_padded - padded)[expert] + route[:, 4:6].astype(jnp.int32)).reshape(-1)
    n_blocks = 2 * n // MOE_ROWS + MOE_EXPERTS
    block_start = jnp.arange(n_blocks, dtype=jnp.int32) * MOE_ROWS
    block_expert = jnp.minimum(jnp.sum((block_start[:, None] >= cum_padded[None, :]).astype(jnp.int32), axis=1),
                               MOE_EXPERTS - 1)
    n_used = (cum_padded[-1:] // MOE_ROWS).astype(jnp.int32)

    spare = jnp.arange(n_blocks - MOE_EXPERTS, n_blocks, dtype=jnp.int32)
    zero_blocks = jnp.concatenate([jnp.where(padded > 0, cum_padded // MOE_ROWS - 1, -1).astype(jnp.int32),
                                   jnp.where(spare >= n_used[0], spare, -1)])
    xs = _dispatch(h, dest, zero_blocks, n_blocks * MOE_ROWS)
    ys = _moe_mlp(xs, block_expert, n_used, w_gate, w_up, w_down, layer)
    return _combine(x, ys, route, dest)


def _even_layer(x, bsz, t_len, norm_g, w_in, prm, v_first, router):
    n = bsz * t_len
    u = _norm_matmul(x, norm_g, w_in.astype(BF16)).reshape(bsz, t_len, -1)
    r, lw, k, v, a, b, g, pool = _even_prep(u, prm, v_first)
    y = _rwkv_chunk(r, lw, k, v, a, b)
    return _even_out(x, y, r, k, v, g, pool, prm, router), v


def _odd_layer(x, bsz, t_len, layer_idx, norm_g, w_in, prm, router):
    n = bsz * t_len
    u = _norm_matmul(x, norm_g, w_in.astype(BF16)).reshape(bsz, t_len, -1)
    q, k = _qk_prep(u, prm["q_norm"], prm["k_norm"])
    lam_init = 0.8 - 0.6 * math.exp(-0.3 * layer_idx)
    lam = (jnp.exp(jnp.sum(prm["lam_q1"] * prm["lam_k1"])) - jnp.exp(jnp.sum(prm["lam_q2"] * prm["lam_k2"]))
           + lam_init)
    o_attn = _diff_attn(q, k, u, lam, prm["subln"], 1.0 - lam_init)
    ng = bsz // S5_BATCH_GROUP
    u5 = u[:, :, 3 * RW_WIDTH:].reshape(ng, S5_BATCH_GROUP, t_len, RW_WIDTH).transpose(0, 2, 1, 3)
    o_ssm = _s5(u5, prm).transpose(0, 2, 1, 3).reshape(bsz, t_len, RW_WIDTH)
    return _odd_out(x, o_attn, o_ssm, prm["w_out"], router)


def _s5_params(a_re, a_im, log_step, b_re, b_im, c_re, c_im, d_skip, w_glu):
    lam = lax.complex(jnp.minimum(a_re, -1e-4), a_im)
    lam_bar = jnp.exp(lam * jnp.exp(log_step))
    b_bar = ((lam_bar - 1.0) / lam)[..., None] * lax.complex(b_re, b_im)
    gpt = LANES // S5_GROUP_DIM
    eye = jnp.eye(gpt, dtype=F32)

    def in_map(part):
        p = part.reshape(S5_TILES, gpt, S5_STATE, S5_GROUP_DIM)
        return jnp.einsum("jgpc,gh->jgchp", p, eye).reshape(S5_TILES, LANES, gpt * S5_STATE)

    def out_map(part):
        p = part.reshape(S5_TILES, gpt, S5_GROUP_DIM, S5_STATE)
        return jnp.einsum("jgcp,gh->jgphc", p, eye).reshape(S5_TILES, gpt * S5_STATE, LANES)

    bt = jnp.concatenate([in_map(jnp.real(b_bar)), in_map(jnp.imag(b_bar))], axis=2).astype(BF16)
    ct = jnp.concatenate([out_map(c_re), -out_map(c_im)], axis=1).astype(BF16)
    return {"bt": bt, "ct": ct, "lam_re": jnp.real(lam_bar).reshape(1, -1), "lam_im": jnp.imag(lam_bar).reshape(1, -1),
            "d": d_skip, "w_glu": w_glu.astype(BF16)}


def _block_diag(blocks):
    g, r, c = blocks.shape
    return jnp.einsum("grc,gh->grhc", blocks, jnp.eye(g, dtype=blocks.dtype)).reshape(g * r, g * c)


def kernel(x, norm_mix_g, norm_ffn_g,
           even_w_in, rw_mu, rw_w0, rw_w2, rw_a0, rw_a2, rw_g2, rw_k_k, rw_k_a, rw_r_k,
           rw_ln_g, rw_ln_b, rw_v0, rw_v1, rw_v2, pool_w, pool_scale, even_w_out,
           odd_w_in, da_q_norm, da_k_norm, da_lam_q1, da_lam_k1, da_lam_q2, da_lam_k2, da_subln,
           s5_a_re, s5_a_im, s5_log_step, s5_b_re, s5_b_im, s5_c_re, s5_c_im, s5_d, s5_w_glu,
           odd_w_out,
           moe_w_group, moe_b_group, moe_w_expert, moe_b_expert, moe_w_gate, moe_w_up, moe_w_down):
    bsz, t_len, d = x.shape
    depth = norm_mix_g.shape[0]
    xf = x.reshape(bsz * t_len, d)
    v_first = None
    for layer in range(depth):
        i = layer // 2
        router = _router_operands(norm_ffn_g[layer], moe_w_group[layer], moe_b_group[layer], moe_w_expert[layer],
                                  moe_b_expert[layer])
        if layer % 2 == 0:
            rank = rw_w2.shape[1]
            wa = jnp.zeros((LANES, 2 * RW_WIDTH), F32)
            wa = wa.at[:rank, :RW_WIDTH].set(rw_w2[i]).at[rank:, RW_WIDTH:].set(rw_a2[i])
            prm = {"mu": rw_mu[i], "w0": rw_w0[i], "a0": rw_a0[i], "wa": wa.astype(BF16), "g2": rw_g2[i].astype(BF16),
                   "k_k": rw_k_k[i], "k_a": rw_k_a[i], "r_k": rw_r_k[i], "ln_g": rw_ln_g[i], "ln_b": rw_ln_b[i],
                   "pool_bd": _block_diag(pool_w[i]).astype(BF16), "pool_scale": pool_scale[i],
                   "w_out": even_w_out[i].astype(BF16)}
            if v_first is not None:
                vr = rw_v1.shape[2]
                prm["v0"] = rw_v0[i - 1]
                prm["v1"] = jnp.zeros((RW_WIDTH, LANES), F32).at[:, :vr].set(rw_v1[i - 1]).astype(BF16)
                prm["v2"] = jnp.zeros((LANES, RW_WIDTH), F32).at[:vr, :].set(rw_v2[i - 1]).astype(BF16)
            routed, v_new = _even_layer(xf, bsz, t_len, norm_mix_g[layer], even_w_in[i], prm, v_first, router)
            if v_first is None:
                v_first = v_new
        else:
            prm = _s5_params(s5_a_re[i], s5_a_im[i], s5_log_step[i], s5_b_re[i], s5_b_im[i], s5_c_re[i],
                             s5_c_im[i], s5_d[i].reshape(-1), s5_w_glu[i])
            prm.update({"q_norm": da_q_norm[i], "k_norm": da_k_norm[i], "lam_q1": da_lam_q1[i],
                        "lam_k1": da_lam_k1[i], "lam_q2": da_lam_q2[i], "lam_k2": da_lam_k2[i],
                        "subln": da_subln[i], "w_out": odd_w_out[i].astype(BF16)})
            routed = _odd_layer(xf, bsz, t_len, layer, norm_mix_g[layer], odd_w_in[i], prm, router)
        xf = _hier_moe(*routed, moe_w_gate, moe_w_up, moe_w_down, layer)
    return xf.reshape(bsz, t_len, d)
```

```python
import functools
import math

import jax
import jax.numpy as jnp
from jax import lax
from jax.experimental import pallas as pl
from jax.experimental.pallas import tpu as pltpu

F32 = jnp.float32
BF16 = jnp.bfloat16

LANES = 128
SUBLANES = 8
VMEM_BYTES_V7X = 64 * 1024 * 1024

D_MODEL = 1024
HEAD = 64
RW_WIDTH = 512
RW_SHIFT_COLS = 3 * RW_WIDTH + 64 + 64 + 128
RW_LN_EPS = 64e-5
POOL_WINDOWS = (2, 4, 8, 16)
POOL_HALO = 16
DA_HEADS = 4
SUBLN_EPS = 1e-5
ROPE_THETA = 10000.0
S5_GROUP_DIM = 16
S5_STATE = 64
S5_TILES = 4
S5_BATCH_GROUP = SUBLANES
MOE_GROUPS = 4
MOE_PER_GROUP = 8
MOE_EXPERTS = 32
MOE_ROWS = 512
RMS_EPS = 1e-6
CHUNK = 64
NEG_BIG = -1e30


def _params(semantics, vmem_mib):
    return pltpu.CompilerParams(dimension_semantics=semantics,
                                vmem_limit_bytes=min(vmem_mib * 1024 * 1024, VMEM_BYTES_V7X - 8 * 1024 * 1024))


def _bdot(a, b):
    return jnp.dot(a.astype(BF16), b.astype(BF16), preferred_element_type=F32)


def _bdot_nt(a, b):
    return lax.dot_general(a.astype(BF16), b.astype(BF16), (((1,), (1,)), ((), ())),
                           preferred_element_type=F32)


def _split_dot(x, m_bf16):
    hi = x.astype(BF16)
    lo = (x - hi.astype(F32)).astype(BF16)
    return (jnp.dot(hi, m_bf16, preferred_element_type=F32)
            + jnp.dot(lo, m_bf16, preferred_element_type=F32))


def _seg_ones(width):
    r = lax.broadcasted_iota(jnp.int32, (LANES, LANES), 0)
    c = lax.broadcasted_iota(jnp.int32, (LANES, LANES), 1)
    sh = int(math.log2(width))
    return ((r >> sh) == (c >> sh)).astype(BF16)


def _segsum(x, seg):
    tiles = [_split_dot(x[:, j * LANES:(j + 1) * LANES], seg) for j in range(x.shape[1] // LANES)]
    return tiles[0] if len(tiles) == 1 else jnp.concatenate(tiles, axis=1)


def _sigmoid(x):
    return 1.0 / (1.0 + jnp.exp(-x))


def _even_prep_kernel(has_vres, tb, *refs):
    if has_vres:
        (x_ref, gn_ref, win_ref, mu_ref, w0_ref, a0_ref, wa_ref, g2_ref, kk_ref, ka_ref, pw_ref, ps_ref,
         vf_ref, v0_ref, v1_ref, v2_ref,
         r_o, lw_o, k_o, v_o, a_o, b_o, g_o, pool_o, carry) = refs
    else:
        (x_ref, gn_ref, win_ref, mu_ref, w0_ref, a0_ref, wa_ref, g2_ref, kk_ref, ka_ref, pw_ref, ps_ref,
         r_o, lw_o, k_o, v_o, a_o, b_o, g_o, pool_o, carry) = refs
    ti = pl.program_id(1)

    @pl.when(ti == 0)
    def _():
        carry[...] = jnp.zeros_like(carry)

    x = x_ref[0]
    h = x * lax.rsqrt(jnp.mean(x * x, axis=-1, keepdims=True) + RMS_EPS) * gn_ref[...]
    u = jnp.dot(h.astype(BF16), win_ref[...], preferred_element_type=F32)
    ext = jnp.concatenate([carry[...], u], axis=0)
    carry[...] = u[tb - POOL_HALO:, :]

    p1 = ext[:, RW_SHIFT_COLS:]
    p2 = p1 + pltpu.roll(p1, 1, 0)
    p4 = p2 + pltpu.roll(p2, 2, 0)
    p8 = p4 + pltpu.roll(p4, 4, 0)
    p16 = p8 + pltpu.roll(p8, 8, 0)
    lane = lax.broadcasted_iota(jnp.int32, (1, RW_WIDTH), 1)
    grp = lane >> 7
    sums = jnp.where(grp == 0, p2, jnp.where(grp == 1, p4, jnp.where(grp == 2, p8, p16)))[POOL_HALO:]
    win = jnp.where(grp == 0, 2.0, jnp.where(grp == 1, 4.0, jnp.where(grp == 2, 8.0, 16.0)))
    n_seen = (ti * tb + lax.broadcasted_iota(jnp.int32, (tb, 1), 0) + 1).astype(F32)
    d = sums / jnp.minimum(n_seen, win) - u[:, RW_SHIFT_COLS:]
    pool_o[0] = (_bdot(d, pw_ref[...]) * ps_ref[...]).astype(pool_o.dtype)

    u_rw = u[:, :RW_SHIFT_COLS]
    prev = pltpu.roll(ext[:, :RW_SHIFT_COLS], 1, 0)[POOL_HALO:]
    m = u_rw + (prev - u_rw) * mu_ref[...]
    r = m[:, :RW_WIDTH]
    k = m[:, RW_WIDTH:2 * RW_WIDTH]
    v = m[:, 2 * RW_WIDTH:3 * RW_WIDTH]
    dwa = m[:, 3 * RW_WIDTH:3 * RW_WIDTH + LANES]
    dg = m[:, 3 * RW_WIDTH + LANES:]
    l128 = lax.broadcasted_iota(jnp.int32, (1, LANES), 1)
    dwa = jnp.where(l128 < HEAD, jnp.tanh(dwa), dwa)
    x12 = _bdot(dwa, wa_ref[...])
    z = -(w0_ref[...] + x12[:, :RW_WIDTH])
    softplus = jnp.maximum(z, 0.0) + jnp.log(1.0 + jnp.exp(-jnp.abs(z)))
    lw = -jnp.exp(-softplus - 0.5)
    a_i = _sigmoid(a0_ref[...] + x12[:, RW_WIDTH:])
    g_o[0] = _bdot(_sigmoid(dg), g2_ref[...]).astype(g_o.dtype)
    if has_vres:
        gate_v = _sigmoid(v0_ref[...] + _bdot(_bdot(v, v1_ref[...]), v2_ref[...]))
        v = v + (vf_ref[0] - v) * gate_v
    kk = k * kk_ref[...]
    ss = _segsum(kk * kk, _seg_ones(HEAD))
    kk = kk / jnp.maximum(jnp.sqrt(ss), 1e-12)
    r_o[0] = r.astype(r_o.dtype)
    lw_o[0] = lw
    k_o[0] = (k * (1.0 + (a_i - 1.0) * ka_ref[...])).astype(k_o.dtype)
    v_o[0] = v.astype(v_o.dtype)
    a_o[0] = (-kk).astype(a_o.dtype)
    b_o[0] = (kk * a_i).astype(b_o.dtype)


def _even_prep(x, norm_g, w_in, prm, v_first, tb=512):
    bsz, t_len, d = x.shape
    cin = w_in.shape[1]
    tb = min(tb, t_len)
    has_vres = v_first is not None
    row = lambda a: a.reshape(1, -1)
    full = lambda shape: pl.BlockSpec(shape, lambda b, t: (0,) * len(shape))
    seq = pl.BlockSpec((1, tb, RW_WIDTH), lambda b, t: (b, t, 0))
    ins = [x, row(norm_g), w_in, row(prm["mu"]), row(prm["w0"]), row(prm["a0"]), prm["wa"], prm["g2"], row(prm["k_k"]),
           row(prm["k_a"]), prm["pool_bd"], row(prm["pool_scale"])]
    specs = [pl.BlockSpec((1, tb, d), lambda b, t: (b, t, 0)), full((1, d)), full((d, cin)), full((1, RW_SHIFT_COLS)),
             full((1, RW_WIDTH)), full((1, RW_WIDTH)), full((LANES, 2 * RW_WIDTH)),
             full((LANES, RW_WIDTH)), full((1, RW_WIDTH)), full((1, RW_WIDTH)),
             full((RW_WIDTH, RW_WIDTH)), full((1, RW_WIDTH))]
    if has_vres:
        ins += [v_first, row(prm["v0"]), prm["v1"], prm["v2"]]
        specs += [seq, full((1, RW_WIDTH)), full((RW_WIDTH, LANES)), full((LANES, RW_WIDTH))]
    out = lambda dt: jax.ShapeDtypeStruct((bsz, t_len, RW_WIDTH), dt)
    return pl.pallas_call(
        functools.partial(_even_prep_kernel, has_vres, tb),
        out_shape=(out(BF16), out(F32)) + (out(BF16),) * 6,
        grid=(bsz, t_len // tb),
        in_specs=specs,
        out_specs=(seq,) * 8,
        scratch_shapes=[pltpu.VMEM((POOL_HALO, cin), F32)],
        compiler_params=_params(("parallel", "arbitrary"), 56),
        name="even_prep",
    )(*ins)


def _rwkv_chunk_kernel(nb, r_ref, lw_ref, k_ref, v_ref, a_ref, b_ref, y_ref, s_ref):
    ci = pl.program_id(1)

    @pl.when(ci == 0)
    def _():
        s_ref[...] = jnp.zeros_like(s_ref)

    L = CHUNK
    n_pair = RW_WIDTH // LANES
    tri = (lax.broadcasted_iota(jnp.int32, (L, L), 0) >= lax.broadcasted_iota(jnp.int32, (L, L), 1)).astype(BF16)
    lane = lax.broadcasted_iota(jnp.int32, (1, LANES), 1)
    m_a = lane < HEAD
    t_idx = lax.broadcasted_iota(jnp.int32, (L, LANES), 0)
    s_idx = lax.broadcasted_iota(jnp.int32, (L, LANES), 1) & (HEAD - 1)
    strict = t_idx > s_idx
    incl = t_idx >= s_idx
    r128 = lax.broadcasted_iota(jnp.int32, (LANES, LANES), 0)
    c128 = lax.broadcasted_iota(jnp.int32, (LANES, LANES), 1)
    eye = (r128 == c128).astype(F32)
    same_head = (r128 >> 6) == (c128 >> 6)

    def only_a(x):
        return jnp.where(m_a, x, jnp.zeros_like(x))

    def only_b(x):
        return jnp.where(m_a, jnp.zeros_like(x), x)

    def stack(x):
        return jnp.concatenate([only_a(x), only_b(x)], axis=0)

    def stack_sw(x):
        return jnp.concatenate([only_b(x), only_a(x)], axis=0)

    def mm(x, y):
        return jnp.dot(x, y, preferred_element_type=F32)

    def mm_nt(x, y):
        return lax.dot_general(x, y, (((1,), (1,)), ((), ())), preferred_element_type=F32)

    chains = []
    for bi in range(nb):
        lw = lw_ref[bi]
        c = _split_dot_lhs(tri, lw)
        e_pos = jnp.exp(c)
        e_neg = jnp.exp(-c)
        a_all = (jnp.exp(c - lw) * a_ref[bi]).astype(BF16)
        b_all = (b_ref[bi] * e_neg).astype(BF16)
        k_all = (k_ref[bi] * e_neg).astype(BF16)
        r_all = r_ref[bi] * e_pos
        v_all = v_ref[bi].astype(BF16)
        for p in range(n_pair):
            sl = slice(p * LANES, (p + 1) * LANES)
            chains.append({"bi": bi, "p": p, "sl": sl, "at": a_all[:, sl], "bt": b_all[:, sl], "kt": k_all[:, sl],
                           "rt": r_all[:, sl], "vv": v_all[:, sl], "gl": e_pos[L - 1:L, sl]})

    for ch in chains:
        rt16 = ch["rt"].astype(BF16)
        bk = jnp.concatenate([ch["bt"], ch["kt"]], axis=0)
        kb = jnp.concatenate([ch["kt"], ch["bt"]], axis=0)
        s_a = mm_nt(jnp.concatenate([only_a(ch["at"]), only_a(rt16)], axis=0), bk)
        s_b = mm_nt(jnp.concatenate([only_b(ch["at"]), only_b(rt16)], axis=0), kb)
        m_ha = jnp.where(strict, s_a[:L], 0.0)
        n_ha = jnp.where(incl, s_a[L:], 0.0)
        m_hb = jnp.where(strict, s_b[:L], 0.0)
        n_hb = jnp.where(incl, s_b[L:], 0.0)
        ch["bk"] = bk
        ch["bdm"] = jnp.concatenate([only_a(m_ha), only_b(m_hb)], axis=0)
        ch["ak_sw"] = jnp.where(m_a, m_hb, m_ha).astype(BF16)
        ch["n_lhs"] = jnp.concatenate([jnp.where(m_a, n_ha, n_hb),
                                       jnp.where(m_a, n_hb, n_ha)], axis=1).astype(BF16)
        ch["t_inv"] = eye + ch["bdm"]
        ch["m_pow"] = ch["bdm"].astype(BF16)
    for ch in chains:
        ch["w"] = mm(ch["ak_sw"], stack_sw(ch["vv"])).astype(BF16)
    for _ in range(int(math.log2(L)) - 1):
        for ch in chains:
            ch["m_pow"] = mm(ch["m_pow"], ch["m_pow"]).astype(BF16)
        for ch in chains:
            ch["t_inv"] = ch["t_inv"] + mm(ch["t_inv"].astype(BF16), ch["m_pow"])
    for ch in chains:
        t_pk = (ch["t_inv"][:L] + ch["t_inv"][L:]).astype(BF16)
        au = mm(t_pk, jnp.concatenate([stack(ch["at"]), stack(ch["w"])], axis=1))
        ch["a_hat"] = au[:, :LANES].astype(BF16)
        ch["u_hat"] = au[:, LANES:].astype(BF16)
    for ch in chains:
        rhs = jnp.concatenate([
            jnp.concatenate([stack(ch["u_hat"]), stack(ch["a_hat"])], axis=1),
            jnp.concatenate([stack_sw(ch["vv"]), jnp.zeros((LANES, LANES), BF16)], axis=1)], axis=0)
        yr = mm(ch["n_lhs"], rhs)
        ch["y_hat"] = yr[:, :LANES]
        ch["r_hat"] = (ch["rt"] + yr[:, LANES:]).astype(BF16)
        g_p = mm(ch["a_hat"].astype(F32).T.astype(BF16), ch["bt"])
        g_q = mm(jnp.concatenate([ch["u_hat"], ch["vv"]], axis=0).astype(F32).T.astype(BF16), ch["bk"])
        ch["p_bd"] = ((eye + jnp.where(same_head, g_p, 0.0)) * ch["gl"]).astype(BF16)
        ch["q_pk"] = jnp.where(m_a, g_q[:L], g_q[L:]) * ch["gl"]
    for ch in chains:
        s0 = s_ref[ch["bi"], ch["p"]]
        s16 = s0.astype(BF16)
        y_ref[ch["bi"], :, ch["sl"]] = mm_nt(ch["r_hat"], stack(s16)) + ch["y_hat"]
        s_ref[ch["bi"], ch["p"]] = mm(s16, ch["p_bd"]) + ch["q_pk"]


def _split_dot_lhs(m_bf16, x):
    hi = x.astype(BF16)
    lo = (x - hi.astype(F32)).astype(BF16)
    return (jnp.dot(m_bf16, hi, preferred_element_type=F32)
            + jnp.dot(m_bf16, lo, preferred_element_type=F32))


def _rwkv_chunk(r, lw, k, v, a, b, nb=4):
    bsz, t_len, _ = r.shape
    seq = pl.BlockSpec((nb, CHUNK, RW_WIDTH), lambda bi, ci: (bi, ci, 0))
    return pl.pallas_call(
        functools.partial(_rwkv_chunk_kernel, nb),
        out_shape=jax.ShapeDtypeStruct((bsz, t_len, RW_WIDTH), F32),
        grid=(bsz // nb, t_len // CHUNK),
        in_specs=[seq] * 6,
        out_specs=seq,
        scratch_shapes=[pltpu.VMEM((nb, RW_WIDTH // LANES, HEAD, LANES), F32)],
        compiler_params=_params(("parallel", "arbitrary"), 32),
        name="rwkv_chunk",
    )(r, lw, k, v, a, b)


def _even_out_kernel(tm, x_ref, y_ref, r_ref, k_ref, v_ref, g_ref, p_ref, lng_ref, lnb_ref, rk_ref, wo_ref,
                     gf_ref, whi_ref, wlo_ref, rb_ref, o_ref, h_o, route_o, cnt_o, run):
    seg = _seg_ones(HEAD)
    y = y_ref[...]
    mean = _segsum(y, seg) * (1.0 / HEAD)
    yc = y - mean
    var = _segsum(yc * yc, seg) * (1.0 / HEAD)
    yn = yc * lax.rsqrt(var + RW_LN_EPS) * lng_ref[...] + lnb_ref[...]
    bonus = _segsum(r_ref[...].astype(F32) * k_ref[...].astype(F32) * rk_ref[...], seg) * v_ref[...].astype(F32)
    o_rw = (yn + bonus) * g_ref[...].astype(F32)
    cat = jnp.concatenate([o_rw.astype(BF16), p_ref[...]], axis=1)
    x_new = x_ref[...] + jnp.dot(cat, wo_ref[...], preferred_element_type=F32)
    o_ref[...] = x_new
    _route_block(tm, x_new, gf_ref, whi_ref, wlo_ref, rb_ref, h_o, route_o, cnt_o, run)


def _even_out(x, y, r, k, v, g, pool, prm, router, tm=512):
    n, d = x.shape
    tm = min(tm, n)
    half = pl.BlockSpec((tm, RW_WIDTH), lambda i: (i, 0))
    vec = pl.BlockSpec((1, RW_WIDTH), lambda i: (0, 0))
    fl = lambda a: a.reshape(n, RW_WIDTH)
    r_in, r_shape, r_out, r_scratch = _router_specs(tm, d)
    return pl.pallas_call(
        functools.partial(_even_out_kernel, tm),
        out_shape=(jax.ShapeDtypeStruct((n, d), F32),) + r_shape(n),
        grid=(n // tm,),
        in_specs=[pl.BlockSpec((tm, d), lambda i: (i, 0))] + [half] * 6 + [vec] * 3
                 + [pl.BlockSpec((d, d), lambda i: (0, 0))] + r_in,
        out_specs=(pl.BlockSpec((tm, d), lambda i: (i, 0)),) + r_out,
        scratch_shapes=[r_scratch],
        compiler_params=_params(("arbitrary",), 48),
        name="even_out",
    )(x, fl(y), fl(r), fl(k), fl(v), fl(g), fl(pool), prm["ln_g"].reshape(1, -1), prm["ln_b"].reshape(1, -1),
      prm["r_k"].reshape(1, -1), prm["w_out"], *router)


def _qk_prep_kernel(x_ref, gn_ref, win_ref, gq_ref, gk_ref, cos_ref, sin_ref, q_o, k_o, rest_o):
    xin = x_ref[0]
    h = xin * lax.rsqrt(jnp.mean(xin * xin, axis=-1, keepdims=True) + RMS_EPS) * gn_ref[...]
    u = jnp.dot(h.astype(BF16), win_ref[...], preferred_element_type=F32)
    rest_o[0] = u[:, 2 * RW_WIDTH:]
    x = u[:, :2 * RW_WIDTH]
    cos = cos_ref[...]
    sin = sin_ref[...]
    seg = _seg_ones(HEAD)
    lane = lax.broadcasted_iota(jnp.int32, (1, LANES), 1)
    first = (lane & (HEAD - 1)) < HEAD // 2
    nq = RW_WIDTH // LANES
    for j in range(2 * nq):
        xt = x[:, j * LANES:(j + 1) * LANES]
        ms = _split_dot(xt * xt, seg) * (1.0 / HEAD)
        gain = gq_ref[...] if j < nq else gk_ref[...]
        xn = xt * lax.rsqrt(ms + RMS_EPS) * gain
        partner = jnp.where(first, pltpu.roll(xn, LANES - HEAD // 2, 1), pltpu.roll(xn, HEAD // 2, 1))
        out = xn * cos + partner * sin
        if j < nq:
            q_o[0, :, j * LANES:(j + 1) * LANES] = (out * (HEAD ** -0.5)).astype(BF16)
        else:
            k_o[0, :, (j - nq) * LANES:(j - nq + 1) * LANES] = out.astype(BF16)


def _qk_prep(x, norm_g, w_in, q_norm, k_norm, tb=512):
    bsz, t_len, d = x.shape
    cin = w_in.shape[1]
    tb = min(tb, t_len)
    inv_freq = ROPE_THETA ** (-jnp.arange(0, HEAD, 2, dtype=F32) / HEAD)
    ang = jnp.arange(t_len, dtype=F32)[:, None] * inv_freq[None, :]
    cos = jnp.tile(jnp.cos(ang), (1, LANES // (HEAD // 2)))
    sin_half = jnp.concatenate([-jnp.sin(ang), jnp.sin(ang)], axis=1)
    sin = jnp.tile(sin_half, (1, LANES // HEAD))
    tile2 = lambda g: jnp.tile(g, LANES // HEAD).reshape(1, LANES)
    out = jax.ShapeDtypeStruct((bsz, t_len, RW_WIDTH), BF16)
    tab = pl.BlockSpec((tb, LANES), lambda b, t: (t, 0))
    vec = pl.BlockSpec((1, LANES), lambda b, t: (0, 0))
    seq = pl.BlockSpec((1, tb, RW_WIDTH), lambda b, t: (b, t, 0))
    const = lambda shape: pl.BlockSpec(shape, lambda b, t: (0,) * len(shape))
    return pl.pallas_call(
        _qk_prep_kernel,
        out_shape=(out, out, jax.ShapeDtypeStruct((bsz, t_len, cin - 2 * RW_WIDTH), F32)),
        grid=(bsz, t_len // tb),
        in_specs=[pl.BlockSpec((1, tb, d), lambda b, t: (b, t, 0)), const((1, d)), const((d, cin)), vec, vec, tab, tab],
        out_specs=(seq, seq, pl.BlockSpec((1, tb, cin - 2 * RW_WIDTH), lambda b, t: (b, t, 0))),
        compiler_params=_params(("parallel", "parallel"), 48),
        name="qk_prep",
    )(x, norm_g.reshape(1, d), w_in, tile2(q_norm), tile2(k_norm), cos, sin)


def _diff_attn_kernel(tq, nh, out_scale, q_ref, k_ref, v_ref, lam_ref, sub_ref, o_ref):
    qi = pl.program_id(2)
    lane = lax.broadcasted_iota(jnp.int32, (1, LANES), 1)
    m_a = lane < HEAD

    def prefix(n_blk):
        kl = n_blk * tq
        on_diag = lax.broadcasted_iota(jnp.int32, (tq, tq), 0) >= lax.broadcasted_iota(jnp.int32, (tq, tq), 1)
        qs, ks, vs = [], [], []
        for h in range(nh):
            sl = slice(h * LANES, (h + 1) * LANES)
            q = q_ref[0, :, sl]
            zero = jnp.zeros_like(q)
            qs += [jnp.where(m_a, q, zero), jnp.where(m_a, zero, q)]
            ks += [k_ref[0, :kl, sl]] * 2
            vs += [jnp.concatenate([v_ref[0, :kl, sl].astype(BF16), jnp.ones((kl, LANES), BF16)], axis=1)] * 2
        s = [lax.dot_general(q, k, (((1,), (1,)), ((), ())), preferred_element_type=F32) for q, k in zip(qs, ks)]
        last = [jnp.where(on_diag, x[:, kl - tq:], NEG_BIG) for x in s]
        if n_blk > 1:
            last = [jnp.concatenate([x[:, :kl - tq], y], axis=1) for x, y in zip(s, last)]
        top = [jnp.max(x, axis=-1, keepdims=True) for x in last]
        e = [jnp.exp((x - m).astype(BF16)) for x, m in zip(last, top)]
        pv = [jnp.dot(x, v, preferred_element_type=F32) for x, v in zip(e, vs)]
        pv = [x[:, :LANES] / x[:, LANES:] for x in pv]
        for h in range(nh):
            sl = slice(h * LANES, (h + 1) * LANES)
            o = pv[2 * h] - lam_ref[...] * pv[2 * h + 1]
            ms = jnp.mean(o * o, axis=-1, keepdims=True)
            o_ref[0, :, sl] = o * lax.rsqrt(ms + SUBLN_EPS) * sub_ref[...] * out_scale

    for blk in range(k_ref.shape[1] // tq):
        pl.when(qi == blk)(functools.partial(prefix, blk + 1))


def _diff_attn(q, k, v_src, lam, subln, out_scale, tq=256, nh=2):
    bsz, t_len, _ = q.shape
    tq = min(tq, t_len)
    wide = nh * LANES
    return pl.pallas_call(
        functools.partial(_diff_attn_kernel, tq, nh, out_scale),
        out_shape=jax.ShapeDtypeStruct((bsz, t_len, RW_WIDTH), F32),
        grid=(bsz, DA_HEADS // nh, t_len // tq),
        in_specs=[pl.BlockSpec((1, tq, wide), lambda b, h, i: (b, i, h)),
                  pl.BlockSpec((1, t_len, wide), lambda b, h, i: (b, 0, h)),
                  pl.BlockSpec((1, t_len, wide), lambda b, h, i: (b, 0, h)),
                  pl.BlockSpec((1, 1), lambda b, h, i: (0, 0)),
                  pl.BlockSpec((1, LANES), lambda b, h, i: (0, 0))],
        out_specs=pl.BlockSpec((1, tq, wide), lambda b, h, i: (b, i, h)),
        compiler_params=_params(("parallel", "parallel", "arbitrary"), 48),
        name="diff_attn",
    )(q, k, v_src, lam.reshape(1, 1), subln.reshape(1, LANES))


def _s5_kernel(tb, u_ref, bt_ref, lre_ref, lim_ref, ct_ref, d_ref, wg_ref, o_ref, xs, st):
    ti = pl.program_id(1)

    @pl.when(ti == 0)
    def _():
        st[...] = jnp.zeros_like(st)

    rows = tb * SUBLANES
    half = RW_WIDTH
    u = u_ref[0].reshape(rows, RW_WIDTH)
    for j in range(S5_TILES):
        xs[:, 2 * half * j:2 * half * (j + 1)] = _bdot(u[:, j * LANES:(j + 1) * LANES], bt_ref[j])
    for j in range(S5_TILES):
        re = slice(2 * half * j, 2 * half * j + half)
        im = slice(2 * half * j + half, 2 * half * (j + 1))
        lr = jnp.broadcast_to(lre_ref[:, half * j:half * (j + 1)], (SUBLANES, half))
        li = jnp.broadcast_to(lim_ref[:, half * j:half * (j + 1)], (SUBLANES, half))

        def step(t, carry, re=re, im=im, lr=lr, li=li):
            xr, xi = carry
            row = pl.multiple_of(t * SUBLANES, SUBLANES)
            nr = lr * xr - li * xi + xs[pl.ds(row, SUBLANES), re]
            ni = lr * xi + li * xr + xs[pl.ds(row, SUBLANES), im]
            xs[pl.ds(row, SUBLANES), re] = nr
            xs[pl.ds(row, SUBLANES), im] = ni
            return nr, ni

        xr, xi = lax.fori_loop(0, tb, step, (st[:, re], st[:, im]), unroll=4)
        st[:, re] = xr
        st[:, im] = xi
    y = jnp.concatenate([_bdot(xs[:, 2 * half * j:2 * half * (j + 1)], ct_ref[j]) for j in range(S5_TILES)], axis=1)
    y = y + d_ref[...] * u
    z = 0.5 * y * (1.0 + jnp.tanh(math.sqrt(2.0 / math.pi) * (y + 0.044715 * (y * y * y))))
    o = z * _sigmoid(_bdot(z, wg_ref[...]))
    o_ref[0] = o.reshape(tb, SUBLANES, RW_WIDTH)


def _s5(u5, prm, tb=128):
    ng, t_len, _, _ = u5.shape
    tb = min(tb, t_len)
    full = lambda shape: pl.BlockSpec(shape, lambda g, t: (0,) * len(shape))
    n_state = 2 * RW_WIDTH * S5_TILES
    return pl.pallas_call(
        functools.partial(_s5_kernel, tb),
        out_shape=jax.ShapeDtypeStruct(u5.shape, F32),
        grid=(ng, t_len // tb),
        in_specs=[pl.BlockSpec((1, tb, SUBLANES, RW_WIDTH), lambda g, t: (g, t, 0, 0)),
                  full((S5_TILES, LANES, 2 * RW_WIDTH)), full((1, n_state // 2)), full((1, n_state // 2)),
                  full((S5_TILES, 2 * RW_WIDTH, LANES)), full((1, RW_WIDTH)), full((RW_WIDTH, RW_WIDTH))],
        out_specs=pl.BlockSpec((1, tb, SUBLANES, RW_WIDTH), lambda g, t: (g, t, 0, 0)),
        scratch_shapes=[pltpu.VMEM((tb * SUBLANES, n_state), F32), pltpu.VMEM((SUBLANES, n_state), F32)],
        compiler_params=_params(("parallel", "arbitrary"), 48),
        name="s5_scan",
    )(u5, prm["bt"], prm["lam_re"], prm["lam_im"], prm["ct"], prm["d"].reshape(1, -1), prm["w_glu"])


def _odd_out_kernel(tm, x_ref, a_ref, s_ref, wo_ref, gf_ref, whi_ref, wlo_ref, rb_ref, o_ref, h_o, route_o, cnt_o, run):
    cat = jnp.concatenate([a_ref[...], s_ref[...]], axis=1)
    x_new = x_ref[...] + _bdot(cat, wo_ref[...])
    o_ref[...] = x_new
    _route_block(tm, x_new, gf_ref, whi_ref, wlo_ref, rb_ref, h_o, route_o, cnt_o, run)


def _odd_out(x, o_attn, o_ssm, w_out, router, tm=512):
    n, d = x.shape
    tm = min(tm, n)
    half = pl.BlockSpec((tm, RW_WIDTH), lambda i: (i, 0))
    r_in, r_shape, r_out, r_scratch = _router_specs(tm, d)
    return pl.pallas_call(
        functools.partial(_odd_out_kernel, tm),
        out_shape=(jax.ShapeDtypeStruct((n, d), F32),) + r_shape(n),
        grid=(n // tm,),
        in_specs=[pl.BlockSpec((tm, d), lambda i: (i, 0)), half, half, pl.BlockSpec((d, d), lambda i: (0, 0))] + r_in,
        out_specs=(pl.BlockSpec((tm, d), lambda i: (i, 0)),) + r_out,
        scratch_shapes=[r_scratch],
        compiler_params=_params(("arbitrary",), 40),
        name="odd_out",
    )(x, o_attn.reshape(n, RW_WIDTH), o_ssm.reshape(n, RW_WIDTH), w_out, *router)


def _store_token_tiles(ref, x, n_tok):
    for c in range(SUBLANES):
        ref[pl.ds(c, n_tok, stride=SUBLANES), :] = x[:, c * LANES:(c + 1) * LANES]


def _load_token_tiles(ref, n_tok):
    return jnp.concatenate([ref[pl.ds(c, n_tok, stride=SUBLANES), :] for c in range(SUBLANES)], axis=1)


def _route_block(tm, x, g_ref, whi_ref, wlo_ref, b_ref, h_o, route_o, cnt_o, run):
    @pl.when(pl.program_id(0) == 0)
    def _():
        run[...] = jnp.zeros_like(run)

    ms = jnp.mean(x * x, axis=-1, keepdims=True)
    h = x * lax.rsqrt(ms + RMS_EPS) * g_ref[...]
    _store_token_tiles(h_o, h, tm)
    h_hi = h.astype(BF16)
    h_lo = (h - h_hi.astype(F32)).astype(BF16)
    logits = (jnp.dot(h_hi, whi_ref[...], preferred_element_type=F32)
              + jnp.dot(h_hi, wlo_ref[...], preferred_element_type=F32)
              + jnp.dot(h_lo, whi_ref[...], preferred_element_type=F32)) + b_ref[...]
    lane = lax.broadcasted_iota(jnp.int32, logits.shape, 1).astype(F32)
    far = float(LANES)
    is_g = lane < MOE_GROUPS
    g_max = jnp.max(jnp.where(is_g, logits, NEG_BIG), axis=-1, keepdims=True)
    g_sum = jnp.sum(jnp.where(is_g, jnp.exp(jnp.minimum(logits - g_max, 0.0)), 0.0), axis=-1, keepdims=True)
    g_top = jnp.min(jnp.where(is_g & (logits == g_max), lane, far), axis=-1, keepdims=True)
    lo = MOE_GROUPS + MOE_PER_GROUP * g_top
    in_grp = (lane >= lo) & (lane < lo + MOE_PER_GROUP)
    e1 = jnp.max(jnp.where(in_grp, logits, NEG_BIG), axis=-1, keepdims=True)
    i1 = jnp.min(jnp.where(in_grp & (logits == e1), lane, far), axis=-1, keepdims=True)
    rest = in_grp & (lane != i1)
    e2 = jnp.max(jnp.where(rest, logits, NEG_BIG), axis=-1, keepdims=True)
    i2 = jnp.min(jnp.where(rest & (logits == e2), lane, far), axis=-1, keepdims=True)
    ratio = jnp.exp(e2 - e1)
    gate1 = 1.0 / (g_sum * (1.0 + ratio))
    gate2 = gate1 * ratio
    ex1 = i1 - MOE_GROUPS
    ex2 = i2 - MOE_GROUPS
    oh1 = lane == ex1
    oh2 = lane == ex2
    before = (lax.broadcasted_iota(jnp.int32, (tm, tm), 0) > lax.broadcasted_iota(jnp.int32, (tm, tm), 1)).astype(BF16)
    pre1 = jnp.dot(before, oh1.astype(BF16), preferred_element_type=F32)
    pre2 = jnp.dot(before, oh2.astype(BF16), preferred_element_type=F32)
    tot1 = jnp.sum(oh1.astype(F32), axis=0, keepdims=True)
    tot2 = jnp.sum(oh2.astype(F32), axis=0, keepdims=True)
    base = run[...]
    rank1 = jnp.sum(jnp.where(oh1, base + pre1, 0.0), axis=-1, keepdims=True)
    rank2 = jnp.sum(jnp.where(oh2, base + tot1 + pre2, 0.0), axis=-1, keepdims=True)
    run[...] = base + tot1 + tot2
    cnt_o[...] = base + tot1 + tot2
    route_o[...] = jnp.where(lane == 0, ex1, jnp.where(lane == 1, ex2, jnp.where(lane == 2, gate1, jnp.where(
        lane == 3, gate2, jnp.where(lane == 4, rank1, jnp.where(lane == 5, rank2, 0.0))))))


def _router_operands(g, w_group, b_group, w_expert, b_expert):
    d = g.shape[0]
    assert d == SUBLANES * LANES
    w_route = jnp.zeros((d, LANES), F32).at[:, :MOE_GROUPS].set(w_group)
    w_route = w_route.at[:, MOE_GROUPS:MOE_GROUPS + MOE_EXPERTS].set(w_expert)
    w_hi = w_route.astype(BF16)
    w_lo = (w_route - w_hi.astype(F32)).astype(BF16)
    bias = jnp.zeros((1, LANES), F32).at[0, :MOE_GROUPS].set(b_group)
    bias = bias.at[0, MOE_GROUPS:MOE_GROUPS + MOE_EXPERTS].set(b_expert)
    return g.reshape(1, d), w_hi, w_lo, bias


def _router_specs(tm, d):
    const = lambda shape: pl.BlockSpec(shape, lambda i: (0,) * len(shape))
    in_specs = [const((1, d)), const((d, LANES)), const((d, LANES)), const((1, LANES))]
    out_specs = (pl.BlockSpec((tm * SUBLANES, LANES), lambda i: (i, 0)), pl.BlockSpec((tm, LANES), lambda i: (i, 0)),
                 const((1, LANES)))
    out_shape = lambda n: (jax.ShapeDtypeStruct((n * SUBLANES, LANES), F32), jax.ShapeDtypeStruct((n, LANES), F32),
                           jax.ShapeDtypeStruct((1, LANES), F32))
    return in_specs, out_shape, out_specs, pltpu.VMEM((1, LANES), F32)


DMA_UNROLL = 8


def _tile_at(ref, tok):
    return ref.at[pl.ds(pl.multiple_of(tok * SUBLANES, SUBLANES), SUBLANES)]


def _dispatch_kernel(rows, dest_ref, zblk_ref, h_ref, xs_ref, zbuf, sem, zsem):
    @pl.when(pl.program_id(0) == 0)
    def _():
        zbuf[...] = jnp.zeros_like(zbuf)

        def zero_copy(j):
            start = pl.multiple_of(jnp.maximum(zblk_ref[0, j], 0) * (MOE_ROWS * SUBLANES), MOE_ROWS * SUBLANES)
            return pltpu.make_async_copy(zbuf, xs_ref.at[pl.ds(start, MOE_ROWS * SUBLANES)], zsem.at[0])

        def z_issue(j, c):
            pl.when(zblk_ref[0, j] >= 0)(lambda: zero_copy(j).start())
            return c

        def z_drain(j, c):
            pl.when(zblk_ref[0, j] >= 0)(lambda: zero_copy(j).wait())
            return c

        lax.fori_loop(0, zblk_ref.shape[1], z_issue, 0)
        lax.fori_loop(0, zblk_ref.shape[1], z_drain, 0)

    def issue(i, c):
        for choice in range(2):
            pltpu.make_async_copy(_tile_at(h_ref, i), _tile_at(xs_ref, dest_ref[0, 0, 2 * i + choice]),
                                  sem.at[choice]).start(priority=choice)
        return c

    lax.fori_loop(0, rows, issue, 0, unroll=DMA_UNROLL)
    for choice in range(2):
        pltpu.make_async_copy(h_ref, xs_ref.at[pl.ds(0, rows * SUBLANES)], sem.at[choice]).wait()


def _dispatch(h_tiles, dest, zero_blocks, n_pad, rows=512):
    n = h_tiles.shape[0] // SUBLANES
    rows = min(rows, n)
    nb = n // rows
    return pl.pallas_call(
        functools.partial(_dispatch_kernel, rows),
        out_shape=jax.ShapeDtypeStruct((n_pad * SUBLANES, LANES), h_tiles.dtype),
        grid=(nb,),
        in_specs=[pl.BlockSpec((1, 1, 2 * rows), lambda i: (i, 0, 0), memory_space=pltpu.SMEM),
                  pl.BlockSpec((1, zero_blocks.shape[0]), lambda i: (0, 0), memory_space=pltpu.SMEM),
                  pl.BlockSpec((rows * SUBLANES, LANES), lambda i: (i, 0))],
        out_specs=pl.BlockSpec(memory_space=pl.ANY),
        scratch_shapes=[pltpu.VMEM((MOE_ROWS * SUBLANES, LANES), h_tiles.dtype),
                        pltpu.SemaphoreType.DMA((2,)), pltpu.SemaphoreType.DMA((1,))],
        compiler_params=_params(("arbitrary",), 32),
        name="moe_dispatch",
    )(dest.reshape(nb, 1, 2 * rows), zero_blocks.reshape(1, -1), h_tiles)


def _moe_mlp_kernel(be_ref, used_ref, x_ref, wg_ref, wu_ref, wd_ref, o_ref, wg_s, wu_s, wd_s):
    i = pl.program_id(0)

    @pl.when(i < used_ref[0])
    def _():
        @pl.when((i == 0) | (be_ref[i] != be_ref[jnp.maximum(i - 1, 0)]))
        def _():
            wg_s[...] = wg_ref[0, 0].astype(BF16)
            wu_s[...] = wu_ref[0, 0].astype(BF16)
            wd_s[...] = wd_ref[0, 0].astype(BF16)

        x = _load_token_tiles(x_ref, MOE_ROWS).astype(BF16)
        hg = jnp.dot(x, wg_s[...], preferred_element_type=F32)
        hu = jnp.dot(x, wu_s[...], preferred_element_type=F32)
        hid = hg * _sigmoid(hg) * hu
        _store_token_tiles(o_ref, jnp.dot(hid.astype(BF16), wd_s[...], preferred_element_type=F32), MOE_ROWS)

    @pl.when(i >= used_ref[0])
    def _():
        o_ref[...] = jnp.zeros_like(o_ref)


def _moe_mlp(xs, block_expert, n_used, w_gate, w_up, w_down, layer):
    n_pad = xs.shape[0] // SUBLANES
    _, _, d, hid = w_gate.shape
    nb = n_pad // MOE_ROWS
    live = lambda i, used: jnp.minimum(i, used[0] - 1)
    tiles = pl.BlockSpec((MOE_ROWS * SUBLANES, LANES), lambda i, be, used: (live(i, used), 0))
    weight = lambda shape: pl.BlockSpec(shape, lambda i, be, used: (layer, be[live(i, used)], 0, 0))
    grid_spec = pltpu.PrefetchScalarGridSpec(
        num_scalar_prefetch=2,
        grid=(nb,),
        in_specs=[tiles, weight((1, 1, d, hid)), weight((1, 1, d, hid)), weight((1, 1, hid, d))],
        out_specs=pl.BlockSpec((MOE_ROWS * SUBLANES, LANES), lambda i, be, used: (i, 0)),
        scratch_shapes=[pltpu.VMEM((d, hid), BF16), pltpu.VMEM((d, hid), BF16), pltpu.VMEM((hid, d), BF16)],
    )
    return pl.pallas_call(
        _moe_mlp_kernel,
        out_shape=jax.ShapeDtypeStruct(xs.shape, F32),
        grid_spec=grid_spec,
        compiler_params=_params(("arbitrary",), 48),
        name="moe_mlp",
    )(block_expert, n_used, xs, w_gate, w_up, w_down)


def _combine_kernel(rows, pos_ref, nxt_ref, x_ref, route_ref, ys_ref, o_ref, bufs, sem):
    step = pl.program_id(0)

    def gather(idx_ref, parity):
        def issue(i, c):
            for choice in range(2):
                pltpu.make_async_copy(_tile_at(ys_ref, idx_ref[0, 0, 2 * i + choice]),
                                      _tile_at(bufs.at[parity, choice], i),
                                      sem.at[parity, choice]).start(priority=choice)
            return c

        lax.fori_loop(0, rows, issue, 0, unroll=DMA_UNROLL)

    pl.when(step == 0)(lambda: gather(pos_ref, 0))
    pl.when(step + 1 < pl.num_programs(0))(lambda: gather(nxt_ref, (step + 1) % 2))
    parity = step % 2
    for choice in range(2):
        pltpu.make_async_copy(ys_ref.at[pl.ds(0, rows * SUBLANES)], bufs.at[parity, choice],
                              sem.at[parity, choice]).wait()
    route = route_ref[...]
    o_ref[...] = (x_ref[...] + route[:, 2:3] * _load_token_tiles(bufs.at[parity, 0], rows)
                  + route[:, 3:4] * _load_token_tiles(bufs.at[parity, 1], rows))


def _combine(x, ys, route, pos, rows=512):
    n, d = x.shape
    rows = min(rows, n)
    nb = n // rows
    idx = lambda f: pl.BlockSpec((1, 1, 2 * rows), lambda i: (f(i), 0, 0), memory_space=pltpu.SMEM)
    pos = pos.reshape(nb, 1, 2 * rows)
    return pl.pallas_call(
        functools.partial(_combine_kernel, rows),
        out_shape=jax.ShapeDtypeStruct((n, d), F32),
        grid=(nb,),
        in_specs=[idx(lambda i: i), idx(lambda i: jnp.minimum(i + 1, nb - 1)),
                  pl.BlockSpec((rows, d), lambda i: (i, 0)),
                  pl.BlockSpec((rows, LANES), lambda i: (i, 0)),
                  pl.BlockSpec(memory_space=pl.ANY)],
        out_specs=pl.BlockSpec((rows, d), lambda i: (i, 0)),
        scratch_shapes=[pltpu.VMEM((2, 2, rows * SUBLANES, LANES), F32), pltpu.SemaphoreType.DMA((2, 2))],
        compiler_params=_params(("arbitrary",), 40),
        name="moe_combine",
    )(pos, pos, x, route, ys)


def _hier_moe(x, h, route, counts, w_gate, w_up, w_down, layer):
    n, _ = x.shape
    counts = counts[0, :MOE_EXPERTS].astype(jnp.int32)
    padded = (counts + MOE_ROWS - 1) // MOE_ROWS * MOE_ROWS
    cum_padded = jnp.cumsum(padded)
    expert = route[:, 0:2].astype(jnp.int32)
    dest = ((cum_padded - padded)[expert] + route[:, 4:6].astype(jnp.int32)).reshape(-1)
    n_blocks = 2 * n // MOE_ROWS + MOE_EXPERTS
    block_start = jnp.arange(n_blocks, dtype=jnp.int32) * MOE_ROWS
    block_expert = jnp.minimum(jnp.sum((block_start[:, None] >= cum_padded[None, :]).astype(jnp.int32), axis=1),
                               MOE_EXPERTS - 1)
    n_used = (cum_padded[-1:] // MOE_ROWS).astype(jnp.int32)

    spare = jnp.arange(n_blocks - MOE_EXPERTS, n_blocks, dtype=jnp.int32)
    zero_blocks = jnp.concatenate([jnp.where(padded > 0, cum_padded // MOE_ROWS - 1, -1).astype(jnp.int32),
                                   jnp.where(spare >= n_used[0], spare, -1)])
    xs = _dispatch(h, dest, zero_blocks, n_blocks * MOE_ROWS)
    ys = _moe_mlp(xs, block_expert, n_used, w_gate, w_up, w_down, layer)
    return _combine(x, ys, route, dest)


def _even_layer(x, bsz, t_len, norm_g, w_in, prm, v_first, router):
    r, lw, k, v, a, b, g, pool = _even_prep(x.reshape(bsz, t_len, -1), norm_g, w_in.astype(BF16), prm, v_first)
    y = _rwkv_chunk(r, lw, k, v, a, b)
    return _even_out(x, y, r, k, v, g, pool, prm, router), v


def _odd_layer(x, bsz, t_len, layer_idx, norm_g, w_in, prm, router):
    q, k, rest = _qk_prep(x.reshape(bsz, t_len, -1), norm_g, w_in.astype(BF16), prm["q_norm"], prm["k_norm"])
    lam_init = 0.8 - 0.6 * math.exp(-0.3 * layer_idx)
    lam = (jnp.exp(jnp.sum(prm["lam_q1"] * prm["lam_k1"])) - jnp.exp(jnp.sum(prm["lam_q2"] * prm["lam_k2"]))
           + lam_init)
    o_attn = _diff_attn(q, k, rest, lam, prm["subln"], 1.0 - lam_init)
    ng = bsz // S5_BATCH_GROUP
    u5 = rest[:, :, RW_WIDTH:].reshape(ng, S5_BATCH_GROUP, t_len, RW_WIDTH).transpose(0, 2, 1, 3)
    o_ssm = _s5(u5, prm).transpose(0, 2, 1, 3).reshape(bsz, t_len, RW_WIDTH)
    return _odd_out(x, o_attn, o_ssm, prm["w_out"], router)


def _s5_params(a_re, a_im, log_step, b_re, b_im, c_re, c_im, d_skip, w_glu):
    lam = lax.complex(jnp.minimum(a_re, -1e-4), a_im)
    lam_bar = jnp.exp(lam * jnp.exp(log_step))
    b_bar = ((lam_bar - 1.0) / lam)[..., None] * lax.complex(b_re, b_im)
    gpt = LANES // S5_GROUP_DIM
    eye = jnp.eye(gpt, dtype=F32)

    def in_map(part):
        p = part.reshape(S5_TILES, gpt, S5_STATE, S5_GROUP_DIM)
        return jnp.einsum("jgpc,gh->jgchp", p, eye).reshape(S5_TILES, LANES, gpt * S5_STATE)

    def out_map(part):
        p = part.reshape(S5_TILES, gpt, S5_GROUP_DIM, S5_STATE)
        return jnp.einsum("jgcp,gh->jgphc", p, eye).reshape(S5_TILES, gpt * S5_STATE, LANES)

    bt = jnp.concatenate([in_map(jnp.real(b_bar)), in_map(jnp.imag(b_bar))], axis=2).astype(BF16)
    ct = jnp.concatenate([out_map(c_re), -out_map(c_im)], axis=1).astype(BF16)
    return {"bt": bt, "ct": ct, "lam_re": jnp.real(lam_bar).reshape(1, -1), "lam_im": jnp.imag(lam_bar).reshape(1, -1),
            "d": d_skip, "w_glu": w_glu.astype(BF16)}


def _block_diag(blocks):
    g, r, c = blocks.shape
    return jnp.einsum("grc,gh->grhc", blocks, jnp.eye(g, dtype=blocks.dtype)).reshape(g * r, g * c)


def kernel(x, norm_mix_g, norm_ffn_g,
           even_w_in, rw_mu, rw_w0, rw_w2, rw_a0, rw_a2, rw_g2, rw_k_k, rw_k_a, rw_r_k,
           rw_ln_g, rw_ln_b, rw_v0, rw_v1, rw_v2, pool_w, pool_scale, even_w_out,
           odd_w_in, da_q_norm, da_k_norm, da_lam_q1, da_lam_k1, da_lam_q2, da_lam_k2, da_subln,
           s5_a_re, s5_a_im, s5_log_step, s5_b_re, s5_b_im, s5_c_re, s5_c_im, s5_d, s5_w_glu,
           odd_w_out,
           moe_w_group, moe_b_group, moe_w_expert, moe_b_expert, moe_w_gate, moe_w_up, moe_w_down):
    bsz, t_len, d = x.shape
    depth = norm_mix_g.shape[0]
    xf = x.reshape(bsz * t_len, d)
    v_first = None
    for layer in range(depth):
        i = layer // 2
        router = _router_operands(norm_ffn_g[layer], moe_w_group[layer], moe_b_group[layer], moe_w_expert[layer],
                                  moe_b_expert[layer])
        if layer % 2 == 0:
            rank = rw_w2.shape[1]
            wa = jnp.zeros((LANES, 2 * RW_WIDTH), F32)
            wa = wa.at[:rank, :RW_WIDTH].set(rw_w2[i]).at[rank:, RW_WIDTH:].set(rw_a2[i])
            prm = {"mu": rw_mu[i], "w0": rw_w0[i], "a0": rw_a0[i], "wa": wa.astype(BF16), "g2": rw_g2[i].astype(BF16),
                   "k_k": rw_k_k[i], "k_a": rw_k_a[i], "r_k": rw_r_k[i], "ln_g": rw_ln_g[i], "ln_b": rw_ln_b[i],
                   "pool_bd": _block_diag(pool_w[i]).astype(BF16), "pool_scale": pool_scale[i],
                   "w_out": even_w_out[i].astype(BF16)}
            if v_first is not None:
                vr = rw_v1.shape[2]
                prm["v0"] = rw_v0[i - 1]
                prm["v1"] = jnp.zeros((RW_WIDTH, LANES), F32).at[:, :vr].set(rw_v1[i - 1]).astype(BF16)
                prm["v2"] = jnp.zeros((LANES, RW_WIDTH), F32).at[:vr, :].set(rw_v2[i - 1]).astype(BF16)
            routed, v_new = _even_layer(xf, bsz, t_len, norm_mix_g[layer], even_w_in[i], prm, v_first, router)
            if v_first is None:
                v_first = v_new
        else:
            prm = _s5_params(s5_a_re[i], s5_a_im[i], s5_log_step[i], s5_b_re[i], s5_b_im[i], s5_c_re[i],
                             s5_c_im[i], s5_d[i].reshape(-1), s5_w_glu[i])
            prm.update({"q_norm": da_q_norm[i], "k_norm": da_k_norm[i], "lam_q1": da_lam_q1[i],
                        "lam_k1": da_lam_k1[i], "lam_q2": da_lam_q2[i], "lam_k2": da_lam_k2[i],
                        "subln": da_subln[i], "w_out": odd_w_out[i].astype(BF16)})
            routed = _odd_layer(xf, bsz, t_len, layer, norm_mix_g[layer], odd_w_in[i], prm, router)
        xf = _hier_moe(*routed, moe_w_gate, moe_w_up, moe_w_down, layer)
    return xf.reshape(bsz, t_len, d)
```

```python
import functools
import math

import jax
import jax.numpy as jnp
from jax import lax
from jax.experimental import pallas as pl
from jax.experimental.pallas import tpu as pltpu

F32 = jnp.float32
BF16 = jnp.bfloat16

LANES = 128
SUBLANES = 8
VMEM_BYTES_V7X = 64 * 1024 * 1024

D_MODEL = 1024
HEAD = 64
RW_WIDTH = 512
RW_SHIFT_COLS = 3 * RW_WIDTH + 64 + 64 + 128
RW_LN_EPS = 64e-5
POOL_WINDOWS = (2, 4, 8, 16)
POOL_HALO = 16
DA_HEADS = 4
SUBLN_EPS = 1e-5
ROPE_THETA = 10000.0
S5_GROUP_DIM = 16
S5_STATE = 64
S5_TILES = 4
S5_BATCH_GROUP = SUBLANES
MOE_GROUPS = 4
MOE_PER_GROUP = 8
MOE_EXPERTS = 32
MOE_ROWS = 512
RMS_EPS = 1e-6
CHUNK = 64
NEG_BIG = -1e30


def _params(semantics, vmem_mib):
    return pltpu.CompilerParams(dimension_semantics=semantics,
                                vmem_limit_bytes=min(vmem_mib * 1024 * 1024, VMEM_BYTES_V7X - 8 * 1024 * 1024))


def _bdot(a, b):
    return jnp.dot(a.astype(BF16), b.astype(BF16), preferred_element_type=F32)


def _bdot_nt(a, b):
    return lax.dot_general(a.astype(BF16), b.astype(BF16), (((1,), (1,)), ((), ())),
                           preferred_element_type=F32)


def _split_dot(x, m_bf16):
    hi = x.astype(BF16)
    lo = (x - hi.astype(F32)).astype(BF16)
    return (jnp.dot(hi, m_bf16, preferred_element_type=F32)
            + jnp.dot(lo, m_bf16, preferred_element_type=F32))


def _seg_ones(width):
    r = lax.broadcasted_iota(jnp.int32, (LANES, LANES), 0)
    c = lax.broadcasted_iota(jnp.int32, (LANES, LANES), 1)
    sh = int(math.log2(width))
    return ((r >> sh) == (c >> sh)).astype(BF16)


def _segsum(x, seg):
    tiles = [_split_dot(x[:, j * LANES:(j + 1) * LANES], seg) for j in range(x.shape[1] // LANES)]
    return tiles[0] if len(tiles) == 1 else jnp.concatenate(tiles, axis=1)


def _sigmoid(x):
    return 1.0 / (1.0 + jnp.exp(-x))


def _even_prep_kernel(has_vres, tb, *refs):
    if has_vres:
        (x_ref, gn_ref, win_ref, mu_ref, w0_ref, a0_ref, wa_ref, g2_ref, kk_ref, ka_ref, pw_ref, ps_ref,
         vf_ref, v0_ref, v1_ref, v2_ref,
         r_o, lw_o, k_o, v_o, a_o, b_o, g_o, pool_o, carry) = refs
    else:
        (x_ref, gn_ref, win_ref, mu_ref, w0_ref, a0_ref, wa_ref, g2_ref, kk_ref, ka_ref, pw_ref, ps_ref,
         r_o, lw_o, k_o, v_o, a_o, b_o, g_o, pool_o, carry) = refs
    ti = pl.program_id(1)

    @pl.when(ti == 0)
    def _():
        carry[...] = jnp.zeros_like(carry)

    x = x_ref[0]
    h = x * lax.rsqrt(jnp.mean(x * x, axis=-1, keepdims=True) + RMS_EPS) * gn_ref[...]
    u = jnp.dot(h.astype(BF16), win_ref[...], preferred_element_type=F32)
    ext = jnp.concatenate([carry[...], u], axis=0)
    carry[...] = u[tb - POOL_HALO:, :]

    p1 = ext[:, RW_SHIFT_COLS:]
    p2 = p1 + pltpu.roll(p1, 1, 0)
    p4 = p2 + pltpu.roll(p2, 2, 0)
    p8 = p4 + pltpu.roll(p4, 4, 0)
    p16 = p8 + pltpu.roll(p8, 8, 0)
    lane = lax.broadcasted_iota(jnp.int32, (1, RW_WIDTH), 1)
    grp = lane >> 7
    sums = jnp.where(grp == 0, p2, jnp.where(grp == 1, p4, jnp.where(grp == 2, p8, p16)))[POOL_HALO:]
    win = jnp.where(grp == 0, 2.0, jnp.where(grp == 1, 4.0, jnp.where(grp == 2, 8.0, 16.0)))
    n_seen = (ti * tb + lax.broadcasted_iota(jnp.int32, (tb, 1), 0) + 1).astype(F32)
    d = sums / jnp.minimum(n_seen, win) - u[:, RW_SHIFT_COLS:]
    pool_o[0] = (_bdot(d, pw_ref[...]) * ps_ref[...]).astype(pool_o.dtype)

    u_rw = u[:, :RW_SHIFT_COLS]
    prev = pltpu.roll(ext[:, :RW_SHIFT_COLS], 1, 0)[POOL_HALO:]
    m = u_rw + (prev - u_rw) * mu_ref[...]
    r = m[:, :RW_WIDTH]
    k = m[:, RW_WIDTH:2 * RW_WIDTH]
    v = m[:, 2 * RW_WIDTH:3 * RW_WIDTH]
    dwa = m[:, 3 * RW_WIDTH:3 * RW_WIDTH + LANES]
    dg = m[:, 3 * RW_WIDTH + LANES:]
    l128 = lax.broadcasted_iota(jnp.int32, (1, LANES), 1)
    dwa = jnp.where(l128 < HEAD, jnp.tanh(dwa), dwa)
    x12 = _bdot(dwa, wa_ref[...])
    z = -(w0_ref[...] + x12[:, :RW_WIDTH])
    softplus = jnp.maximum(z, 0.0) + jnp.log(1.0 + jnp.exp(-jnp.abs(z)))
    lw = -jnp.exp(-softplus - 0.5)
    a_i = _sigmoid(a0_ref[...] + x12[:, RW_WIDTH:])
    g_o[0] = _bdot(_sigmoid(dg), g2_ref[...]).astype(g_o.dtype)
    if has_vres:
        gate_v = _sigmoid(v0_ref[...] + _bdot(_bdot(v, v1_ref[...]), v2_ref[...]))
        v = v + (vf_ref[0] - v) * gate_v
    kk = k * kk_ref[...]
    ss = _segsum(kk * kk, _seg_ones(HEAD))
    kk = kk / jnp.maximum(jnp.sqrt(ss), 1e-12)
    r_o[0] = r.astype(r_o.dtype)
    lw_o[0] = lw
    k_o[0] = (k * (1.0 + (a_i - 1.0) * ka_ref[...])).astype(k_o.dtype)
    v_o[0] = v.astype(v_o.dtype)
    a_o[0] = (-kk).astype(a_o.dtype)
    b_o[0] = (kk * a_i).astype(b_o.dtype)


def _even_prep(x, norm_g, w_in, prm, v_first, tb=512):
    bsz, t_len, d = x.shape
    cin = w_in.shape[1]
    tb = min(tb, t_len)
    has_vres = v_first is not None
    row = lambda a: a.reshape(1, -1)
    full = lambda shape: pl.BlockSpec(shape, lambda b, t: (0,) * len(shape))
    seq = pl.BlockSpec((1, tb, RW_WIDTH), lambda b, t: (b, t, 0))
    ins = [x, row(norm_g), w_in, row(prm["mu"]), row(prm["w0"]), row(prm["a0"]), prm["wa"], prm["g2"], row(prm["k_k"]),
           row(prm["k_a"]), prm["pool_bd"], row(prm["pool_scale"])]
    specs = [pl.BlockSpec((1, tb, d), lambda b, t: (b, t, 0)), full((1, d)), full((d, cin)), full((1, RW_SHIFT_COLS)),
             full((1, RW_WIDTH)), full((1, RW_WIDTH)), full((LANES, 2 * RW_WIDTH)),
             full((LANES, RW_WIDTH)), full((1, RW_WIDTH)), full((1, RW_WIDTH)),
             full((RW_WIDTH, RW_WIDTH)), full((1, RW_WIDTH))]
    if has_vres:
        ins += [v_first, row(prm["v0"]), prm["v1"], prm["v2"]]
        specs += [seq, full((1, RW_WIDTH)), full((RW_WIDTH, LANES)), full((LANES, RW_WIDTH))]
    out = lambda dt: jax.ShapeDtypeStruct((bsz, t_len, RW_WIDTH), dt)
    return pl.pallas_call(
        functools.partial(_even_prep_kernel, has_vres, tb),
        out_shape=(out(BF16), out(F32)) + (out(BF16),) * 6,
        grid=(bsz, t_len // tb),
        in_specs=specs,
        out_specs=(seq,) * 8,
        scratch_shapes=[pltpu.VMEM((POOL_HALO, cin), F32)],
        compiler_params=_params(("parallel", "arbitrary"), 56),
        name="even_prep",
    )(*ins)


def _rwkv_chunk_kernel(nb, r_ref, lw_ref, k_ref, v_ref, a_ref, b_ref, y_ref, s_ref):
    ci = pl.program_id(1)

    @pl.when(ci == 0)
    def _():
        s_ref[...] = jnp.zeros_like(s_ref)

    L = CHUNK
    n_pair = RW_WIDTH // LANES
    tri = (lax.broadcasted_iota(jnp.int32, (L, L), 0) >= lax.broadcasted_iota(jnp.int32, (L, L), 1)).astype(BF16)
    lane = lax.broadcasted_iota(jnp.int32, (1, LANES), 1)
    m_a = lane < HEAD
    t_idx = lax.broadcasted_iota(jnp.int32, (L, LANES), 0)
    s_idx = lax.broadcasted_iota(jnp.int32, (L, LANES), 1) & (HEAD - 1)
    strict = t_idx > s_idx
    incl = t_idx >= s_idx
    r128 = lax.broadcasted_iota(jnp.int32, (LANES, LANES), 0)
    c128 = lax.broadcasted_iota(jnp.int32, (LANES, LANES), 1)
    eye = (r128 == c128).astype(F32)
    same_head = (r128 >> 6) == (c128 >> 6)

    def only_a(x):
        return jnp.where(m_a, x, jnp.zeros_like(x))

    def only_b(x):
        return jnp.where(m_a, jnp.zeros_like(x), x)

    def stack(x):
        return jnp.concatenate([only_a(x), only_b(x)], axis=0)

    def stack_sw(x):
        return jnp.concatenate([only_b(x), only_a(x)], axis=0)

    def mm(x, y):
        return jnp.dot(x, y, preferred_element_type=F32)

    def mm_nt(x, y):
        return lax.dot_general(x, y, (((1,), (1,)), ((), ())), preferred_element_type=F32)

    chains = []
    for bi in range(nb):
        lw = lw_ref[bi]
        c = _split_dot_lhs(tri, lw)
        e_pos = jnp.exp(c)
        e_neg = jnp.exp(-c)
        a_all = (jnp.exp(c - lw) * a_ref[bi]).astype(BF16)
        b_all = (b_ref[bi] * e_neg).astype(BF16)
        k_all = (k_ref[bi] * e_neg).astype(BF16)
        r_all = r_ref[bi] * e_pos
        v_all = v_ref[bi].astype(BF16)
        for p in range(n_pair):
            sl = slice(p * LANES, (p + 1) * LANES)
            chains.append({"bi": bi, "p": p, "sl": sl, "at": a_all[:, sl], "bt": b_all[:, sl], "kt": k_all[:, sl],
                           "rt": r_all[:, sl], "vv": v_all[:, sl], "gl": e_pos[L - 1:L, sl]})

    for ch in chains:
        rt16 = ch["rt"].astype(BF16)
        bk = jnp.concatenate([ch["bt"], ch["kt"]], axis=0)
        kb = jnp.concatenate([ch["kt"], ch["bt"]], axis=0)
        s_a = mm_nt(jnp.concatenate([only_a(ch["at"]), only_a(rt16)], axis=0), bk)
        s_b = mm_nt(jnp.concatenate([only_b(ch["at"]), only_b(rt16)], axis=0), kb)
        m_ha = jnp.where(strict, s_a[:L], 0.0)
        n_ha = jnp.where(incl, s_a[L:], 0.0)
        m_hb = jnp.where(strict, s_b[:L], 0.0)
        n_hb = jnp.where(incl, s_b[L:], 0.0)
        ch["bk"] = bk
        ch["bdm"] = jnp.concatenate([only_a(m_ha), only_b(m_hb)], axis=0)
        ch["ak_sw"] = jnp.where(m_a, m_hb, m_ha).astype(BF16)
        ch["n_lhs"] = jnp.concatenate([jnp.where(m_a, n_ha, n_hb),
                                       jnp.where(m_a, n_hb, n_ha)], axis=1).astype(BF16)
        ch["t_inv"] = eye + ch["bdm"]
        ch["m_pow"] = ch["bdm"].astype(BF16)
    for ch in chains:
        ch["w"] = mm(ch["ak_sw"], stack_sw(ch["vv"])).astype(BF16)
    for _ in range(int(math.log2(L)) - 1):
        for ch in chains:
            ch["m_pow"] = mm(ch["m_pow"], ch["m_pow"]).astype(BF16)
        for ch in chains:
            ch["t_inv"] = ch["t_inv"] + mm(ch["t_inv"].astype(BF16), ch["m_pow"])
    for ch in chains:
        t_pk = (ch["t_inv"][:L] + ch["t_inv"][L:]).astype(BF16)
        au = mm(t_pk, jnp.concatenate([stack(ch["at"]), stack(ch["w"])], axis=1))
        ch["a_hat"] = au[:, :LANES].astype(BF16)
        ch["u_hat"] = au[:, LANES:].astype(BF16)
    for ch in chains:
        rhs = jnp.concatenate([
            jnp.concatenate([stack(ch["u_hat"]), stack(ch["a_hat"])], axis=1),
            jnp.concatenate([stack_sw(ch["vv"]), jnp.zeros((LANES, LANES), BF16)], axis=1)], axis=0)
        yr = mm(ch["n_lhs"], rhs)
        ch["y_hat"] = yr[:, :LANES]
        ch["r_hat"] = (ch["rt"] + yr[:, LANES:]).astype(BF16)
        g_p = mm(ch["a_hat"].astype(F32).T.astype(BF16), ch["bt"])
        g_q = mm(jnp.concatenate([ch["u_hat"], ch["vv"]], axis=0).astype(F32).T.astype(BF16), ch["bk"])
        ch["p_bd"] = ((eye + jnp.where(same_head, g_p, 0.0)) * ch["gl"]).astype(BF16)
        ch["q_pk"] = jnp.where(m_a, g_q[:L], g_q[L:]) * ch["gl"]
    for ch in chains:
        s0 = s_ref[ch["bi"], ch["p"]]
        s16 = s0.astype(BF16)
        y_ref[ch["bi"], :, ch["sl"]] = mm_nt(ch["r_hat"], stack(s16)) + ch["y_hat"]
        s_ref[ch["bi"], ch["p"]] = mm(s16, ch["p_bd"]) + ch["q_pk"]


def _split_dot_lhs(m_bf16, x):
    hi = x.astype(BF16)
    lo = (x - hi.astype(F32)).astype(BF16)
    return (jnp.dot(m_bf16, hi, preferred_element_type=F32)
            + jnp.dot(m_bf16, lo, preferred_element_type=F32))


def _rwkv_chunk(r, lw, k, v, a, b, nb=4):
    bsz, t_len, _ = r.shape
    seq = pl.BlockSpec((nb, CHUNK, RW_WIDTH), lambda bi, ci: (bi, ci, 0))
    return pl.pallas_call(
        functools.partial(_rwkv_chunk_kernel, nb),
        out_shape=jax.ShapeDtypeStruct((bsz, t_len, RW_WIDTH), F32),
        grid=(bsz // nb, t_len // CHUNK),
        in_specs=[seq] * 6,
        out_specs=seq,
        scratch_shapes=[pltpu.VMEM((nb, RW_WIDTH // LANES, HEAD, LANES), F32)],
        compiler_params=_params(("parallel", "arbitrary"), 32),
        name="rwkv_chunk",
    )(r, lw, k, v, a, b)


def _even_out_kernel(tm, x_ref, y_ref, r_ref, k_ref, v_ref, g_ref, p_ref, lng_ref, lnb_ref, rk_ref, wo_ref,
                     gf_ref, whi_ref, wlo_ref, rb_ref, o_ref, h_o, route_o, cnt_o, run):
    seg = _seg_ones(HEAD)
    y = y_ref[...]
    mean = _segsum(y, seg) * (1.0 / HEAD)
    yc = y - mean
    var = _segsum(yc * yc, seg) * (1.0 / HEAD)
    yn = yc * lax.rsqrt(var + RW_LN_EPS) * lng_ref[...] + lnb_ref[...]
    bonus = _segsum(r_ref[...].astype(F32) * k_ref[...].astype(F32) * rk_ref[...], seg) * v_ref[...].astype(F32)
    o_rw = (yn + bonus) * g_ref[...].astype(F32)
    cat = jnp.concatenate([o_rw.astype(BF16), p_ref[...]], axis=1)
    x_new = x_ref[...] + jnp.dot(cat, wo_ref[...], preferred_element_type=F32)
    o_ref[...] = x_new
    _route_block(tm, x_new, gf_ref, whi_ref, wlo_ref, rb_ref, h_o, route_o, cnt_o, run)


def _even_out(x, y, r, k, v, g, pool, prm, router, tm=512):
    n, d = x.shape
    tm = min(tm, n)
    half = pl.BlockSpec((tm, RW_WIDTH), lambda i: (i, 0))
    vec = pl.BlockSpec((1, RW_WIDTH), lambda i: (0, 0))
    fl = lambda a: a.reshape(n, RW_WIDTH)
    r_in, r_shape, r_out, r_scratch = _router_specs(tm, d)
    return pl.pallas_call(
        functools.partial(_even_out_kernel, tm),
        out_shape=(jax.ShapeDtypeStruct((n, d), F32),) + r_shape(n),
        grid=(n // tm,),
        in_specs=[pl.BlockSpec((tm, d), lambda i: (i, 0))] + [half] * 6 + [vec] * 3
                 + [pl.BlockSpec((d, d), lambda i: (0, 0))] + r_in,
        out_specs=(pl.BlockSpec((tm, d), lambda i: (i, 0)),) + r_out,
        scratch_shapes=[r_scratch],
        compiler_params=_params(("arbitrary",), 48),
        name="even_out",
    )(x, fl(y), fl(r), fl(k), fl(v), fl(g), fl(pool), prm["ln_g"].reshape(1, -1), prm["ln_b"].reshape(1, -1),
      prm["r_k"].reshape(1, -1), prm["w_out"], *router)


def _qk_prep_kernel(x_ref, gn_ref, win_ref, gq_ref, gk_ref, cos_ref, sin_ref, q_o, k_o, rest_o):
    xin = x_ref[0]
    h = xin * lax.rsqrt(jnp.mean(xin * xin, axis=-1, keepdims=True) + RMS_EPS) * gn_ref[...]
    u = jnp.dot(h.astype(BF16), win_ref[...], preferred_element_type=F32)
    rest_o[0] = u[:, 2 * RW_WIDTH:]
    x = u[:, :2 * RW_WIDTH]
    cos = cos_ref[...]
    sin = sin_ref[...]
    seg = _seg_ones(HEAD)
    lane = lax.broadcasted_iota(jnp.int32, (1, LANES), 1)
    first = (lane & (HEAD - 1)) < HEAD // 2
    nq = RW_WIDTH // LANES
    for j in range(2 * nq):
        xt = x[:, j * LANES:(j + 1) * LANES]
        ms = _split_dot(xt * xt, seg) * (1.0 / HEAD)
        gain = gq_ref[...] if j < nq else gk_ref[...]
        xn = xt * lax.rsqrt(ms + RMS_EPS) * gain
        partner = jnp.where(first, pltpu.roll(xn, LANES - HEAD // 2, 1), pltpu.roll(xn, HEAD // 2, 1))
        out = xn * cos + partner * sin
        if j < nq:
            q_o[0, :, j * LANES:(j + 1) * LANES] = (out * (HEAD ** -0.5)).astype(BF16)
        else:
            k_o[0, :, (j - nq) * LANES:(j - nq + 1) * LANES] = out.astype(BF16)


def _qk_prep(x, norm_g, w_in, q_norm, k_norm, tb=512):
    bsz, t_len, d = x.shape
    cin = w_in.shape[1]
    tb = min(tb, t_len)
    inv_freq = ROPE_THETA ** (-jnp.arange(0, HEAD, 2, dtype=F32) / HEAD)
    ang = jnp.arange(t_len, dtype=F32)[:, None] * inv_freq[None, :]
    cos = jnp.tile(jnp.cos(ang), (1, LANES // (HEAD // 2)))
    sin_half = jnp.concatenate([-jnp.sin(ang), jnp.sin(ang)], axis=1)
    sin = jnp.tile(sin_half, (1, LANES // HEAD))
    tile2 = lambda g: jnp.tile(g, LANES // HEAD).reshape(1, LANES)
    out = jax.ShapeDtypeStruct((bsz, t_len, RW_WIDTH), BF16)
    tab = pl.BlockSpec((tb, LANES), lambda b, t: (t, 0))
    vec = pl.BlockSpec((1, LANES), lambda b, t: (0, 0))
    seq = pl.BlockSpec((1, tb, RW_WIDTH), lambda b, t: (b, t, 0))
    const = lambda shape: pl.BlockSpec(shape, lambda b, t: (0,) * len(shape))
    return pl.pallas_call(
        _qk_prep_kernel,
        out_shape=(out, out, jax.ShapeDtypeStruct((bsz, t_len, cin - 2 * RW_WIDTH), F32)),
        grid=(bsz, t_len // tb),
        in_specs=[pl.BlockSpec((1, tb, d), lambda b, t: (b, t, 0)), const((1, d)), const((d, cin)), vec, vec, tab, tab],
        out_specs=(seq, seq, pl.BlockSpec((1, tb, cin - 2 * RW_WIDTH), lambda b, t: (b, t, 0))),
        compiler_params=_params(("parallel", "parallel"), 48),
        name="qk_prep",
    )(x, norm_g.reshape(1, d), w_in, tile2(q_norm), tile2(k_norm), cos, sin)


def _diff_attn_kernel(tq, nh, out_scale, q_ref, k_ref, v_ref, lam_ref, sub_ref, o_ref):
    qi = pl.program_id(2)
    lane = lax.broadcasted_iota(jnp.int32, (1, LANES), 1)
    m_a = lane < HEAD

    def prefix(n_blk):
        kl = n_blk * tq
        on_diag = lax.broadcasted_iota(jnp.int32, (tq, tq), 0) >= lax.broadcasted_iota(jnp.int32, (tq, tq), 1)
        qs, ks, vs = [], [], []
        for h in range(nh):
            sl = slice(h * LANES, (h + 1) * LANES)
            q = q_ref[0, :, sl]
            zero = jnp.zeros_like(q)
            qs += [jnp.where(m_a, q, zero), jnp.where(m_a, zero, q)]
            ks += [k_ref[0, :kl, sl]] * 2
            vs += [jnp.concatenate([v_ref[0, :kl, sl].astype(BF16), jnp.ones((kl, LANES), BF16)], axis=1)] * 2
        s = [lax.dot_general(q, k, (((1,), (1,)), ((), ())), preferred_element_type=F32) for q, k in zip(qs, ks)]
        last = [jnp.where(on_diag, x[:, kl - tq:], NEG_BIG) for x in s]
        if n_blk > 1:
            last = [jnp.concatenate([x[:, :kl - tq], y], axis=1) for x, y in zip(s, last)]
        top = [jnp.max(x, axis=-1, keepdims=True) for x in last]
        e = [jnp.exp((x - m).astype(BF16)) for x, m in zip(last, top)]
        pv = [jnp.dot(x, v, preferred_element_type=F32) for x, v in zip(e, vs)]
        pv = [x[:, :LANES] / x[:, LANES:] for x in pv]
        for h in range(nh):
            sl = slice(h * LANES, (h + 1) * LANES)
            o = pv[2 * h] - lam_ref[...] * pv[2 * h + 1]
            ms = jnp.mean(o * o, axis=-1, keepdims=True)
            o_ref[0, :, sl] = o * lax.rsqrt(ms + SUBLN_EPS) * sub_ref[...] * out_scale

    for blk in range(k_ref.shape[1] // tq):
        pl.when(qi == blk)(functools.partial(prefix, blk + 1))


def _diff_attn(q, k, v_src, lam, subln, out_scale, tq=256, nh=4):
    bsz, t_len, _ = q.shape
    tq = min(tq, t_len)
    wide = nh * LANES
    return pl.pallas_call(
        functools.partial(_diff_attn_kernel, tq, nh, out_scale),
        out_shape=jax.ShapeDtypeStruct((bsz, t_len, RW_WIDTH), F32),
        grid=(bsz, DA_HEADS // nh, t_len // tq),
        in_specs=[pl.BlockSpec((1, tq, wide), lambda b, h, i: (b, i, h)),
                  pl.BlockSpec((1, t_len, wide), lambda b, h, i: (b, 0, h)),
                  pl.BlockSpec((1, t_len, wide), lambda b, h, i: (b, 0, h)),
                  pl.BlockSpec((1, 1), lambda b, h, i: (0, 0)),
                  pl.BlockSpec((1, LANES), lambda b, h, i: (0, 0))],
        out_specs=pl.BlockSpec((1, tq, wide), lambda b, h, i: (b, i, h)),
        compiler_params=_params(("parallel", "parallel", "arbitrary"), 48),
        name="diff_attn",
    )(q, k, v_src, lam.reshape(1, 1), subln.reshape(1, LANES))


def _s5_kernel(tb, u_ref, bt_ref, lre_ref, lim_ref, ct_ref, d_ref, wg_ref, o_ref, xs, st):
    ti = pl.program_id(1)

    @pl.when(ti == 0)
    def _():
        st[...] = jnp.zeros_like(st)

    rows = tb * SUBLANES
    half = RW_WIDTH
    u = u_ref[0].reshape(rows, RW_WIDTH)
    for j in range(S5_TILES):
        xs[:, 2 * half * j:2 * half * (j + 1)] = _bdot(u[:, j * LANES:(j + 1) * LANES], bt_ref[j])
    for j in range(S5_TILES):
        re = slice(2 * half * j, 2 * half * j + half)
        im = slice(2 * half * j + half, 2 * half * (j + 1))
        lr = jnp.broadcast_to(lre_ref[:, half * j:half * (j + 1)], (SUBLANES, half))
        li = jnp.broadcast_to(lim_ref[:, half * j:half * (j + 1)], (SUBLANES, half))

        def step(t, carry, re=re, im=im, lr=lr, li=li):
            xr, xi = carry
            row = pl.multiple_of(t * SUBLANES, SUBLANES)
            nr = lr * xr - li * xi + xs[pl.ds(row, SUBLANES), re]
            ni = lr * xi + li * xr + xs[pl.ds(row, SUBLANES), im]
            xs[pl.ds(row, SUBLANES), re] = nr
            xs[pl.ds(row, SUBLANES), im] = ni
            return nr, ni

        xr, xi = lax.fori_loop(0, tb, step, (st[:, re], st[:, im]), unroll=4)
        st[:, re] = xr
        st[:, im] = xi
    y = jnp.concatenate([_bdot(xs[:, 2 * half * j:2 * half * (j + 1)], ct_ref[j]) for j in range(S5_TILES)], axis=1)
    y = y + d_ref[...] * u
    z = 0.5 * y * (1.0 + jnp.tanh(math.sqrt(2.0 / math.pi) * (y + 0.044715 * (y * y * y))))
    o = z * _sigmoid(_bdot(z, wg_ref[...]))
    o_ref[0] = o.reshape(tb, SUBLANES, RW_WIDTH)


def _s5(u5, prm, tb=128):
    ng, t_len, _, _ = u5.shape
    tb = min(tb, t_len)
    full = lambda shape: pl.BlockSpec(shape, lambda g, t: (0,) * len(shape))
    n_state = 2 * RW_WIDTH * S5_TILES
    return pl.pallas_call(
        functools.partial(_s5_kernel, tb),
        out_shape=jax.ShapeDtypeStruct(u5.shape, F32),
        grid=(ng, t_len // tb),
        in_specs=[pl.BlockSpec((1, tb, SUBLANES, RW_WIDTH), lambda g, t: (g, t, 0, 0)),
                  full((S5_TILES, LANES, 2 * RW_WIDTH)), full((1, n_state // 2)), full((1, n_state // 2)),
                  full((S5_TILES, 2 * RW_WIDTH, LANES)), full((1, RW_WIDTH)), full((RW_WIDTH, RW_WIDTH))],
        out_specs=pl.BlockSpec((1, tb, SUBLANES, RW_WIDTH), lambda g, t: (g, t, 0, 0)),
        scratch_shapes=[pltpu.VMEM((tb * SUBLANES, n_state), F32), pltpu.VMEM((SUBLANES, n_state), F32)],
        compiler_params=_params(("parallel", "arbitrary"), 48),
        name="s5_scan",
    )(u5, prm["bt"], prm["lam_re"], prm["lam_im"], prm["ct"], prm["d"].reshape(1, -1), prm["w_glu"])


def _odd_out_kernel(tm, x_ref, a_ref, s_ref, wo_ref, gf_ref, whi_ref, wlo_ref, rb_ref, o_ref, h_o, route_o, cnt_o, run):
    cat = jnp.concatenate([a_ref[...], s_ref[...]], axis=1)
    x_new = x_ref[...] + _bdot(cat, wo_ref[...])
    o_ref[...] = x_new
    _route_block(tm, x_new, gf_ref, whi_ref, wlo_ref, rb_ref, h_o, route_o, cnt_o, run)


def _odd_out(x, o_attn, o_ssm, w_out, router, tm=512):
    n, d = x.shape
    tm = min(tm, n)
    half = pl.BlockSpec((tm, RW_WIDTH), lambda i: (i, 0))
    r_in, r_shape, r_out, r_scratch = _router_specs(tm, d)
    return pl.pallas_call(
        functools.partial(_odd_out_kernel, tm),
        out_shape=(jax.ShapeDtypeStruct((n, d), F32),) + r_shape(n),
        grid=(n // tm,),
        in_specs=[pl.BlockSpec((tm, d), lambda i: (i, 0)), half, half, pl.BlockSpec((d, d), lambda i: (0, 0))] + r_in,
        out_specs=(pl.BlockSpec((tm, d), lambda i: (i, 0)),) + r_out,
        scratch_shapes=[r_scratch],
        compiler_params=_params(("arbitrary",), 40),
        name="odd_out",
    )(x, o_attn.reshape(n, RW_WIDTH), o_ssm.reshape(n, RW_WIDTH), w_out, *router)


def _store_token_tiles(ref, x, n_tok):
    for c in range(SUBLANES):
        ref[pl.ds(c, n_tok, stride=SUBLANES), :] = x[:, c * LANES:(c + 1) * LANES]


def _load_token_tiles(ref, n_tok):
    return jnp.concatenate([ref[pl.ds(c, n_tok, stride=SUBLANES), :] for c in range(SUBLANES)], axis=1)


def _route_block(tm, x, g_ref, whi_ref, wlo_ref, b_ref, h_o, route_o, cnt_o, run):
    @pl.when(pl.program_id(0) == 0)
    def _():
        run[...] = jnp.zeros_like(run)

    ms = jnp.mean(x * x, axis=-1, keepdims=True)
    h = x * lax.rsqrt(ms + RMS_EPS) * g_ref[...]
    _store_token_tiles(h_o, h, tm)
    h_hi = h.astype(BF16)
    h_lo = (h - h_hi.astype(F32)).astype(BF16)
    w_hi = whi_ref[...]
    both = jnp.dot(h_hi, jnp.concatenate([w_hi, wlo_ref[...]], axis=1), preferred_element_type=F32)
    logits = both[:, :LANES] + both[:, LANES:] + jnp.dot(h_lo, w_hi, preferred_element_type=F32) + b_ref[...]
    lane = lax.broadcasted_iota(jnp.int32, logits.shape, 1).astype(F32)
    far = float(LANES)
    is_g = lane < MOE_GROUPS
    g_max = jnp.max(jnp.where(is_g, logits, NEG_BIG), axis=-1, keepdims=True)
    g_sum = jnp.sum(jnp.where(is_g, jnp.exp(jnp.minimum(logits - g_max, 0.0)), 0.0), axis=-1, keepdims=True)
    g_top = jnp.min(jnp.where(is_g & (logits == g_max), lane, far), axis=-1, keepdims=True)
    lo = MOE_GROUPS + MOE_PER_GROUP * g_top
    in_grp = (lane >= lo) & (lane < lo + MOE_PER_GROUP)
    e1 = jnp.max(jnp.where(in_grp, logits, NEG_BIG), axis=-1, keepdims=True)
    i1 = jnp.min(jnp.where(in_grp & (logits == e1), lane, far), axis=-1, keepdims=True)
    rest = in_grp & (lane != i1)
    e2 = jnp.max(jnp.where(rest, logits, NEG_BIG), axis=-1, keepdims=True)
    i2 = jnp.min(jnp.where(rest & (logits == e2), lane, far), axis=-1, keepdims=True)
    ratio = jnp.exp(e2 - e1)
    gate1 = 1.0 / (g_sum * (1.0 + ratio))
    gate2 = gate1 * ratio
    ex1 = i1 - MOE_GROUPS
    ex2 = i2 - MOE_GROUPS
    oh1 = lane == ex1
    oh2 = lane == ex2
    before = (lax.broadcasted_iota(jnp.int32, (tm, tm), 0) > lax.broadcasted_iota(jnp.int32, (tm, tm), 1)).astype(BF16)
    pre1 = jnp.dot(before, oh1.astype(BF16), preferred_element_type=F32)
    pre2 = jnp.dot(before, oh2.astype(BF16), preferred_element_type=F32)
    tot1 = jnp.sum(oh1.astype(F32), axis=0, keepdims=True)
    tot2 = jnp.sum(oh2.astype(F32), axis=0, keepdims=True)
    base = run[...]
    rank1 = jnp.sum(jnp.where(oh1, base + pre1, 0.0), axis=-1, keepdims=True)
    rank2 = jnp.sum(jnp.where(oh2, base + tot1 + pre2, 0.0), axis=-1, keepdims=True)
    run[...] = base + tot1 + tot2
    cnt_o[...] = base + tot1 + tot2
    route_o[...] = jnp.where(lane == 0, ex1, jnp.where(lane == 1, ex2, jnp.where(lane == 2, gate1, jnp.where(
        lane == 3, gate2, jnp.where(lane == 4, rank1, jnp.where(lane == 5, rank2, 0.0))))))


def _router_operands(g, w_group, b_group, w_expert, b_expert):
    d = g.shape[0]
    assert d == SUBLANES * LANES
    w_route = jnp.zeros((d, LANES), F32).at[:, :MOE_GROUPS].set(w_group)
    w_route = w_route.at[:, MOE_GROUPS:MOE_GROUPS + MOE_EXPERTS].set(w_expert)
    w_hi = w_route.astype(BF16)
    w_lo = (w_route - w_hi.astype(F32)).astype(BF16)
    bias = jnp.zeros((1, LANES), F32).at[0, :MOE_GROUPS].set(b_group)
    bias = bias.at[0, MOE_GROUPS:MOE_GROUPS + MOE_EXPERTS].set(b_expert)
    return g.reshape(1, d), w_hi, w_lo, bias


def _router_specs(tm, d):
    const = lambda shape: pl.BlockSpec(shape, lambda i: (0,) * len(shape))
    in_specs = [const((1, d)), const((d, LANES)), const((d, LANES)), const((1, LANES))]
    out_specs = (pl.BlockSpec((tm * SUBLANES, LANES), lambda i: (i, 0)), pl.BlockSpec((tm, LANES), lambda i: (i, 0)),
                 const((1, LANES)))
    out_shape = lambda n: (jax.ShapeDtypeStruct((n * SUBLANES, LANES), F32), jax.ShapeDtypeStruct((n, LANES), F32),
                           jax.ShapeDtypeStruct((1, LANES), F32))
    return in_specs, out_shape, out_specs, pltpu.VMEM((1, LANES), F32)


DMA_UNROLL = 16


def _tile_at(ref, tok):
    return ref.at[pl.ds(pl.multiple_of(tok * SUBLANES, SUBLANES), SUBLANES)]


def _dispatch_kernel(rows, dest_ref, zblk_ref, h_ref, xs_ref, zbuf, sem, zsem):
    @pl.when(pl.program_id(0) == 0)
    def _():
        zbuf[...] = jnp.zeros_like(zbuf)

        def zero_copy(j):
            start = pl.multiple_of(jnp.maximum(zblk_ref[0, j], 0) * (MOE_ROWS * SUBLANES), MOE_ROWS * SUBLANES)
            return pltpu.make_async_copy(zbuf, xs_ref.at[pl.ds(start, MOE_ROWS * SUBLANES)], zsem.at[0])

        def z_issue(j, c):
            pl.when(zblk_ref[0, j] >= 0)(lambda: zero_copy(j).start())
            return c

        def z_drain(j, c):
            pl.when(zblk_ref[0, j] >= 0)(lambda: zero_copy(j).wait())
            return c

        lax.fori_loop(0, zblk_ref.shape[1], z_issue, 0)
        lax.fori_loop(0, zblk_ref.shape[1], z_drain, 0)

    def issue(i, c):
        for choice in range(2):
            pltpu.make_async_copy(_tile_at(h_ref, i), _tile_at(xs_ref, dest_ref[0, 0, 2 * i + choice]),
                                  sem.at[choice]).start(priority=choice)
        return c

    lax.fori_loop(0, rows, issue, 0, unroll=DMA_UNROLL)
    for choice in range(2):
        pltpu.make_async_copy(h_ref, xs_ref.at[pl.ds(0, rows * SUBLANES)], sem.at[choice]).wait()


def _dispatch(h_tiles, dest, zero_blocks, n_pad, rows=512):
    n = h_tiles.shape[0] // SUBLANES
    rows = min(rows, n)
    nb = n // rows
    return pl.pallas_call(
        functools.partial(_dispatch_kernel, rows),
        out_shape=jax.ShapeDtypeStruct((n_pad * SUBLANES, LANES), h_tiles.dtype),
        grid=(nb,),
        in_specs=[pl.BlockSpec((1, 1, 2 * rows), lambda i: (i, 0, 0), memory_space=pltpu.SMEM),
                  pl.BlockSpec((1, zero_blocks.shape[0]), lambda i: (0, 0), memory_space=pltpu.SMEM),
                  pl.BlockSpec((rows * SUBLANES, LANES), lambda i: (i, 0))],
        out_specs=pl.BlockSpec(memory_space=pl.ANY),
        scratch_shapes=[pltpu.VMEM((MOE_ROWS * SUBLANES, LANES), h_tiles.dtype),
                        pltpu.SemaphoreType.DMA((2,)), pltpu.SemaphoreType.DMA((1,))],
        compiler_params=_params(("arbitrary",), 32),
        name="moe_dispatch",
    )(dest.reshape(nb, 1, 2 * rows), zero_blocks.reshape(1, -1), h_tiles)


def _moe_mlp_kernel(be_ref, used_ref, x_ref, wg_ref, wu_ref, wd_ref, o_ref, wg_s, wu_s, wd_s):
    i = pl.program_id(0)

    @pl.when(i < used_ref[0])
    def _():
        @pl.when((i == 0) | (be_ref[i] != be_ref[jnp.maximum(i - 1, 0)]))
        def _():
            wg_s[...] = wg_ref[0, 0].astype(BF16)
            wu_s[...] = wu_ref[0, 0].astype(BF16)
            wd_s[...] = wd_ref[0, 0].astype(BF16)

        x = _load_token_tiles(x_ref, MOE_ROWS).astype(BF16)
        hg = jnp.dot(x, wg_s[...], preferred_element_type=F32)
        hu = jnp.dot(x, wu_s[...], preferred_element_type=F32)
        hid = hg * _sigmoid(hg) * hu
        _store_token_tiles(o_ref, jnp.dot(hid.astype(BF16), wd_s[...], preferred_element_type=F32), MOE_ROWS)

    @pl.when(i >= used_ref[0])
    def _():
        o_ref[...] = jnp.zeros_like(o_ref)


def _moe_mlp(xs, block_expert, n_used, w_gate, w_up, w_down, layer):
    n_pad = xs.shape[0] // SUBLANES
    _, _, d, hid = w_gate.shape
    nb = n_pad // MOE_ROWS
    live = lambda i, used: jnp.minimum(i, used[0] - 1)
    tiles = pl.BlockSpec((MOE_ROWS * SUBLANES, LANES), lambda i, be, used: (live(i, used), 0))
    weight = lambda shape: pl.BlockSpec(shape, lambda i, be, used: (layer, be[live(i, used)], 0, 0))
    grid_spec = pltpu.PrefetchScalarGridSpec(
        num_scalar_prefetch=2,
        grid=(nb,),
        in_specs=[tiles, weight((1, 1, d, hid)), weight((1, 1, d, hid)), weight((1, 1, hid, d))],
        out_specs=pl.BlockSpec((MOE_ROWS * SUBLANES, LANES), lambda i, be, used: (i, 0)),
        scratch_shapes=[pltpu.VMEM((d, hid), BF16), pltpu.VMEM((d, hid), BF16), pltpu.VMEM((hid, d), BF16)],
    )
    return pl.pallas_call(
        _moe_mlp_kernel,
        out_shape=jax.ShapeDtypeStruct(xs.shape, F32),
        grid_spec=grid_spec,
        compiler_params=_params(("arbitrary",), 48),
        name="moe_mlp",
    )(block_expert, n_used, xs, w_gate, w_up, w_down)


COMBINE_AHEAD = 2


def _combine_kernel(rows, *refs):
    idx_refs = refs[:COMBINE_AHEAD + 1]
    x_ref, route_ref, ys_ref, o_ref, bufs, sem = refs[COMBINE_AHEAD + 1:]
    ring = COMBINE_AHEAD + 1
    step = pl.program_id(0)
    n_steps = pl.num_programs(0)
    cur = step % ring
    group = SUBLANES

    def issue(idx_ref, dst, first):
        for i in range(group):
            for choice in range(2):
                pltpu.make_async_copy(_tile_at(ys_ref, idx_ref[0, 0, 2 * (first + i) + choice]),
                                      _tile_at(bufs.at[dst, choice], first + i),
                                      sem.at[dst, choice]).start(priority=choice)

    def combine(first):
        rows_g = pl.ds(pl.multiple_of(first, group), group)
        gate = route_ref[rows_g, :]
        tile0 = pl.multiple_of(first * SUBLANES, group * SUBLANES)
        for c in range(SUBLANES):
            lanes = slice(c * LANES, (c + 1) * LANES)
            picked = [bufs[cur, choice, pl.ds(tile0 + c, group, stride=SUBLANES), :] for choice in range(2)]
            o_ref[rows_g, lanes] = x_ref[rows_g, lanes] + gate[:, 2:3] * picked[0] + gate[:, 3:4] * picked[1]

    def trips(body):
        lax.fori_loop(0, rows // group, lambda t, c: (body(t * group), c)[1], 0, unroll=4)

    for ahead in range(COMBINE_AHEAD):
        pl.when((step == 0) & (ahead < n_steps))(
            lambda ahead=ahead: trips(lambda first: issue(idx_refs[ahead], ahead, first)))
    for choice in range(2):
        pltpu.make_async_copy(ys_ref.at[pl.ds(0, rows * SUBLANES)], bufs.at[cur, choice], sem.at[cur, choice]).wait()
    more = step + COMBINE_AHEAD < n_steps
    pl.when(more)(lambda: trips(lambda first: (combine(first),
                                               issue(idx_refs[COMBINE_AHEAD], (step + COMBINE_AHEAD) % ring, first))))
    pl.when(jnp.logical_not(more))(lambda: trips(combine))


def _combine(x, ys, route, pos, rows=512):
    n, d = x.shape
    rows = min(rows, n)
    nb = n // rows
    idx = lambda ahead: pl.BlockSpec((1, 1, 2 * rows), lambda i: (jnp.minimum(i + ahead, nb - 1), 0, 0),
                                     memory_space=pltpu.SMEM)
    pos = pos.reshape(nb, 1, 2 * rows)
    ring = COMBINE_AHEAD + 1
    return pl.pallas_call(
        functools.partial(_combine_kernel, rows),
        out_shape=jax.ShapeDtypeStruct((n, d), F32),
        grid=(nb,),
        in_specs=[idx(ahead) for ahead in range(ring)]
                 + [pl.BlockSpec((rows, d), lambda i: (i, 0)),
                    pl.BlockSpec((rows, LANES), lambda i: (i, 0)),
                    pl.BlockSpec(memory_space=pl.ANY)],
        out_specs=pl.BlockSpec((rows, d), lambda i: (i, 0)),
        scratch_shapes=[pltpu.VMEM((ring, 2, rows * SUBLANES, LANES), F32), pltpu.SemaphoreType.DMA((ring, 2))],
        compiler_params=_params(("arbitrary",), 40),
        name="moe_combine",
    )(*([pos] * ring), x, route, ys)


def _hier_moe(x, h, route, counts, w_gate, w_up, w_down, layer):
    n, _ = x.shape
    counts = counts[0, :MOE_EXPERTS].astype(jnp.int32)
    padded = (counts + MOE_ROWS - 1) // MOE_ROWS * MOE_ROWS
    cum_padded = jnp.cumsum(padded)
    expert = route[:, 0:2].astype(jnp.int32)
    seg_start = jnp.sum(jnp.where(expert[..., None] == jnp.arange(MOE_EXPERTS, dtype=jnp.int32),
                                  cum_padded - padded, 0), axis=-1)
    dest = (seg_start + route[:, 4:6].astype(jnp.int32)).reshape(-1)
    n_blocks = 2 * n // MOE_ROWS + MOE_EXPERTS
    block_start = jnp.arange(n_blocks, dtype=jnp.int32) * MOE_ROWS
    block_expert = jnp.minimum(jnp.sum((block_start[:, None] >= cum_padded[None, :]).astype(jnp.int32), axis=1),
                               MOE_EXPERTS - 1)
    n_used = (cum_padded[-1:] // MOE_ROWS).astype(jnp.int32)

    spare = jnp.arange(n_blocks - MOE_EXPERTS, n_blocks, dtype=jnp.int32)
    zero_blocks = jnp.concatenate([jnp.where(padded > 0, cum_padded // MOE_ROWS - 1, -1).astype(jnp.int32),
                                   jnp.where(spare >= n_used[0], spare, -1)])
    xs = _dispatch(h, dest, zero_blocks, n_blocks * MOE_ROWS)
    ys = _moe_mlp(xs, block_expert, n_used, w_gate, w_up, w_down, layer)
    return _combine(x, ys, route, dest)


def _even_layer(x, bsz, t_len, norm_g, w_in, prm, v_first, router):
    r, lw, k, v, a, b, g, pool = _even_prep(x.reshape(bsz, t_len, -1), norm_g, w_in.astype(BF16), prm, v_first)
    y = _rwkv_chunk(r, lw, k, v, a, b)
    return _even_out(x, y, r, k, v, g, pool, prm, router), v


def _odd_layer(x, bsz, t_len, layer_idx, norm_g, w_in, prm, router):
    q, k, rest = _qk_prep(x.reshape(bsz, t_len, -1), norm_g, w_in.astype(BF16), prm["q_norm"], prm["k_norm"])
    lam_init = 0.8 - 0.6 * math.exp(-0.3 * layer_idx)
    lam = (jnp.exp(jnp.sum(prm["lam_q1"] * prm["lam_k1"])) - jnp.exp(jnp.sum(prm["lam_q2"] * prm["lam_k2"]))
           + lam_init)
    o_attn = _diff_attn(q, k, rest, lam, prm["subln"], 1.0 - lam_init)
    ng = bsz // S5_BATCH_GROUP
    u5 = rest[:, :, RW_WIDTH:].reshape(ng, S5_BATCH_GROUP, t_len, RW_WIDTH).transpose(0, 2, 1, 3)
    o_ssm = _s5(u5, prm).transpose(0, 2, 1, 3).reshape(bsz, t_len, RW_WIDTH)
    return _odd_out(x, o_attn, o_ssm, prm["w_out"], router)


def _s5_params(a_re, a_im, log_step, b_re, b_im, c_re, c_im, d_skip, w_glu):
    lam = lax.complex(jnp.minimum(a_re, -1e-4), a_im)
    lam_bar = jnp.exp(lam * jnp.exp(log_step))
    b_bar = ((lam_bar - 1.0) / lam)[..., None] * lax.complex(b_re, b_im)
    gpt = LANES // S5_GROUP_DIM
    eye = jnp.eye(gpt, dtype=F32)

    def in_map(part):
        p = part.reshape(S5_TILES, gpt, S5_STATE, S5_GROUP_DIM)
        return jnp.einsum("jgpc,gh->jgchp", p, eye).reshape(S5_TILES, LANES, gpt * S5_STATE)

    def out_map(part):
        p = part.reshape(S5_TILES, gpt, S5_GROUP_DIM, S5_STATE)
        return jnp.einsum("jgcp,gh->jgphc", p, eye).reshape(S5_TILES, gpt * S5_STATE, LANES)

    bt = jnp.concatenate([in_map(jnp.real(b_bar)), in_map(jnp.imag(b_bar))], axis=2).astype(BF16)
    ct = jnp.concatenate([out_map(c_re), -out_map(c_im)], axis=1).astype(BF16)
    return {"bt": bt, "ct": ct, "lam_re": jnp.real(lam_bar).reshape(1, -1), "lam_im": jnp.imag(lam_bar).reshape(1, -1),
            "d": d_skip, "w_glu": w_glu.astype(BF16)}


def _block_diag(blocks):
    g, r, c = blocks.shape
    return jnp.einsum("grc,gh->grhc", blocks, jnp.eye(g, dtype=blocks.dtype)).reshape(g * r, g * c)


def kernel(x, norm_mix_g, norm_ffn_g,
           even_w_in, rw_mu, rw_w0, rw_w2, rw_a0, rw_a2, rw_g2, rw_k_k, rw_k_a, rw_r_k,
           rw_ln_g, rw_ln_b, rw_v0, rw_v1, rw_v2, pool_w, pool_scale, even_w_out,
           odd_w_in, da_q_norm, da_k_norm, da_lam_q1, da_lam_k1, da_lam_q2, da_lam_k2, da_subln,
           s5_a_re, s5_a_im, s5_log_step, s5_b_re, s5_b_im, s5_c_re, s5_c_im, s5_d, s5_w_glu,
           odd_w_out,
           moe_w_group, moe_b_group, moe_w_expert, moe_b_expert, moe_w_gate, moe_w_up, moe_w_down):
    bsz, t_len, d = x.shape
    depth = norm_mix_g.shape[0]
    xf = x.reshape(bsz * t_len, d)
    v_first = None
    for layer in range(depth):
        i = layer // 2
        router = _router_operands(norm_ffn_g[layer], moe_w_group[layer], moe_b_group[layer], moe_w_expert[layer],
                                  moe_b_expert[layer])
        if layer % 2 == 0:
            rank = rw_w2.shape[1]
            wa = jnp.zeros((LANES, 2 * RW_WIDTH), F32)
            wa = wa.at[:rank, :RW_WIDTH].set(rw_w2[i]).at[rank:, RW_WIDTH:].set(rw_a2[i])
            prm = {"mu": rw_mu[i], "w0": rw_w0[i], "a0": rw_a0[i], "wa": wa.astype(BF16), "g2": rw_g2[i].astype(BF16),
                   "k_k": rw_k_k[i], "k_a": rw_k_a[i], "r_k": rw_r_k[i], "ln_g": rw_ln_g[i], "ln_b": rw_ln_b[i],
                   "pool_bd": _block_diag(pool_w[i]).astype(BF16), "pool_scale": pool_scale[i],
                   "w_out": even_w_out[i].astype(BF16)}
            if v_first is not None:
                vr = rw_v1.shape[2]
                prm["v0"] = rw_v0[i - 1]
                prm["v1"] = jnp.zeros((RW_WIDTH, LANES), F32).at[:, :vr].set(rw_v1[i - 1]).astype(BF16)
                prm["v2"] = jnp.zeros((LANES, RW_WIDTH), F32).at[:vr, :].set(rw_v2[i - 1]).astype(BF16)
            routed, v_new = _even_layer(xf, bsz, t_len, norm_mix_g[layer], even_w_in[i], prm, v_first, router)
            if v_first is None:
                v_first = v_new
        else:
            prm = _s5_params(s5_a_re[i], s5_a_im[i], s5_log_step[i], s5_b_re[i], s5_b_im[i], s5_c_re[i],
                             s5_c_im[i], s5_d[i].reshape(-1), s5_w_glu[i])
            prm.update({"q_norm": da_q_norm[i], "k_norm": da_k_norm[i], "lam_q1": da_lam_q1[i],
                        "lam_k1": da_lam_k1[i], "lam_q2": da_lam_q2[i], "lam_k2": da_lam_k2[i],
                        "subln": da_subln[i], "w_out": odd_w_out[i].astype(BF16)})
            routed = _odd_layer(xf, bsz, t_len, layer, norm_mix_g[layer], odd_w_in[i], prm, router)
        xf = _hier_moe(*routed, moe_w_gate, moe_w_up, moe_w_down, layer)
    return xf.reshape(bsz, t_len, d)
```

```python
import functools
import math

import jax
import jax.numpy as jnp
from jax import lax
from jax.experimental import pallas as pl
from jax.experimental.pallas import tpu as pltpu

F32 = jnp.float32
BF16 = jnp.bfloat16

LANES = 128
SUBLANES = 8
VMEM_BYTES_V7X = 64 * 1024 * 1024

D_MODEL = 1024
HEAD = 64
RW_WIDTH = 512
RW_SHIFT_COLS = 3 * RW_WIDTH + 64 + 64 + 128
RW_LN_EPS = 64e-5
POOL_WINDOWS = (2, 4, 8, 16)
POOL_HALO = 16
DA_HEADS = 4
SUBLN_EPS = 1e-5
ROPE_THETA = 10000.0
S5_GROUP_DIM = 16
S5_STATE = 64
S5_TILES = 4
S5_BATCH_GROUP = SUBLANES
MOE_GROUPS = 4
MOE_PER_GROUP = 8
MOE_EXPERTS = 32
MOE_ROWS = 512
RMS_EPS = 1e-6
CHUNK = 64
NEG_BIG = -1e30


def _params(semantics, vmem_mib):
    return pltpu.CompilerParams(dimension_semantics=semantics,
                                vmem_limit_bytes=min(vmem_mib * 1024 * 1024, VMEM_BYTES_V7X - 8 * 1024 * 1024))


def _bdot(a, b):
    return jnp.dot(a.astype(BF16), b.astype(BF16), preferred_element_type=F32)


def _bdot_nt(a, b):
    return lax.dot_general(a.astype(BF16), b.astype(BF16), (((1,), (1,)), ((), ())),
                           preferred_element_type=F32)


def _split_dot(x, m_bf16):
    hi = x.astype(BF16)
    lo = (x - hi.astype(F32)).astype(BF16)
    return (jnp.dot(hi, m_bf16, preferred_element_type=F32)
            + jnp.dot(lo, m_bf16, preferred_element_type=F32))


def _seg_ones(width):
    r = lax.broadcasted_iota(jnp.int32, (LANES, LANES), 0)
    c = lax.broadcasted_iota(jnp.int32, (LANES, LANES), 1)
    sh = int(math.log2(width))
    return ((r >> sh) == (c >> sh)).astype(BF16)


def _segsum(x, seg):
    tiles = [_split_dot(x[:, j * LANES:(j + 1) * LANES], seg) for j in range(x.shape[1] // LANES)]
    return tiles[0] if len(tiles) == 1 else jnp.concatenate(tiles, axis=1)


def _sigmoid(x):
    return 1.0 / (1.0 + jnp.exp(-x))


def _even_prep_kernel(has_vres, tb, *refs):
    if has_vres:
        (x_ref, gn_ref, win_ref, mu_ref, w0_ref, a0_ref, wa_ref, g2_ref, kk_ref, ka_ref, pw_ref, ps_ref,
         vf_ref, v0_ref, v1_ref, v2_ref,
         r_o, lw_o, k_o, v_o, a_o, b_o, g_o, pool_o, carry) = refs
    else:
        (x_ref, gn_ref, win_ref, mu_ref, w0_ref, a0_ref, wa_ref, g2_ref, kk_ref, ka_ref, pw_ref, ps_ref,
         r_o, lw_o, k_o, v_o, a_o, b_o, g_o, pool_o, carry) = refs
    ti = pl.program_id(1)

    @pl.when(ti == 0)
    def _():
        carry[...] = jnp.zeros_like(carry)

    x = x_ref[0]
    h = x * lax.rsqrt(jnp.mean(x * x, axis=-1, keepdims=True) + RMS_EPS) * gn_ref[...]
    u = jnp.dot(h.astype(BF16), win_ref[...], preferred_element_type=F32)
    ext = jnp.concatenate([carry[...], u], axis=0)
    carry[...] = u[tb - POOL_HALO:, :]

    p1 = ext[:, RW_SHIFT_COLS:]
    p2 = p1 + pltpu.roll(p1, 1, 0)
    p4 = p2 + pltpu.roll(p2, 2, 0)
    p8 = p4 + pltpu.roll(p4, 4, 0)
    p16 = p8 + pltpu.roll(p8, 8, 0)
    lane = lax.broadcasted_iota(jnp.int32, (1, RW_WIDTH), 1)
    grp = lane >> 7
    sums = jnp.where(grp == 0, p2, jnp.where(grp == 1, p4, jnp.where(grp == 2, p8, p16)))[POOL_HALO:]
    win = jnp.where(grp == 0, 2.0, jnp.where(grp == 1, 4.0, jnp.where(grp == 2, 8.0, 16.0)))
    n_seen = (ti * tb + lax.broadcasted_iota(jnp.int32, (tb, 1), 0) + 1).astype(F32)
    d = sums / jnp.minimum(n_seen, win) - u[:, RW_SHIFT_COLS:]
    pool_o[0] = (_bdot(d, pw_ref[...]) * ps_ref[...]).astype(pool_o.dtype)

    u_rw = u[:, :RW_SHIFT_COLS]
    prev = pltpu.roll(ext[:, :RW_SHIFT_COLS], 1, 0)[POOL_HALO:]
    m = u_rw + (prev - u_rw) * mu_ref[...]
    r = m[:, :RW_WIDTH]
    k = m[:, RW_WIDTH:2 * RW_WIDTH]
    v = m[:, 2 * RW_WIDTH:3 * RW_WIDTH]
    dwa = m[:, 3 * RW_WIDTH:3 * RW_WIDTH + LANES]
    dg = m[:, 3 * RW_WIDTH + LANES:]
    l128 = lax.broadcasted_iota(jnp.int32, (1, LANES), 1)
    dwa = jnp.where(l128 < HEAD, jnp.tanh(dwa), dwa)
    x12 = _bdot(dwa, wa_ref[...])
    z = -(w0_ref[...] + x12[:, :RW_WIDTH])
    softplus = jnp.maximum(z, 0.0) + jnp.log(1.0 + jnp.exp(-jnp.abs(z)))
    lw = -jnp.exp(-softplus - 0.5)
    a_i = _sigmoid(a0_ref[...] + x12[:, RW_WIDTH:])
    g_o[0] = _bdot(_sigmoid(dg), g2_ref[...]).astype(g_o.dtype)
    if has_vres:
        gate_v = _sigmoid(v0_ref[...] + _bdot(_bdot(v, v1_ref[...]), v2_ref[...]))
        v = v + (vf_ref[0] - v) * gate_v
    kk = k * kk_ref[...]
    ss = _segsum(kk * kk, _seg_ones(HEAD))
    kk = kk / jnp.maximum(jnp.sqrt(ss), 1e-12)
    r_o[0] = r.astype(r_o.dtype)
    lw_o[0] = lw
    k_o[0] = (k * (1.0 + (a_i - 1.0) * ka_ref[...])).astype(k_o.dtype)
    v_o[0] = v.astype(v_o.dtype)
    a_o[0] = (-kk).astype(a_o.dtype)
    b_o[0] = (kk * a_i).astype(b_o.dtype)


def _even_prep(x, norm_g, w_in, prm, v_first, tb=512):
    bsz, t_len, d = x.shape
    cin = w_in.shape[1]
    tb = min(tb, t_len)
    has_vres = v_first is not None
    row = lambda a: a.reshape(1, -1)
    full = lambda shape: pl.BlockSpec(shape, lambda b, t: (0,) * len(shape))
    seq = pl.BlockSpec((1, tb, RW_WIDTH), lambda b, t: (b, t, 0))
    ins = [x, row(norm_g), w_in, row(prm["mu"]), row(prm["w0"]), row(prm["a0"]), prm["wa"], prm["g2"], row(prm["k_k"]),
           row(prm["k_a"]), prm["pool_bd"], row(prm["pool_scale"])]
    specs = [pl.BlockSpec((1, tb, d), lambda b, t: (b, t, 0)), full((1, d)), full((d, cin)), full((1, RW_SHIFT_COLS)),
             full((1, RW_WIDTH)), full((1, RW_WIDTH)), full((LANES, 2 * RW_WIDTH)),
             full((LANES, RW_WIDTH)), full((1, RW_WIDTH)), full((1, RW_WIDTH)),
             full((RW_WIDTH, RW_WIDTH)), full((1, RW_WIDTH))]
    if has_vres:
        ins += [v_first, row(prm["v0"]), prm["v1"], prm["v2"]]
        specs += [seq, full((1, RW_WIDTH)), full((RW_WIDTH, LANES)), full((LANES, RW_WIDTH))]
    out = lambda dt: jax.ShapeDtypeStruct((bsz, t_len, RW_WIDTH), dt)
    return pl.pallas_call(
        functools.partial(_even_prep_kernel, has_vres, tb),
        out_shape=(out(BF16), out(F32)) + (out(BF16),) * 6,
        grid=(bsz, t_len // tb),
        in_specs=specs,
        out_specs=(seq,) * 8,
        scratch_shapes=[pltpu.VMEM((POOL_HALO, cin), F32)],
        compiler_params=_params(("parallel", "arbitrary"), 56),
        name="even_prep",
    )(*ins)


def _rwkv_chunk_kernel(nb, r_ref, lw_ref, k_ref, v_ref, a_ref, b_ref, y_ref, s_ref):
    ci = pl.program_id(1)

    @pl.when(ci == 0)
    def _():
        s_ref[...] = jnp.zeros_like(s_ref)

    L = CHUNK
    n_pair = RW_WIDTH // LANES
    tri = (lax.broadcasted_iota(jnp.int32, (L, L), 0) >= lax.broadcasted_iota(jnp.int32, (L, L), 1)).astype(BF16)
    lane = lax.broadcasted_iota(jnp.int32, (1, LANES), 1)
    m_a = lane < HEAD
    t_idx = lax.broadcasted_iota(jnp.int32, (L, LANES), 0)
    s_idx = lax.broadcasted_iota(jnp.int32, (L, LANES), 1) & (HEAD - 1)
    strict = t_idx > s_idx
    incl = t_idx >= s_idx
    r128 = lax.broadcasted_iota(jnp.int32, (LANES, LANES), 0)
    c128 = lax.broadcasted_iota(jnp.int32, (LANES, LANES), 1)
    eye = (r128 == c128).astype(F32)
    same_head = (r128 >> 6) == (c128 >> 6)

    def only_a(x):
        return jnp.where(m_a, x, jnp.zeros_like(x))

    def only_b(x):
        return jnp.where(m_a, jnp.zeros_like(x), x)

    def stack(x):
        return jnp.concatenate([only_a(x), only_b(x)], axis=0)

    def stack_sw(x):
        return jnp.concatenate([only_b(x), only_a(x)], axis=0)

    def mm(x, y):
        return jnp.dot(x, y, preferred_element_type=F32)

    def mm_nt(x, y):
        return lax.dot_general(x, y, (((1,), (1,)), ((), ())), preferred_element_type=F32)

    chains = []
    for bi in range(nb):
        lw = lw_ref[bi]
        c = _split_dot_lhs(tri, lw)
        e_pos = jnp.exp(c)
        e_neg = jnp.exp(-c)
        a_all = (jnp.exp(c - lw) * a_ref[bi]).astype(BF16)
        b_all = (b_ref[bi] * e_neg).astype(BF16)
        k_all = (k_ref[bi] * e_neg).astype(BF16)
        r_all = r_ref[bi] * e_pos
        v_all = v_ref[bi].astype(BF16)
        for p in range(n_pair):
            sl = slice(p * LANES, (p + 1) * LANES)
            chains.append({"bi": bi, "p": p, "sl": sl, "at": a_all[:, sl], "bt": b_all[:, sl], "kt": k_all[:, sl],
                           "rt": r_all[:, sl], "vv": v_all[:, sl], "gl": e_pos[L - 1:L, sl]})

    for ch in chains:
        rt16 = ch["rt"].astype(BF16)
        bk = jnp.concatenate([ch["bt"], ch["kt"]], axis=0)
        kb = jnp.concatenate([ch["kt"], ch["bt"]], axis=0)
        s_a = mm_nt(jnp.concatenate([only_a(ch["at"]), only_a(rt16)], axis=0), bk)
        s_b = mm_nt(jnp.concatenate([only_b(ch["at"]), only_b(rt16)], axis=0), kb)
        m_ha = jnp.where(strict, s_a[:L], 0.0)
        n_ha = jnp.where(incl, s_a[L:], 0.0)
        m_hb = jnp.where(strict, s_b[:L], 0.0)
        n_hb = jnp.where(incl, s_b[L:], 0.0)
        ch["bk"] = bk
        ch["bdm"] = jnp.concatenate([only_a(m_ha), only_b(m_hb)], axis=0)
        ch["ak_sw"] = jnp.where(m_a, m_hb, m_ha).astype(BF16)
        ch["n_lhs"] = jnp.concatenate([jnp.where(m_a, n_ha, n_hb),
                                       jnp.where(m_a, n_hb, n_ha)], axis=1).astype(BF16)
        ch["t_inv"] = eye + ch["bdm"]
        ch["m_pow"] = ch["bdm"].astype(BF16)
    for ch in chains:
        ch["w"] = mm(ch["ak_sw"], stack_sw(ch["vv"])).astype(BF16)
    for _ in range(int(math.log2(L)) - 1):
        for ch in chains:
            ch["m_pow"] = mm(ch["m_pow"], ch["m_pow"]).astype(BF16)
        for ch in chains:
            ch["t_inv"] = ch["t_inv"] + mm(ch["t_inv"].astype(BF16), ch["m_pow"])
    for ch in chains:
        t_pk = (ch["t_inv"][:L] + ch["t_inv"][L:]).astype(BF16)
        au = mm(t_pk, jnp.concatenate([stack(ch["at"]), stack(ch["w"])], axis=1))
        ch["a_hat"] = au[:, :LANES].astype(BF16)
        ch["u_hat"] = au[:, LANES:].astype(BF16)
    for ch in chains:
        rhs = jnp.concatenate([
            jnp.concatenate([stack(ch["u_hat"]), stack(ch["a_hat"])], axis=1),
            jnp.concatenate([stack_sw(ch["vv"]), jnp.zeros((LANES, LANES), BF16)], axis=1)], axis=0)
        yr = mm(ch["n_lhs"], rhs)
        ch["y_hat"] = yr[:, :LANES]
        ch["r_hat"] = (ch["rt"] + yr[:, LANES:]).astype(BF16)
        g_p = mm(ch["a_hat"].astype(F32).T.astype(BF16), ch["bt"])
        g_q = mm(jnp.concatenate([ch["u_hat"], ch["vv"]], axis=0).astype(F32).T.astype(BF16), ch["bk"])
        ch["p_bd"] = ((eye + jnp.where(same_head, g_p, 0.0)) * ch["gl"]).astype(BF16)
        ch["q_pk"] = jnp.where(m_a, g_q[:L], g_q[L:]) * ch["gl"]
    for ch in chains:
        s0 = s_ref[ch["bi"], ch["p"]]
        s16 = s0.astype(BF16)
        y_ref[ch["bi"], :, ch["sl"]] = mm_nt(ch["r_hat"], stack(s16)) + ch["y_hat"]
        s_ref[ch["bi"], ch["p"]] = mm(s16, ch["p_bd"]) + ch["q_pk"]


def _split_dot_lhs(m_bf16, x):
    hi = x.astype(BF16)
    lo = (x - hi.astype(F32)).astype(BF16)
    return (jnp.dot(m_bf16, hi, preferred_element_type=F32)
            + jnp.dot(m_bf16, lo, preferred_element_type=F32))


def _rwkv_chunk(r, lw, k, v, a, b, nb=4):
    bsz, t_len, _ = r.shape
    seq = pl.BlockSpec((nb, CHUNK, RW_WIDTH), lambda bi, ci: (bi, ci, 0))
    return pl.pallas_call(
        functools.partial(_rwkv_chunk_kernel, nb),
        out_shape=jax.ShapeDtypeStruct((bsz, t_len, RW_WIDTH), F32),
        grid=(bsz // nb, t_len // CHUNK),
        in_specs=[seq] * 6,
        out_specs=seq,
        scratch_shapes=[pltpu.VMEM((nb, RW_WIDTH // LANES, HEAD, LANES), F32)],
        compiler_params=_params(("parallel", "arbitrary"), 32),
        name="rwkv_chunk",
    )(r, lw, k, v, a, b)


def _even_out_kernel(tm, x_ref, y_ref, r_ref, k_ref, v_ref, g_ref, p_ref, lng_ref, lnb_ref, rk_ref, wo_ref,
                     gf_ref, whi_ref, wlo_ref, rb_ref, o_ref, h_o, route_o, cnt_o, run):
    seg = _seg_ones(HEAD)
    y = y_ref[...]
    mean = _segsum(y, seg) * (1.0 / HEAD)
    yc = y - mean
    var = _segsum(yc * yc, seg) * (1.0 / HEAD)
    yn = yc * lax.rsqrt(var + RW_LN_EPS) * lng_ref[...] + lnb_ref[...]
    bonus = _segsum(r_ref[...].astype(F32) * k_ref[...].astype(F32) * rk_ref[...], seg) * v_ref[...].astype(F32)
    o_rw = (yn + bonus) * g_ref[...].astype(F32)
    cat = jnp.concatenate([o_rw.astype(BF16), p_ref[...]], axis=1)
    x_new = x_ref[...] + jnp.dot(cat, wo_ref[...], preferred_element_type=F32)
    o_ref[...] = x_new
    _route_block(tm, x_new, gf_ref, whi_ref, wlo_ref, rb_ref, h_o, route_o, cnt_o, run)


def _even_out(x, y, r, k, v, g, pool, prm, router, tm=512):
    n, d = x.shape
    tm = min(tm, n)
    half = pl.BlockSpec((tm, RW_WIDTH), lambda i: (i, 0))
    vec = pl.BlockSpec((1, RW_WIDTH), lambda i: (0, 0))
    fl = lambda a: a.reshape(n, RW_WIDTH)
    r_in, r_shape, r_out, r_scratch = _router_specs(tm, d)
    return pl.pallas_call(
        functools.partial(_even_out_kernel, tm),
        out_shape=(jax.ShapeDtypeStruct((n, d), F32),) + r_shape(n),
        grid=(n // tm,),
        in_specs=[pl.BlockSpec((tm, d), lambda i: (i, 0))] + [half] * 6 + [vec] * 3
                 + [pl.BlockSpec((d, d), lambda i: (0, 0))] + r_in,
        out_specs=(pl.BlockSpec((tm, d), lambda i: (i, 0)),) + r_out,
        scratch_shapes=[r_scratch],
        compiler_params=_params(("arbitrary",), 48),
        name="even_out",
    )(x, fl(y), fl(r), fl(k), fl(v), fl(g), fl(pool), prm["ln_g"].reshape(1, -1), prm["ln_b"].reshape(1, -1),
      prm["r_k"].reshape(1, -1), prm["w_out"], *router)


def _qk_prep_kernel(x_ref, gn_ref, win_ref, gq_ref, gk_ref, cos_ref, sin_ref, q_o, k_o, rest_o):
    xin = x_ref[0]
    h = xin * lax.rsqrt(jnp.mean(xin * xin, axis=-1, keepdims=True) + RMS_EPS) * gn_ref[...]
    u = jnp.dot(h.astype(BF16), win_ref[...], preferred_element_type=F32)
    rest_o[0] = u[:, 2 * RW_WIDTH:]
    x = u[:, :2 * RW_WIDTH]
    cos = cos_ref[...]
    sin = sin_ref[...]
    inv_rms = lax.rsqrt(_segsum(x * x, _seg_ones(HEAD)) * (1.0 / HEAD) + RMS_EPS)
    lane = lax.broadcasted_iota(jnp.int32, (1, LANES), 1)
    first = (lane & (HEAD - 1)) < HEAD // 2
    nq = RW_WIDTH // LANES
    for j in range(2 * nq):
        tile = slice(j * LANES, (j + 1) * LANES)
        gain = gq_ref[...] if j < nq else gk_ref[...]
        xn = x[:, tile] * inv_rms[:, tile] * gain
        partner = jnp.where(first, pltpu.roll(xn, LANES - HEAD // 2, 1), pltpu.roll(xn, HEAD // 2, 1))
        out = xn * cos + partner * sin
        if j < nq:
            q_o[0, :, j * LANES:(j + 1) * LANES] = (out * (HEAD ** -0.5)).astype(BF16)
        else:
            k_o[0, :, (j - nq) * LANES:(j - nq + 1) * LANES] = out.astype(BF16)


def _qk_prep(x, norm_g, w_in, q_norm, k_norm, tb=512):
    bsz, t_len, d = x.shape
    cin = w_in.shape[1]
    tb = min(tb, t_len)
    inv_freq = ROPE_THETA ** (-jnp.arange(0, HEAD, 2, dtype=F32) / HEAD)
    ang = jnp.arange(t_len, dtype=F32)[:, None] * inv_freq[None, :]
    cos = jnp.tile(jnp.cos(ang), (1, LANES // (HEAD // 2)))
    sin_half = jnp.concatenate([-jnp.sin(ang), jnp.sin(ang)], axis=1)
    sin = jnp.tile(sin_half, (1, LANES // HEAD))
    tile2 = lambda g: jnp.tile(g, LANES // HEAD).reshape(1, LANES)
    out = jax.ShapeDtypeStruct((bsz, t_len, RW_WIDTH), BF16)
    tab = pl.BlockSpec((tb, LANES), lambda b, t: (t, 0))
    vec = pl.BlockSpec((1, LANES), lambda b, t: (0, 0))
    seq = pl.BlockSpec((1, tb, RW_WIDTH), lambda b, t: (b, t, 0))
    const = lambda shape: pl.BlockSpec(shape, lambda b, t: (0,) * len(shape))
    return pl.pallas_call(
        _qk_prep_kernel,
        out_shape=(out, out, jax.ShapeDtypeStruct((bsz, t_len, cin - 2 * RW_WIDTH), F32)),
        grid=(bsz, t_len // tb),
        in_specs=[pl.BlockSpec((1, tb, d), lambda b, t: (b, t, 0)), const((1, d)), const((d, cin)), vec, vec, tab, tab],
        out_specs=(seq, seq, pl.BlockSpec((1, tb, cin - 2 * RW_WIDTH), lambda b, t: (b, t, 0))),
        compiler_params=_params(("parallel", "parallel"), 48),
        name="qk_prep",
    )(x, norm_g.reshape(1, d), w_in, tile2(q_norm), tile2(k_norm), cos, sin)


def _diff_attn_kernel(tq, nh, out_scale, q_ref, k_ref, v_ref, lam_ref, sub_ref, o_ref):
    qi = pl.program_id(2)
    lane = lax.broadcasted_iota(jnp.int32, (1, LANES), 1)
    m_a = lane < HEAD

    def prefix(n_blk):
        kl = n_blk * tq
        on_diag = lax.broadcasted_iota(jnp.int32, (tq, tq), 0) >= lax.broadcasted_iota(jnp.int32, (tq, tq), 1)
        qs, ks, vs = [], [], []
        for h in range(nh):
            sl = slice(h * LANES, (h + 1) * LANES)
            q = q_ref[0, :, sl]
            zero = jnp.zeros_like(q)
            qs += [jnp.where(m_a, q, zero), jnp.where(m_a, zero, q)]
            ks += [k_ref[0, :kl, sl]] * 2
            vs += [jnp.concatenate([v_ref[0, :kl, sl].astype(BF16), jnp.ones((kl, LANES), BF16)], axis=1)] * 2
        s = [lax.dot_general(q, k, (((1,), (1,)), ((), ())), preferred_element_type=F32) for q, k in zip(qs, ks)]
        last = [jnp.where(on_diag, x[:, kl - tq:], NEG_BIG) for x in s]
        if n_blk > 1:
            last = [jnp.concatenate([x[:, :kl - tq], y], axis=1) for x, y in zip(s, last)]
        top = [jnp.max(x, axis=-1, keepdims=True) for x in last]
        e = [jnp.exp((x - m).astype(BF16)) for x, m in zip(last, top)]
        pv = [jnp.dot(x, v, preferred_element_type=F32) for x, v in zip(e, vs)]
        pv = [x[:, :LANES] / x[:, LANES:] for x in pv]
        for h in range(nh):
            sl = slice(h * LANES, (h + 1) * LANES)
            o = pv[2 * h] - lam_ref[...] * pv[2 * h + 1]
            ms = jnp.mean(o * o, axis=-1, keepdims=True)
            o_ref[0, :, sl] = o * lax.rsqrt(ms + SUBLN_EPS) * sub_ref[...] * out_scale

    for blk in range(k_ref.shape[1] // tq):
        pl.when(qi == blk)(functools.partial(prefix, blk + 1))


def _diff_attn(q, k, v_src, lam, subln, out_scale, tq=256, nh=4):
    bsz, t_len, _ = q.shape
    tq = min(tq, t_len)
    wide = nh * LANES
    return pl.pallas_call(
        functools.partial(_diff_attn_kernel, tq, nh, out_scale),
        out_shape=jax.ShapeDtypeStruct((bsz, t_len, RW_WIDTH), F32),
        grid=(bsz, DA_HEADS // nh, t_len // tq),
        in_specs=[pl.BlockSpec((1, tq, wide), lambda b, h, i: (b, i, h)),
                  pl.BlockSpec((1, t_len, wide), lambda b, h, i: (b, 0, h)),
                  pl.BlockSpec((1, t_len, wide), lambda b, h, i: (b, 0, h)),
                  pl.BlockSpec((1, 1), lambda b, h, i: (0, 0)),
                  pl.BlockSpec((1, LANES), lambda b, h, i: (0, 0))],
        out_specs=pl.BlockSpec((1, tq, wide), lambda b, h, i: (b, i, h)),
        compiler_params=_params(("parallel", "parallel", "arbitrary"), 48),
        name="diff_attn",
    )(q, k, v_src, lam.reshape(1, 1), subln.reshape(1, LANES))


def _s5_kernel(tb, u_ref, bt_ref, lre_ref, lim_ref, ct_ref, d_ref, wg_ref, o_ref, xs, st):
    ti = pl.program_id(1)

    @pl.when(ti == 0)
    def _():
        st[...] = jnp.zeros_like(st)

    rows = tb * SUBLANES
    half = RW_WIDTH
    u = u_ref[0].reshape(rows, RW_WIDTH)
    for j in range(S5_TILES):
        xs[:, 2 * half * j:2 * half * (j + 1)] = _bdot(u[:, j * LANES:(j + 1) * LANES], bt_ref[j])
    for j in range(S5_TILES):
        re = slice(2 * half * j, 2 * half * j + half)
        im = slice(2 * half * j + half, 2 * half * (j + 1))
        lr = jnp.broadcast_to(lre_ref[:, half * j:half * (j + 1)], (SUBLANES, half))
        li = jnp.broadcast_to(lim_ref[:, half * j:half * (j + 1)], (SUBLANES, half))

        def step(t, carry, re=re, im=im, lr=lr, li=li):
            xr, xi = carry
            row = pl.multiple_of(t * SUBLANES, SUBLANES)
            nr = lr * xr - li * xi + xs[pl.ds(row, SUBLANES), re]
            ni = lr * xi + li * xr + xs[pl.ds(row, SUBLANES), im]
            xs[pl.ds(row, SUBLANES), re] = nr
            xs[pl.ds(row, SUBLANES), im] = ni
            return nr, ni

        xr, xi = lax.fori_loop(0, tb, step, (st[:, re], st[:, im]), unroll=4)
        st[:, re] = xr
        st[:, im] = xi
    y = jnp.concatenate([_bdot(xs[:, 2 * half * j:2 * half * (j + 1)], ct_ref[j]) for j in range(S5_TILES)], axis=1)
    y = y + d_ref[...] * u
    z = 0.5 * y * (1.0 + jnp.tanh(math.sqrt(2.0 / math.pi) * (y + 0.044715 * (y * y * y))))
    o = z * _sigmoid(_bdot(z, wg_ref[...]))
    o_ref[0] = o.reshape(tb, SUBLANES, RW_WIDTH)


def _s5(u5, prm, tb=128):
    ng, t_len, _, _ = u5.shape
    tb = min(tb, t_len)
    full = lambda shape: pl.BlockSpec(shape, lambda g, t: (0,) * len(shape))
    n_state = 2 * RW_WIDTH * S5_TILES
    return pl.pallas_call(
        functools.partial(_s5_kernel, tb),
        out_shape=jax.ShapeDtypeStruct(u5.shape, F32),
        grid=(ng, t_len // tb),
        in_specs=[pl.BlockSpec((1, tb, SUBLANES, RW_WIDTH), lambda g, t: (g, t, 0, 0)),
                  full((S5_TILES, LANES, 2 * RW_WIDTH)), full((1, n_state // 2)), full((1, n_state // 2)),
                  full((S5_TILES, 2 * RW_WIDTH, LANES)), full((1, RW_WIDTH)), full((RW_WIDTH, RW_WIDTH))],
        out_specs=pl.BlockSpec((1, tb, SUBLANES, RW_WIDTH), lambda g, t: (g, t, 0, 0)),
        scratch_shapes=[pltpu.VMEM((tb * SUBLANES, n_state), F32), pltpu.VMEM((SUBLANES, n_state), F32)],
        compiler_params=_params(("parallel", "arbitrary"), 48),
        name="s5_scan",
    )(u5, prm["bt"], prm["lam_re"], prm["lam_im"], prm["ct"], prm["d"].reshape(1, -1), prm["w_glu"])


def _odd_out_kernel(tm, x_ref, a_ref, s_ref, wo_ref, gf_ref, whi_ref, wlo_ref, rb_ref, o_ref, h_o, route_o, cnt_o, run):
    cat = jnp.concatenate([a_ref[...], s_ref[...]], axis=1)
    x_new = x_ref[...] + _bdot(cat, wo_ref[...])
    o_ref[...] = x_new
    _route_block(tm, x_new, gf_ref, whi_ref, wlo_ref, rb_ref, h_o, route_o, cnt_o, run)


def _odd_out(x, o_attn, o_ssm, w_out, router, tm=512):
    n, d = x.shape
    tm = min(tm, n)
    half = pl.BlockSpec((tm, RW_WIDTH), lambda i: (i, 0))
    r_in, r_shape, r_out, r_scratch = _router_specs(tm, d)
    return pl.pallas_call(
        functools.partial(_odd_out_kernel, tm),
        out_shape=(jax.ShapeDtypeStruct((n, d), F32),) + r_shape(n),
        grid=(n // tm,),
        in_specs=[pl.BlockSpec((tm, d), lambda i: (i, 0)), half, half, pl.BlockSpec((d, d), lambda i: (0, 0))] + r_in,
        out_specs=(pl.BlockSpec((tm, d), lambda i: (i, 0)),) + r_out,
        scratch_shapes=[r_scratch],
        compiler_params=_params(("arbitrary",), 40),
        name="odd_out",
    )(x, o_attn.reshape(n, RW_WIDTH), o_ssm.reshape(n, RW_WIDTH), w_out, *router)


TOKEN_ROWS = SUBLANES // 2
U32 = jnp.uint32


def _pack_rows(x):
    half = x.shape[1] // 2
    hi = lax.bitcast_convert_type(x[:, :half].astype(BF16).astype(F32), U32)
    lo = lax.bitcast_convert_type(x[:, half:].astype(BF16).astype(F32), U32)
    return hi | (lo >> 16)


def _unpack_words(w):
    return (lax.bitcast_convert_type(w & jnp.uint32(0xFFFF0000), F32), lax.bitcast_convert_type(w << 16, F32))


def _store_token_tiles(ref, x, n_tok):
    words = _pack_rows(x)
    for c in range(TOKEN_ROWS):
        ref[pl.ds(c, n_tok, stride=TOKEN_ROWS), :] = words[:, c * LANES:(c + 1) * LANES]


def _load_token_tiles(ref, n_tok):
    words = jnp.concatenate([ref[pl.ds(c, n_tok, stride=TOKEN_ROWS), :] for c in range(TOKEN_ROWS)], axis=1)
    return jnp.concatenate(_unpack_words(words), axis=1)


def _route_block(tm, x, g_ref, whi_ref, wlo_ref, b_ref, h_o, route_o, cnt_o, run):
    @pl.when(pl.program_id(0) == 0)
    def _():
        run[...] = jnp.zeros_like(run)

    ms = jnp.mean(x * x, axis=-1, keepdims=True)
    h = x * lax.rsqrt(ms + RMS_EPS) * g_ref[...]
    _store_token_tiles(h_o, h, tm)
    h_hi = h.astype(BF16)
    h_lo = (h - h_hi.astype(F32)).astype(BF16)
    w_hi = whi_ref[...]
    both = jnp.dot(h_hi, jnp.concatenate([w_hi, wlo_ref[...]], axis=1), preferred_element_type=F32)
    logits = both[:, :LANES] + both[:, LANES:] + jnp.dot(h_lo, w_hi, preferred_element_type=F32) + b_ref[...]
    lane = lax.broadcasted_iota(jnp.int32, logits.shape, 1).astype(F32)
    far = float(LANES)
    is_g = lane < MOE_GROUPS
    g_max = jnp.max(jnp.where(is_g, logits, NEG_BIG), axis=-1, keepdims=True)
    g_sum = jnp.sum(jnp.where(is_g, jnp.exp(jnp.minimum(logits - g_max, 0.0)), 0.0), axis=-1, keepdims=True)
    g_top = jnp.min(jnp.where(is_g & (logits == g_max), lane, far), axis=-1, keepdims=True)
    lo = MOE_GROUPS + MOE_PER_GROUP * g_top
    in_grp = (lane >= lo) & (lane < lo + MOE_PER_GROUP)
    e1 = jnp.max(jnp.where(in_grp, logits, NEG_BIG), axis=-1, keepdims=True)
    i1 = jnp.min(jnp.where(in_grp & (logits == e1), lane, far), axis=-1, keepdims=True)
    rest = in_grp & (lane != i1)
    e2 = jnp.max(jnp.where(rest, logits, NEG_BIG), axis=-1, keepdims=True)
    i2 = jnp.min(jnp.where(rest & (logits == e2), lane, far), axis=-1, keepdims=True)
    ratio = jnp.exp(e2 - e1)
    gate1 = 1.0 / (g_sum * (1.0 + ratio))
    gate2 = gate1 * ratio
    ex1 = i1 - MOE_GROUPS
    ex2 = i2 - MOE_GROUPS
    oh1 = lane == ex1
    oh2 = lane == ex2
    before = (lax.broadcasted_iota(jnp.int32, (tm, tm), 0) > lax.broadcasted_iota(jnp.int32, (tm, tm), 1)).astype(BF16)
    pre1 = jnp.dot(before, oh1.astype(BF16), preferred_element_type=F32)
    pre2 = jnp.dot(before, oh2.astype(BF16), preferred_element_type=F32)
    tot1 = jnp.sum(oh1.astype(F32), axis=0, keepdims=True)
    tot2 = jnp.sum(oh2.astype(F32), axis=0, keepdims=True)
    base = run[...]
    rank1 = jnp.sum(jnp.where(oh1, base + pre1, 0.0), axis=-1, keepdims=True)
    rank2 = jnp.sum(jnp.where(oh2, base + tot1 + pre2, 0.0), axis=-1, keepdims=True)
    run[...] = base + tot1 + tot2
    cnt_o[...] = base + tot1 + tot2
    route_o[...] = jnp.where(lane == 0, ex1, jnp.where(lane == 1, ex2, jnp.where(lane == 2, gate1, jnp.where(
        lane == 3, gate2, jnp.where(lane == 4, rank1, jnp.where(lane == 5, rank2, 0.0))))))


def _router_operands(g, w_group, b_group, w_expert, b_expert):
    d = g.shape[0]
    assert d == SUBLANES * LANES
    w_route = jnp.zeros((d, LANES), F32).at[:, :MOE_GROUPS].set(w_group)
    w_route = w_route.at[:, MOE_GROUPS:MOE_GROUPS + MOE_EXPERTS].set(w_expert)
    w_hi = w_route.astype(BF16)
    w_lo = (w_route - w_hi.astype(F32)).astype(BF16)
    bias = jnp.zeros((1, LANES), F32).at[0, :MOE_GROUPS].set(b_group)
    bias = bias.at[0, MOE_GROUPS:MOE_GROUPS + MOE_EXPERTS].set(b_expert)
    return g.reshape(1, d), w_hi, w_lo, bias


def _router_specs(tm, d):
    const = lambda shape: pl.BlockSpec(shape, lambda i: (0,) * len(shape))
    in_specs = [const((1, d)), const((d, LANES)), const((d, LANES)), const((1, LANES))]
    out_specs = (pl.BlockSpec((tm * TOKEN_ROWS, LANES), lambda i: (i, 0)), pl.BlockSpec((tm, LANES), lambda i: (i, 0)),
                 const((1, LANES)))
    out_shape = lambda n: (jax.ShapeDtypeStruct((n * TOKEN_ROWS, LANES), U32), jax.ShapeDtypeStruct((n, LANES), F32),
                           jax.ShapeDtypeStruct((1, LANES), F32))
    return in_specs, out_shape, out_specs, pltpu.VMEM((1, LANES), F32)


DMA_UNROLL = 16


def _tile_at(ref, tok):
    return ref.at[pl.ds(pl.multiple_of(tok * TOKEN_ROWS, TOKEN_ROWS), TOKEN_ROWS)]


def _dispatch_kernel(rows, dest_ref, zblk_ref, h_ref, xs_ref, zbuf, sem, zsem):
    @pl.when(pl.program_id(0) == 0)
    def _():
        zbuf[...] = jnp.zeros_like(zbuf)

        def zero_copy(j):
            start = pl.multiple_of(jnp.maximum(zblk_ref[0, j], 0) * (MOE_ROWS * TOKEN_ROWS), MOE_ROWS * TOKEN_ROWS)
            return pltpu.make_async_copy(zbuf, xs_ref.at[pl.ds(start, MOE_ROWS * TOKEN_ROWS)], zsem.at[0])

        def z_issue(j, c):
            pl.when(zblk_ref[0, j] >= 0)(lambda: zero_copy(j).start())
            return c

        def z_drain(j, c):
            pl.when(zblk_ref[0, j] >= 0)(lambda: zero_copy(j).wait())
            return c

        lax.fori_loop(0, zblk_ref.shape[1], z_issue, 0)
        lax.fori_loop(0, zblk_ref.shape[1], z_drain, 0)

    def issue(i, c):
        for choice in range(2):
            pltpu.make_async_copy(_tile_at(h_ref, i), _tile_at(xs_ref, dest_ref[0, 0, 2 * i + choice]),
                                  sem.at[choice]).start(priority=choice)
        return c

    lax.fori_loop(0, rows, issue, 0, unroll=DMA_UNROLL)
    for choice in range(2):
        pltpu.make_async_copy(h_ref, xs_ref.at[pl.ds(0, rows * TOKEN_ROWS)], sem.at[choice]).wait()


def _dispatch(h_tiles, dest, zero_blocks, n_pad, rows=512):
    n = h_tiles.shape[0] // TOKEN_ROWS
    rows = min(rows, n)
    nb = n // rows
    return pl.pallas_call(
        functools.partial(_dispatch_kernel, rows),
        out_shape=jax.ShapeDtypeStruct((n_pad * TOKEN_ROWS, LANES), h_tiles.dtype),
        grid=(nb,),
        in_specs=[pl.BlockSpec((1, 1, 2 * rows), lambda i: (i, 0, 0), memory_space=pltpu.SMEM),
                  pl.BlockSpec((1, zero_blocks.shape[0]), lambda i: (0, 0), memory_space=pltpu.SMEM),
                  pl.BlockSpec((rows * TOKEN_ROWS, LANES), lambda i: (i, 0))],
        out_specs=pl.BlockSpec(memory_space=pl.ANY),
        scratch_shapes=[pltpu.VMEM((MOE_ROWS * TOKEN_ROWS, LANES), h_tiles.dtype),
                        pltpu.SemaphoreType.DMA((2,)), pltpu.SemaphoreType.DMA((1,))],
        compiler_params=_params(("arbitrary",), 32),
        name="moe_dispatch",
    )(dest.reshape(nb, 1, 2 * rows), zero_blocks.reshape(1, -1), h_tiles)


def _moe_mlp_kernel(be_ref, used_ref, x_ref, wg_ref, wu_ref, wd_ref, o_ref, wg_s, wu_s, wd_s):
    i = pl.program_id(0)

    @pl.when(i < used_ref[0])
    def _():
        @pl.when((i == 0) | (be_ref[i] != be_ref[jnp.maximum(i - 1, 0)]))
        def _():
            wg_s[...] = wg_ref[0, 0].astype(BF16)
            wu_s[...] = wu_ref[0, 0].astype(BF16)
            wd_s[...] = wd_ref[0, 0].astype(BF16)

        x = _load_token_tiles(x_ref, MOE_ROWS).astype(BF16)
        hg = jnp.dot(x, wg_s[...], preferred_element_type=F32)
        hu = jnp.dot(x, wu_s[...], preferred_element_type=F32)
        hid = hg * _sigmoid(hg) * hu
        _store_token_tiles(o_ref, jnp.dot(hid.astype(BF16), wd_s[...], preferred_element_type=F32), MOE_ROWS)

    @pl.when(i >= used_ref[0])
    def _():
        o_ref[...] = jnp.zeros_like(o_ref)


def _moe_mlp(xs, block_expert, n_used, w_gate, w_up, w_down, layer):
    n_pad = xs.shape[0] // TOKEN_ROWS
    _, _, d, hid = w_gate.shape
    nb = n_pad // MOE_ROWS
    live = lambda i, used: jnp.minimum(i, used[0] - 1)
    tiles = pl.BlockSpec((MOE_ROWS * TOKEN_ROWS, LANES), lambda i, be, used: (live(i, used), 0))
    weight = lambda shape: pl.BlockSpec(shape, lambda i, be, used: (layer, be[live(i, used)], 0, 0))
    grid_spec = pltpu.PrefetchScalarGridSpec(
        num_scalar_prefetch=2,
        grid=(nb,),
        in_specs=[tiles, weight((1, 1, d, hid)), weight((1, 1, d, hid)), weight((1, 1, hid, d))],
        out_specs=pl.BlockSpec((MOE_ROWS * TOKEN_ROWS, LANES), lambda i, be, used: (i, 0)),
        scratch_shapes=[pltpu.VMEM((d, hid), BF16), pltpu.VMEM((d, hid), BF16), pltpu.VMEM((hid, d), BF16)],
    )
    return pl.pallas_call(
        _moe_mlp_kernel,
        out_shape=jax.ShapeDtypeStruct(xs.shape, xs.dtype),
        grid_spec=grid_spec,
        compiler_params=_params(("arbitrary",), 48),
        name="moe_mlp",
    )(block_expert, n_used, xs, w_gate, w_up, w_down)


COMBINE_AHEAD = 2


def _combine_kernel(rows, *refs):
    idx_refs = refs[:COMBINE_AHEAD + 1]
    x_ref, route_ref, ys_ref, o_ref, bufs, sem = refs[COMBINE_AHEAD + 1:]
    ring = COMBINE_AHEAD + 1
    step = pl.program_id(0)
    n_steps = pl.num_programs(0)
    cur = step % ring
    group = SUBLANES

    def issue(idx_ref, dst, first):
        for i in range(group):
            for choice in range(2):
                pltpu.make_async_copy(_tile_at(ys_ref, idx_ref[0, 0, 2 * (first + i) + choice]),
                                      _tile_at(bufs.at[dst, choice], first + i),
                                      sem.at[dst, choice]).start(priority=choice)

    def combine(first):
        rows_g = pl.ds(pl.multiple_of(first, group), group)
        gate = route_ref[rows_g, :]
        tile0 = pl.multiple_of(first * TOKEN_ROWS, group * TOKEN_ROWS)
        half = TOKEN_ROWS * LANES
        for c in range(TOKEN_ROWS):
            picked = [_unpack_words(bufs[cur, choice, pl.ds(tile0 + c, group, stride=TOKEN_ROWS), :])
                      for choice in range(2)]
            for part in range(2):
                lanes = slice(part * half + c * LANES, part * half + (c + 1) * LANES)
                o_ref[rows_g, lanes] = (x_ref[rows_g, lanes] + gate[:, 2:3] * picked[0][part]
                                        + gate[:, 3:4] * picked[1][part])

    def trips(body):
        lax.fori_loop(0, rows // group, lambda t, c: (body(t * group), c)[1], 0, unroll=4)

    for ahead in range(COMBINE_AHEAD):
        pl.when((step == 0) & (ahead < n_steps))(
            lambda ahead=ahead: trips(lambda first: issue(idx_refs[ahead], ahead, first)))
    for choice in range(2):
        pltpu.make_async_copy(ys_ref.at[pl.ds(0, rows * TOKEN_ROWS)], bufs.at[cur, choice], sem.at[cur, choice]).wait()
    more = step + COMBINE_AHEAD < n_steps
    pl.when(more)(lambda: trips(lambda first: (combine(first),
                                               issue(idx_refs[COMBINE_AHEAD], (step + COMBINE_AHEAD) % ring, first))))
    pl.when(jnp.logical_not(more))(lambda: trips(combine))


def _combine(x, ys, route, pos, rows=512):
    n, d = x.shape
    rows = min(rows, n)
    nb = n // rows
    idx = lambda ahead: pl.BlockSpec((1, 1, 2 * rows), lambda i: (jnp.minimum(i + ahead, nb - 1), 0, 0),
                                     memory_space=pltpu.SMEM)
    pos = pos.reshape(nb, 1, 2 * rows)
    ring = COMBINE_AHEAD + 1
    return pl.pallas_call(
        functools.partial(_combine_kernel, rows),
        out_shape=jax.ShapeDtypeStruct((n, d), F32),
        grid=(nb,),
        in_specs=[idx(ahead) for ahead in range(ring)]
                 + [pl.BlockSpec((rows, d), lambda i: (i, 0)),
                    pl.BlockSpec((rows, LANES), lambda i: (i, 0)),
                    pl.BlockSpec(memory_space=pl.ANY)],
        out_specs=pl.BlockSpec((rows, d), lambda i: (i, 0)),
        scratch_shapes=[pltpu.VMEM((ring, 2, rows * TOKEN_ROWS, LANES), ys.dtype), pltpu.SemaphoreType.DMA((ring, 2))],
        compiler_params=_params(("arbitrary",), 40),
        name="moe_combine",
    )(*([pos] * ring), x, route, ys)


def _hier_moe(x, h, route, counts, w_gate, w_up, w_down, layer):
    n, _ = x.shape
    counts = counts[0, :MOE_EXPERTS].astype(jnp.int32)
    padded = (counts + MOE_ROWS - 1) // MOE_ROWS * MOE_ROWS
    cum_padded = jnp.cumsum(padded)
    expert = route[:, 0:2].astype(jnp.int32)
    seg_start = jnp.sum(jnp.where(expert[..., None] == jnp.arange(MOE_EXPERTS, dtype=jnp.int32),
                                  cum_padded - padded, 0), axis=-1)
    dest = (seg_start + route[:, 4:6].astype(jnp.int32)).reshape(-1)
    n_blocks = 2 * n // MOE_ROWS + MOE_EXPERTS
    block_start = jnp.arange(n_blocks, dtype=jnp.int32) * MOE_ROWS
    block_expert = jnp.minimum(jnp.sum((block_start[:, None] >= cum_padded[None, :]).astype(jnp.int32), axis=1),
                               MOE_EXPERTS - 1)
    n_used = (cum_padded[-1:] // MOE_ROWS).astype(jnp.int32)

    spare = jnp.arange(n_blocks - MOE_EXPERTS, n_blocks, dtype=jnp.int32)
    zero_blocks = jnp.concatenate([jnp.where(padded > 0, cum_padded // MOE_ROWS - 1, -1).astype(jnp.int32),
                                   jnp.where(spare >= n_used[0], spare, -1)])
    xs = _dispatch(h, dest, zero_blocks, n_blocks * MOE_ROWS)
    ys = _moe_mlp(xs, block_expert, n_used, w_gate, w_up, w_down, layer)
    return _combine(x, ys, route, dest)


def _even_layer(x, bsz, t_len, norm_g, w_in, prm, v_first, router):
    r, lw, k, v, a, b, g, pool = _even_prep(x.reshape(bsz, t_len, -1), norm_g, w_in.astype(BF16), prm, v_first)
    y = _rwkv_chunk(r, lw, k, v, a, b)
    return _even_out(x, y, r, k, v, g, pool, prm, router), v


def _odd_layer(x, bsz, t_len, layer_idx, norm_g, w_in, prm, router):
    q, k, rest = _qk_prep(x.reshape(bsz, t_len, -1), norm_g, w_in.astype(BF16), prm["q_norm"], prm["k_norm"])
    lam_init = 0.8 - 0.6 * math.exp(-0.3 * layer_idx)
    lam = (jnp.exp(jnp.sum(prm["lam_q1"] * prm["lam_k1"])) - jnp.exp(jnp.sum(prm["lam_q2"] * prm["lam_k2"]))
           + lam_init)
    o_attn = _diff_attn(q, k, rest, lam, prm["subln"], 1.0 - lam_init)
    ng = bsz // S5_BATCH_GROUP
    u5 = rest[:, :, RW_WIDTH:].reshape(ng, S5_BATCH_GROUP, t_len, RW_WIDTH).transpose(0, 2, 1, 3)
    o_ssm = _s5(u5, prm).transpose(0, 2, 1, 3).reshape(bsz, t_len, RW_WIDTH)
    return _odd_out(x, o_attn, o_ssm, prm["w_out"], router)


def _s5_params(a_re, a_im, log_step, b_re, b_im, c_re, c_im, d_skip, w_glu):
    lam = lax.complex(jnp.minimum(a_re, -1e-4), a_im)
    lam_bar = jnp.exp(lam * jnp.exp(log_step))
    b_bar = ((lam_bar - 1.0) / lam)[..., None] * lax.complex(b_re, b_im)
    gpt = LANES // S5_GROUP_DIM
    eye = jnp.eye(gpt, dtype=F32)

    def in_map(part):
        p = part.reshape(S5_TILES, gpt, S5_STATE, S5_GROUP_DIM)
        return jnp.einsum("jgpc,gh->jgchp", p, eye).reshape(S5_TILES, LANES, gpt * S5_STATE)

    def out_map(part):
        p = part.reshape(S5_TILES, gpt, S5_GROUP_DIM, S5_STATE)
        return jnp.einsum("jgcp,gh->jgphc", p, eye).reshape(S5_TILES, gpt * S5_STATE, LANES)

    bt = jnp.concatenate([in_map(jnp.real(b_bar)), in_map(jnp.imag(b_bar))], axis=2).astype(BF16)
    ct = jnp.concatenate([out_map(c_re), -out_map(c_im)], axis=1).astype(BF16)
    return {"bt": bt, "ct": ct, "lam_re": jnp.real(lam_bar).reshape(1, -1), "lam_im": jnp.imag(lam_bar).reshape(1, -1),
            "d": d_skip, "w_glu": w_glu.astype(BF16)}


def _block_diag(blocks):
    g, r, c = blocks.shape
    return jnp.einsum("grc,gh->grhc", blocks, jnp.eye(g, dtype=blocks.dtype)).reshape(g * r, g * c)


def kernel(x, norm_mix_g, norm_ffn_g,
           even_w_in, rw_mu, rw_w0, rw_w2, rw_a0, rw_a2, rw_g2, rw_k_k, rw_k_a, rw_r_k,
           rw_ln_g, rw_ln_b, rw_v0, rw_v1, rw_v2, pool_w, pool_scale, even_w_out,
           odd_w_in, da_q_norm, da_k_norm, da_lam_q1, da_lam_k1, da_lam_q2, da_lam_k2, da_subln,
           s5_a_re, s5_a_im, s5_log_step, s5_b_re, s5_b_im, s5_c_re, s5_c_im, s5_d, s5_w_glu,
           odd_w_out,
           moe_w_group, moe_b_group, moe_w_expert, moe_b_expert, moe_w_gate, moe_w_up, moe_w_down):
    bsz, t_len, d = x.shape
    depth = norm_mix_g.shape[0]
    xf = x.reshape(bsz * t_len, d)
    v_first = None
    for layer in range(depth):
        i = layer // 2
        router = _router_operands(norm_ffn_g[layer], moe_w_group[layer], moe_b_group[layer], moe_w_expert[layer],
                                  moe_b_expert[layer])
        if layer % 2 == 0:
            rank = rw_w2.shape[1]
            wa = jnp.zeros((LANES, 2 * RW_WIDTH), F32)
            wa = wa.at[:rank, :RW_WIDTH].set(rw_w2[i]).at[rank:, RW_WIDTH:].set(rw_a2[i])
            prm = {"mu": rw_mu[i], "w0": rw_w0[i], "a0": rw_a0[i], "wa": wa.astype(BF16), "g2": rw_g2[i].astype(BF16),
                   "k_k": rw_k_k[i], "k_a": rw_k_a[i], "r_k": rw_r_k[i], "ln_g": rw_ln_g[i], "ln_b": rw_ln_b[i],
                   "pool_bd": _block_diag(pool_w[i]).astype(BF16), "pool_scale": pool_scale[i],
                   "w_out": even_w_out[i].astype(BF16)}
            if v_first is not None:
                vr = rw_v1.shape[2]
                prm["v0"] = rw_v0[i - 1]
                prm["v1"] = jnp.zeros((RW_WIDTH, LANES), F32).at[:, :vr].set(rw_v1[i - 1]).astype(BF16)
                prm["v2"] = jnp.zeros((LANES, RW_WIDTH), F32).at[:vr, :].set(rw_v2[i - 1]).astype(BF16)
            routed, v_new = _even_layer(xf, bsz, t_len, norm_mix_g[layer], even_w_in[i], prm, v_first, router)
            if v_first is None:
                v_first = v_new
        else:
            prm = _s5_params(s5_a_re[i], s5_a_im[i], s5_log_step[i], s5_b_re[i], s5_b_im[i], s5_c_re[i],
                             s5_c_im[i], s5_d[i].reshape(-1), s5_w_glu[i])
            prm.update({"q_norm": da_q_norm[i], "k_norm": da_k_norm[i], "lam_q1": da_lam_q1[i],
                        "lam_k1": da_lam_k1[i], "lam_q2": da_lam_q2[i], "lam_k2": da_lam_k2[i],
                        "subln": da_subln[i], "w_out": odd_w_out[i].astype(BF16)})
            routed = _odd_layer(xf, bsz, t_len, layer, norm_mix_g[layer], odd_w_in[i], prm, router)
        xf = _hier_moe(*routed, moe_w_gate, moe_w_up, moe_w_down, layer)
    return xf.reshape(bsz, t_len, d)
```

```python
import functools
import math

import jax
import jax.numpy as jnp
from jax import lax
from jax.experimental import pallas as pl
from jax.experimental.pallas import tpu as pltpu

F32 = jnp.float32
BF16 = jnp.bfloat16

LANES = 128
SUBLANES = 8
VMEM_BYTES_V7X = 64 * 1024 * 1024

D_MODEL = 1024
HEAD = 64
RW_WIDTH = 512
RW_SHIFT_COLS = 3 * RW_WIDTH + 64 + 64 + 128
RW_LN_EPS = 64e-5
POOL_WINDOWS = (2, 4, 8, 16)
POOL_HALO = 16
DA_HEADS = 4
SUBLN_EPS = 1e-5
ROPE_THETA = 10000.0
S5_GROUP_DIM = 16
S5_STATE = 64
S5_TILES = 4
S5_BATCH_GROUP = SUBLANES
MOE_GROUPS = 4
MOE_PER_GROUP = 8
MOE_EXPERTS = 32
MOE_ROWS = 512
RMS_EPS = 1e-6
CHUNK = 64
NEG_BIG = -1e30


def _params(semantics, vmem_mib):
    return pltpu.CompilerParams(dimension_semantics=semantics,
                                vmem_limit_bytes=min(vmem_mib * 1024 * 1024, VMEM_BYTES_V7X - 8 * 1024 * 1024))


def _bdot(a, b):
    return jnp.dot(a.astype(BF16), b.astype(BF16), preferred_element_type=F32)


def _bdot_nt(a, b):
    return lax.dot_general(a.astype(BF16), b.astype(BF16), (((1,), (1,)), ((), ())),
                           preferred_element_type=F32)


def _split_dot(x, m_bf16):
    hi = x.astype(BF16)
    lo = (x - hi.astype(F32)).astype(BF16)
    return (jnp.dot(hi, m_bf16, preferred_element_type=F32)
            + jnp.dot(lo, m_bf16, preferred_element_type=F32))


def _seg_ones(width):
    r = lax.broadcasted_iota(jnp.int32, (LANES, LANES), 0)
    c = lax.broadcasted_iota(jnp.int32, (LANES, LANES), 1)
    sh = int(math.log2(width))
    return ((r >> sh) == (c >> sh)).astype(BF16)


def _segsum(x, seg):
    tiles = [_split_dot(x[:, j * LANES:(j + 1) * LANES], seg) for j in range(x.shape[1] // LANES)]
    return tiles[0] if len(tiles) == 1 else jnp.concatenate(tiles, axis=1)


def _sigmoid(x):
    return 1.0 / (1.0 + jnp.exp(-x))


def _even_prep_kernel(has_vres, tb, *refs):
    if has_vres:
        (x_ref, gn_ref, win_ref, mu_ref, w0_ref, a0_ref, wa_ref, g2_ref, kk_ref, ka_ref, pw_ref, ps_ref,
         vf_ref, v0_ref, v1_ref, v2_ref,
         r_o, lw_o, k_o, v_o, a_o, b_o, g_o, pool_o, carry) = refs
    else:
        (x_ref, gn_ref, win_ref, mu_ref, w0_ref, a0_ref, wa_ref, g2_ref, kk_ref, ka_ref, pw_ref, ps_ref,
         r_o, lw_o, k_o, v_o, a_o, b_o, g_o, pool_o, carry) = refs
    ti = pl.program_id(1)

    @pl.when(ti == 0)
    def _():
        carry[...] = jnp.zeros_like(carry)

    x = x_ref[0]
    h = x * lax.rsqrt(jnp.mean(x * x, axis=-1, keepdims=True) + RMS_EPS) * gn_ref[...]
    u = jnp.dot(h.astype(BF16), win_ref[...], preferred_element_type=F32)
    ext = jnp.concatenate([carry[...], u], axis=0)
    carry[...] = u[tb - POOL_HALO:, :]

    p1 = ext[:, RW_SHIFT_COLS:]
    p2 = p1 + pltpu.roll(p1, 1, 0)
    p4 = p2 + pltpu.roll(p2, 2, 0)
    p8 = p4 + pltpu.roll(p4, 4, 0)
    p16 = p8 + pltpu.roll(p8, 8, 0)
    lane = lax.broadcasted_iota(jnp.int32, (1, RW_WIDTH), 1)
    grp = lane >> 7
    sums = jnp.where(grp == 0, p2, jnp.where(grp == 1, p4, jnp.where(grp == 2, p8, p16)))[POOL_HALO:]
    win = jnp.where(grp == 0, 2.0, jnp.where(grp == 1, 4.0, jnp.where(grp == 2, 8.0, 16.0)))
    n_seen = (ti * tb + lax.broadcasted_iota(jnp.int32, (tb, 1), 0) + 1).astype(F32)
    d = sums / jnp.minimum(n_seen, win) - u[:, RW_SHIFT_COLS:]
    pool_o[0] = (_bdot(d, pw_ref[...]) * ps_ref[...]).astype(pool_o.dtype)

    u_rw = u[:, :RW_SHIFT_COLS]
    prev = pltpu.roll(ext[:, :RW_SHIFT_COLS], 1, 0)[POOL_HALO:]
    m = u_rw + (prev - u_rw) * mu_ref[...]
    r = m[:, :RW_WIDTH]
    k = m[:, RW_WIDTH:2 * RW_WIDTH]
    v = m[:, 2 * RW_WIDTH:3 * RW_WIDTH]
    dwa = m[:, 3 * RW_WIDTH:3 * RW_WIDTH + LANES]
    dg = m[:, 3 * RW_WIDTH + LANES:]
    l128 = lax.broadcasted_iota(jnp.int32, (1, LANES), 1)
    dwa = jnp.where(l128 < HEAD, jnp.tanh(dwa), dwa)
    x12 = _bdot(dwa, wa_ref[...])
    z = -(w0_ref[...] + x12[:, :RW_WIDTH])
    softplus = jnp.maximum(z, 0.0) + jnp.log(1.0 + jnp.exp(-jnp.abs(z)))
    lw = -jnp.exp(-softplus - 0.5)
    a_i = _sigmoid(a0_ref[...] + x12[:, RW_WIDTH:])
    g_o[0] = _bdot(_sigmoid(dg), g2_ref[...]).astype(g_o.dtype)
    if has_vres:
        gate_v = _sigmoid(v0_ref[...] + _bdot(_bdot(v, v1_ref[...]), v2_ref[...]))
        v = v + (vf_ref[0] - v) * gate_v
    kk = k * kk_ref[...]
    ss = _segsum(kk * kk, _seg_ones(HEAD))
    kk = kk / jnp.maximum(jnp.sqrt(ss), 1e-12)
    r_o[0] = r.astype(r_o.dtype)
    lw_o[0] = lw
    k_o[0] = (k * (1.0 + (a_i - 1.0) * ka_ref[...])).astype(k_o.dtype)
    v_o[0] = v.astype(v_o.dtype)
    a_o[0] = (-kk).astype(a_o.dtype)
    b_o[0] = (kk * a_i).astype(b_o.dtype)


def _even_prep(x, norm_g, w_in, prm, v_first, tb=512):
    bsz, t_len, d = x.shape
    cin = w_in.shape[1]
    tb = min(tb, t_len)
    has_vres = v_first is not None
    row = lambda a: a.reshape(1, -1)
    full = lambda shape: pl.BlockSpec(shape, lambda b, t: (0,) * len(shape))
    seq = pl.BlockSpec((1, tb, RW_WIDTH), lambda b, t: (b, t, 0))
    ins = [x, row(norm_g), w_in, row(prm["mu"]), row(prm["w0"]), row(prm["a0"]), prm["wa"], prm["g2"], row(prm["k_k"]),
           row(prm["k_a"]), prm["pool_bd"], row(prm["pool_scale"])]
    specs = [pl.BlockSpec((1, tb, d), lambda b, t: (b, t, 0)), full((1, d)), full((d, cin)), full((1, RW_SHIFT_COLS)),
             full((1, RW_WIDTH)), full((1, RW_WIDTH)), full((LANES, 2 * RW_WIDTH)),
             full((LANES, RW_WIDTH)), full((1, RW_WIDTH)), full((1, RW_WIDTH)),
             full((RW_WIDTH, RW_WIDTH)), full((1, RW_WIDTH))]
    if has_vres:
        ins += [v_first, row(prm["v0"]), prm["v1"], prm["v2"]]
        specs += [seq, full((1, RW_WIDTH)), full((RW_WIDTH, LANES)), full((LANES, RW_WIDTH))]
    out = lambda dt: jax.ShapeDtypeStruct((bsz, t_len, RW_WIDTH), dt)
    return pl.pallas_call(
        functools.partial(_even_prep_kernel, has_vres, tb),
        out_shape=(out(BF16), out(F32)) + (out(BF16),) * 6,
        grid=(bsz, t_len // tb),
        in_specs=specs,
        out_specs=(seq,) * 8,
        scratch_shapes=[pltpu.VMEM((POOL_HALO, cin), F32)],
        compiler_params=_params(("parallel", "arbitrary"), 56),
        name="even_prep",
    )(*ins)


def _rwkv_chunk_kernel(nb, r_ref, lw_ref, k_ref, v_ref, a_ref, b_ref, y_ref, s_ref):
    ci = pl.program_id(1)

    @pl.when(ci == 0)
    def _():
        s_ref[...] = jnp.zeros_like(s_ref)

    L = CHUNK
    n_pair = RW_WIDTH // LANES
    tri = (lax.broadcasted_iota(jnp.int32, (L, L), 0) >= lax.broadcasted_iota(jnp.int32, (L, L), 1)).astype(BF16)
    lane = lax.broadcasted_iota(jnp.int32, (1, LANES), 1)
    m_a = lane < HEAD
    t_idx = lax.broadcasted_iota(jnp.int32, (L, LANES), 0)
    s_idx = lax.broadcasted_iota(jnp.int32, (L, LANES), 1) & (HEAD - 1)
    strict = t_idx > s_idx
    incl = t_idx >= s_idx
    r128 = lax.broadcasted_iota(jnp.int32, (LANES, LANES), 0)
    c128 = lax.broadcasted_iota(jnp.int32, (LANES, LANES), 1)
    eye = (r128 == c128).astype(F32)
    same_head = (r128 >> 6) == (c128 >> 6)

    def only_a(x):
        return jnp.where(m_a, x, jnp.zeros_like(x))

    def only_b(x):
        return jnp.where(m_a, jnp.zeros_like(x), x)

    def stack(x):
        return jnp.concatenate([only_a(x), only_b(x)], axis=0)

    def stack_sw(x):
        return jnp.concatenate([only_b(x), only_a(x)], axis=0)

    def mm(x, y):
        return jnp.dot(x, y, preferred_element_type=F32)

    def mm_nt(x, y):
        return lax.dot_general(x, y, (((1,), (1,)), ((), ())), preferred_element_type=F32)

    chains = []
    for bi in range(nb):
        lw = lw_ref[bi]
        c = _split_dot_lhs(tri, lw)
        e_pos = jnp.exp(c)
        e_neg = jnp.exp(-c)
        a_all = (jnp.exp(c - lw) * a_ref[bi]).astype(BF16)
        b_all = (b_ref[bi] * e_neg).astype(BF16)
        k_all = (k_ref[bi] * e_neg).astype(BF16)
        r_all = r_ref[bi] * e_pos
        v_all = v_ref[bi].astype(BF16)
        for p in range(n_pair):
            sl = slice(p * LANES, (p + 1) * LANES)
            chains.append({"bi": bi, "p": p, "sl": sl, "at": a_all[:, sl], "bt": b_all[:, sl], "kt": k_all[:, sl],
                           "rt": r_all[:, sl], "vv": v_all[:, sl], "gl": e_pos[L - 1:L, sl]})

    for ch in chains:
        rt16 = ch["rt"].astype(BF16)
        bk = jnp.concatenate([ch["bt"], ch["kt"]], axis=0)
        kb = jnp.concatenate([ch["kt"], ch["bt"]], axis=0)
        s_a = mm_nt(jnp.concatenate([only_a(ch["at"]), only_a(rt16)], axis=0), bk)
        s_b = mm_nt(jnp.concatenate([only_b(ch["at"]), only_b(rt16)], axis=0), kb)
        m_ha = jnp.where(strict, s_a[:L], 0.0)
        n_ha = jnp.where(incl, s_a[L:], 0.0)
        m_hb = jnp.where(strict, s_b[:L], 0.0)
        n_hb = jnp.where(incl, s_b[L:], 0.0)
        ch["bk"] = bk
        ch["bdm"] = jnp.concatenate([only_a(m_ha), only_b(m_hb)], axis=0)
        ch["ak_sw"] = jnp.where(m_a, m_hb, m_ha).astype(BF16)
        ch["n_lhs"] = jnp.concatenate([jnp.where(m_a, n_ha, n_hb),
                                       jnp.where(m_a, n_hb, n_ha)], axis=1).astype(BF16)
        ch["t_inv"] = eye + ch["bdm"]
        ch["m_pow"] = ch["bdm"].astype(BF16)
    for ch in chains:
        ch["w"] = mm(ch["ak_sw"], stack_sw(ch["vv"])).astype(BF16)
    for _ in range(int(math.log2(L)) - 1):
        for ch in chains:
            ch["m_pow"] = mm(ch["m_pow"], ch["m_pow"]).astype(BF16)
        for ch in chains:
            ch["t_inv"] = ch["t_inv"] + mm(ch["t_inv"].astype(BF16), ch["m_pow"])
    for ch in chains:
        t_pk = (ch["t_inv"][:L] + ch["t_inv"][L:]).astype(BF16)
        au = mm(t_pk, jnp.concatenate([stack(ch["at"]), stack(ch["w"])], axis=1))
        ch["a_hat"] = au[:, :LANES].astype(BF16)
        ch["u_hat"] = au[:, LANES:].astype(BF16)
    for ch in chains:
        rhs = jnp.concatenate([
            jnp.concatenate([stack(ch["u_hat"]), stack(ch["a_hat"])], axis=1),
            jnp.concatenate([stack_sw(ch["vv"]), jnp.zeros((LANES, LANES), BF16)], axis=1)], axis=0)
        yr = mm(ch["n_lhs"], rhs)
        ch["y_hat"] = yr[:, :LANES]
        ch["r_hat"] = (ch["rt"] + yr[:, LANES:]).astype(BF16)
        g_p = mm(ch["a_hat"].astype(F32).T.astype(BF16), ch["bt"])
        g_q = mm(jnp.concatenate([ch["u_hat"], ch["vv"]], axis=0).astype(F32).T.astype(BF16), ch["bk"])
        ch["p_bd"] = ((eye + jnp.where(same_head, g_p, 0.0)) * ch["gl"]).astype(BF16)
        ch["q_pk"] = jnp.where(m_a, g_q[:L], g_q[L:]) * ch["gl"]
    for ch in chains:
        s0 = s_ref[ch["bi"], ch["p"]]
        s16 = s0.astype(BF16)
        y_ref[ch["bi"], :, ch["sl"]] = mm_nt(ch["r_hat"], stack(s16)) + ch["y_hat"]
        s_ref[ch["bi"], ch["p"]] = mm(s16, ch["p_bd"]) + ch["q_pk"]


def _split_dot_lhs(m_bf16, x):
    hi = x.astype(BF16)
    lo = (x - hi.astype(F32)).astype(BF16)
    return (jnp.dot(m_bf16, hi, preferred_element_type=F32)
            + jnp.dot(m_bf16, lo, preferred_element_type=F32))


def _rwkv_chunk(r, lw, k, v, a, b, nb=8):
    bsz, t_len, _ = r.shape
    seq = pl.BlockSpec((nb, CHUNK, RW_WIDTH), lambda bi, ci: (bi, ci, 0))
    return pl.pallas_call(
        functools.partial(_rwkv_chunk_kernel, nb),
        out_shape=jax.ShapeDtypeStruct((bsz, t_len, RW_WIDTH), F32),
        grid=(bsz // nb, t_len // CHUNK),
        in_specs=[seq] * 6,
        out_specs=seq,
        scratch_shapes=[pltpu.VMEM((nb, RW_WIDTH // LANES, HEAD, LANES), F32)],
        compiler_params=_params(("parallel", "arbitrary"), 32),
        name="rwkv_chunk",
    )(r, lw, k, v, a, b)


def _even_out_kernel(tm, x_ref, y_ref, r_ref, k_ref, v_ref, g_ref, p_ref, lng_ref, lnb_ref, rk_ref, wo_ref,
                     gf_ref, whi_ref, wlo_ref, rb_ref, o_ref, h_o, route_o, cnt_o, run):
    seg = _seg_ones(HEAD)
    y = y_ref[...]
    mean = _segsum(y, seg) * (1.0 / HEAD)
    yc = y - mean
    var = _segsum(yc * yc, seg) * (1.0 / HEAD)
    yn = yc * lax.rsqrt(var + RW_LN_EPS) * lng_ref[...] + lnb_ref[...]
    bonus = _segsum(r_ref[...].astype(F32) * k_ref[...].astype(F32) * rk_ref[...], seg) * v_ref[...].astype(F32)
    o_rw = (yn + bonus) * g_ref[...].astype(F32)
    cat = jnp.concatenate([o_rw.astype(BF16), p_ref[...]], axis=1)
    x_new = x_ref[...] + jnp.dot(cat, wo_ref[...], preferred_element_type=F32)
    o_ref[...] = x_new
    _route_block(tm, x_new, gf_ref, whi_ref, wlo_ref, rb_ref, h_o, route_o, cnt_o, run)


def _even_out(x, y, r, k, v, g, pool, prm, router, tm=512):
    n, d = x.shape
    tm = min(tm, n)
    half = pl.BlockSpec((tm, RW_WIDTH), lambda i: (i, 0))
    vec = pl.BlockSpec((1, RW_WIDTH), lambda i: (0, 0))
    fl = lambda a: a.reshape(n, RW_WIDTH)
    r_in, r_shape, r_out, r_scratch = _router_specs(tm, d)
    return pl.pallas_call(
        functools.partial(_even_out_kernel, tm),
        out_shape=(jax.ShapeDtypeStruct((n, d), F32),) + r_shape(n),
        grid=(n // tm,),
        in_specs=[pl.BlockSpec((tm, d), lambda i: (i, 0))] + [half] * 6 + [vec] * 3
                 + [pl.BlockSpec((d, d), lambda i: (0, 0))] + r_in,
        out_specs=(pl.BlockSpec((tm, d), lambda i: (i, 0)),) + r_out,
        scratch_shapes=[r_scratch],
        compiler_params=_params(("arbitrary",), 48),
        name="even_out",
    )(x, fl(y), fl(r), fl(k), fl(v), fl(g), fl(pool), prm["ln_g"].reshape(1, -1), prm["ln_b"].reshape(1, -1),
      prm["r_k"].reshape(1, -1), prm["w_out"], *router)


def _qk_prep_kernel(x_ref, gn_ref, win_ref, gq_ref, gk_ref, cos_ref, sin_ref, q_o, k_o, v_o, s5_o):
    xin = x_ref[0]
    h = xin * lax.rsqrt(jnp.mean(xin * xin, axis=-1, keepdims=True) + RMS_EPS) * gn_ref[...]
    u = jnp.dot(h.astype(BF16), win_ref[...], preferred_element_type=F32)
    v_o[0] = u[:, 2 * RW_WIDTH:3 * RW_WIDTH]
    s5_o[0] = u[:, 3 * RW_WIDTH:]
    x = u[:, :2 * RW_WIDTH]
    cos = cos_ref[...]
    sin = sin_ref[...]
    inv_rms = lax.rsqrt(_segsum(x * x, _seg_ones(HEAD)) * (1.0 / HEAD) + RMS_EPS)
    lane = lax.broadcasted_iota(jnp.int32, (1, LANES), 1)
    first = (lane & (HEAD - 1)) < HEAD // 2
    nq = RW_WIDTH // LANES
    for j in range(2 * nq):
        tile = slice(j * LANES, (j + 1) * LANES)
        gain = gq_ref[...] if j < nq else gk_ref[...]
        xn = x[:, tile] * inv_rms[:, tile] * gain
        partner = jnp.where(first, pltpu.roll(xn, LANES - HEAD // 2, 1), pltpu.roll(xn, HEAD // 2, 1))
        out = xn * cos + partner * sin
        if j < nq:
            q_o[0, :, j * LANES:(j + 1) * LANES] = (out * (HEAD ** -0.5)).astype(BF16)
        else:
            k_o[0, :, (j - nq) * LANES:(j - nq + 1) * LANES] = out.astype(BF16)


def _qk_prep(x, norm_g, w_in, q_norm, k_norm, tb=512):
    bsz, t_len, d = x.shape
    cin = w_in.shape[1]
    tb = min(tb, t_len)
    inv_freq = ROPE_THETA ** (-jnp.arange(0, HEAD, 2, dtype=F32) / HEAD)
    ang = jnp.arange(t_len, dtype=F32)[:, None] * inv_freq[None, :]
    cos = jnp.tile(jnp.cos(ang), (1, LANES // (HEAD // 2)))
    sin_half = jnp.concatenate([-jnp.sin(ang), jnp.sin(ang)], axis=1)
    sin = jnp.tile(sin_half, (1, LANES // HEAD))
    tile2 = lambda g: jnp.tile(g, LANES // HEAD).reshape(1, LANES)
    out = jax.ShapeDtypeStruct((bsz, t_len, RW_WIDTH), BF16)
    tab = pl.BlockSpec((tb, LANES), lambda b, t: (t, 0))
    vec = pl.BlockSpec((1, LANES), lambda b, t: (0, 0))
    seq = pl.BlockSpec((1, tb, RW_WIDTH), lambda b, t: (b, t, 0))
    const = lambda shape: pl.BlockSpec(shape, lambda b, t: (0,) * len(shape))
    return pl.pallas_call(
        _qk_prep_kernel,
        out_shape=(out, out) + (jax.ShapeDtypeStruct((bsz, t_len, RW_WIDTH), F32),) * 2,
        grid=(bsz, t_len // tb),
        in_specs=[pl.BlockSpec((1, tb, d), lambda b, t: (b, t, 0)), const((1, d)), const((d, cin)), vec, vec, tab, tab],
        out_specs=(seq,) * 4,
        compiler_params=_params(("parallel", "parallel"), 48),
        name="qk_prep",
    )(x, norm_g.reshape(1, d), w_in, tile2(q_norm), tile2(k_norm), cos, sin)


def _diff_attn_kernel(tq, nh, out_scale, q_ref, k_ref, v_ref, lam_ref, sub_ref, o_ref):
    qi = pl.program_id(2)
    lane = lax.broadcasted_iota(jnp.int32, (1, LANES), 1)
    m_a = lane < HEAD

    def prefix(n_blk):
        kl = n_blk * tq
        on_diag = lax.broadcasted_iota(jnp.int32, (tq, tq), 0) >= lax.broadcasted_iota(jnp.int32, (tq, tq), 1)
        qs, ks, vs = [], [], []
        for h in range(nh):
            sl = slice(h * LANES, (h + 1) * LANES)
            q = q_ref[0, :, sl]
            zero = jnp.zeros_like(q)
            qs += [jnp.where(m_a, q, zero), jnp.where(m_a, zero, q)]
            ks += [k_ref[0, :kl, sl]] * 2
            vs += [jnp.concatenate([v_ref[0, :kl, sl].astype(BF16), jnp.ones((kl, LANES), BF16)], axis=1)] * 2
        s = [lax.dot_general(q, k, (((1,), (1,)), ((), ())), preferred_element_type=F32) for q, k in zip(qs, ks)]
        last = [jnp.where(on_diag, x[:, kl - tq:], NEG_BIG) for x in s]
        if n_blk > 1:
            last = [jnp.concatenate([x[:, :kl - tq], y], axis=1) for x, y in zip(s, last)]
        top = [jnp.max(x, axis=-1, keepdims=True) for x in last]
        e = [jnp.exp((x - m).astype(BF16)) for x, m in zip(last, top)]
        pv = [jnp.dot(x, v, preferred_element_type=F32) for x, v in zip(e, vs)]
        pv = [x[:, :LANES] / x[:, LANES:] for x in pv]
        for h in range(nh):
            sl = slice(h * LANES, (h + 1) * LANES)
            o = pv[2 * h] - lam_ref[...] * pv[2 * h + 1]
            ms = jnp.mean(o * o, axis=-1, keepdims=True)
            o_ref[0, :, sl] = o * lax.rsqrt(ms + SUBLN_EPS) * sub_ref[...] * out_scale

    for blk in range(k_ref.shape[1] // tq):
        pl.when(qi == blk)(functools.partial(prefix, blk + 1))


def _diff_attn(q, k, v_src, lam, subln, out_scale, tq=256, nh=4):
    bsz, t_len, _ = q.shape
    tq = min(tq, t_len)
    wide = nh * LANES
    return pl.pallas_call(
        functools.partial(_diff_attn_kernel, tq, nh, out_scale),
        out_shape=jax.ShapeDtypeStruct((bsz, t_len, RW_WIDTH), F32),
        grid=(bsz, DA_HEADS // nh, t_len // tq),
        in_specs=[pl.BlockSpec((1, tq, wide), lambda b, h, i: (b, i, h)),
                  pl.BlockSpec((1, t_len, wide), lambda b, h, i: (b, 0, h)),
                  pl.BlockSpec((1, t_len, wide), lambda b, h, i: (b, 0, h)),
                  pl.BlockSpec((1, 1), lambda b, h, i: (0, 0)),
                  pl.BlockSpec((1, LANES), lambda b, h, i: (0, 0))],
        out_specs=pl.BlockSpec((1, tq, wide), lambda b, h, i: (b, i, h)),
        compiler_params=_params(("parallel", "parallel", "arbitrary"), 48),
        name="diff_attn",
    )(q, k, v_src, lam.reshape(1, 1), subln.reshape(1, LANES))


def _s5_kernel(tb, u_ref, bt_ref, lre_ref, lim_ref, ct_ref, d_ref, wg_ref, o_ref, xs, st):
    ti = pl.program_id(1)

    @pl.when(ti == 0)
    def _():
        st[...] = jnp.zeros_like(st)

    rows = tb * SUBLANES
    half = RW_WIDTH
    u = u_ref[0].reshape(rows, RW_WIDTH)
    for j in range(S5_TILES):
        xs[:, 2 * half * j:2 * half * (j + 1)] = _bdot(u[:, j * LANES:(j + 1) * LANES], bt_ref[j])
    for j in range(S5_TILES):
        re = slice(2 * half * j, 2 * half * j + half)
        im = slice(2 * half * j + half, 2 * half * (j + 1))
        lr = jnp.broadcast_to(lre_ref[:, half * j:half * (j + 1)], (SUBLANES, half))
        li = jnp.broadcast_to(lim_ref[:, half * j:half * (j + 1)], (SUBLANES, half))

        xr, xi = st[:, re], st[:, im]
        for t in range(tb):
            row = slice(t * SUBLANES, (t + 1) * SUBLANES)
            xr, xi = lr * xr - li * xi + xs[row, re], lr * xi + li * xr + xs[row, im]
            xs[row, re] = xr
            xs[row, im] = xi
        st[:, re] = xr
        st[:, im] = xi
    y = jnp.concatenate([_bdot(xs[:, 2 * half * j:2 * half * (j + 1)], ct_ref[j]) for j in range(S5_TILES)], axis=1)
    y = y + d_ref[...] * u
    z = 0.5 * y * (1.0 + jnp.tanh(math.sqrt(2.0 / math.pi) * (y + 0.044715 * (y * y * y))))
    o = z * _sigmoid(_bdot(z, wg_ref[...]))
    o_ref[0] = o.reshape(tb, SUBLANES, RW_WIDTH)


def _s5(u5, prm, tb=128):
    ng, t_len, _, _ = u5.shape
    tb = min(tb, t_len)
    full = lambda shape: pl.BlockSpec(shape, lambda g, t: (0,) * len(shape))
    n_state = 2 * RW_WIDTH * S5_TILES
    return pl.pallas_call(
        functools.partial(_s5_kernel, tb),
        out_shape=jax.ShapeDtypeStruct(u5.shape, F32),
        grid=(ng, t_len // tb),
        in_specs=[pl.BlockSpec((1, tb, SUBLANES, RW_WIDTH), lambda g, t: (g, t, 0, 0)),
                  full((S5_TILES, LANES, 2 * RW_WIDTH)), full((1, n_state // 2)), full((1, n_state // 2)),
                  full((S5_TILES, 2 * RW_WIDTH, LANES)), full((1, RW_WIDTH)), full((RW_WIDTH, RW_WIDTH))],
        out_specs=pl.BlockSpec((1, tb, SUBLANES, RW_WIDTH), lambda g, t: (g, t, 0, 0)),
        scratch_shapes=[pltpu.VMEM((tb * SUBLANES, n_state), F32), pltpu.VMEM((SUBLANES, n_state), F32)],
        compiler_params=_params(("parallel", "arbitrary"), 48),
        name="s5_scan",
    )(u5, prm["bt"], prm["lam_re"], prm["lam_im"], prm["ct"], prm["d"].reshape(1, -1), prm["w_glu"])


def _odd_out_kernel(tm, x_ref, a_ref, s_ref, wo_ref, gf_ref, whi_ref, wlo_ref, rb_ref, o_ref, h_o, route_o, cnt_o, run):
    cat = jnp.concatenate([a_ref[...], s_ref[...]], axis=1)
    x_new = x_ref[...] + _bdot(cat, wo_ref[...])
    o_ref[...] = x_new
    _route_block(tm, x_new, gf_ref, whi_ref, wlo_ref, rb_ref, h_o, route_o, cnt_o, run)


def _odd_out(x, o_attn, o_ssm, w_out, router, tm=512):
    n, d = x.shape
    tm = min(tm, n)
    half = pl.BlockSpec((tm, RW_WIDTH), lambda i: (i, 0))
    r_in, r_shape, r_out, r_scratch = _router_specs(tm, d)
    return pl.pallas_call(
        functools.partial(_odd_out_kernel, tm),
        out_shape=(jax.ShapeDtypeStruct((n, d), F32),) + r_shape(n),
        grid=(n // tm,),
        in_specs=[pl.BlockSpec((tm, d), lambda i: (i, 0)), half, half, pl.BlockSpec((d, d), lambda i: (0, 0))] + r_in,
        out_specs=(pl.BlockSpec((tm, d), lambda i: (i, 0)),) + r_out,
        scratch_shapes=[r_scratch],
        compiler_params=_params(("arbitrary",), 40),
        name="odd_out",
    )(x, o_attn.reshape(n, RW_WIDTH), o_ssm.reshape(n, RW_WIDTH), w_out, *router)


TOKEN_ROWS = SUBLANES // 2
U32 = jnp.uint32


def _pack_rows(x):
    half = x.shape[1] // 2
    hi = lax.bitcast_convert_type(x[:, :half].astype(BF16).astype(F32), U32)
    lo = lax.bitcast_convert_type(x[:, half:].astype(BF16).astype(F32), U32)
    return hi | (lo >> 16)


def _unpack_words(w):
    return (lax.bitcast_convert_type(w & jnp.uint32(0xFFFF0000), F32), lax.bitcast_convert_type(w << 16, F32))


def _store_token_tiles(ref, x, n_tok):
    words = _pack_rows(x)
    for c in range(TOKEN_ROWS):
        ref[pl.ds(c, n_tok, stride=TOKEN_ROWS), :] = words[:, c * LANES:(c + 1) * LANES]


def _load_token_tiles(ref, n_tok):
    words = jnp.concatenate([ref[pl.ds(c, n_tok, stride=TOKEN_ROWS), :] for c in range(TOKEN_ROWS)], axis=1)
    return jnp.concatenate(_unpack_words(words), axis=1)


def _route_block(tm, x, g_ref, whi_ref, wlo_ref, b_ref, h_o, route_o, cnt_o, run):
    @pl.when(pl.program_id(0) == 0)
    def _():
        run[...] = jnp.zeros_like(run)

    ms = jnp.mean(x * x, axis=-1, keepdims=True)
    h = x * lax.rsqrt(ms + RMS_EPS) * g_ref[...]
    _store_token_tiles(h_o, h, tm)
    h_hi = h.astype(BF16)
    h_lo = (h - h_hi.astype(F32)).astype(BF16)
    w_hi = whi_ref[...]
    both = jnp.dot(h_hi, jnp.concatenate([w_hi, wlo_ref[...]], axis=1), preferred_element_type=F32)
    logits = both[:, :LANES] + both[:, LANES:] + jnp.dot(h_lo, w_hi, preferred_element_type=F32) + b_ref[...]
    lane = lax.broadcasted_iota(jnp.int32, logits.shape, 1).astype(F32)
    far = float(LANES)
    is_g = lane < MOE_GROUPS
    g_max = jnp.max(jnp.where(is_g, logits, NEG_BIG), axis=-1, keepdims=True)
    g_sum = jnp.sum(jnp.where(is_g, jnp.exp(jnp.minimum(logits - g_max, 0.0)), 0.0), axis=-1, keepdims=True)
    g_top = jnp.min(jnp.where(is_g & (logits == g_max), lane, far), axis=-1, keepdims=True)
    lo = MOE_GROUPS + MOE_PER_GROUP * g_top
    in_grp = (lane >= lo) & (lane < lo + MOE_PER_GROUP)
    e1 = jnp.max(jnp.where(in_grp, logits, NEG_BIG), axis=-1, keepdims=True)
    i1 = jnp.min(jnp.where(in_grp & (logits == e1), lane, far), axis=-1, keepdims=True)
    rest = in_grp & (lane != i1)
    e2 = jnp.max(jnp.where(rest, logits, NEG_BIG), axis=-1, keepdims=True)
    i2 = jnp.min(jnp.where(rest & (logits == e2), lane, far), axis=-1, keepdims=True)
    ratio = jnp.exp(e2 - e1)
    gate1 = 1.0 / (g_sum * (1.0 + ratio))
    gate2 = gate1 * ratio
    ex1 = i1 - MOE_GROUPS
    ex2 = i2 - MOE_GROUPS
    oh1 = lane == ex1
    oh2 = lane == ex2
    before = (lax.broadcasted_iota(jnp.int32, (tm, tm), 0) > lax.broadcasted_iota(jnp.int32, (tm, tm), 1)).astype(BF16)
    pre1 = jnp.dot(before, oh1.astype(BF16), preferred_element_type=F32)
    pre2 = jnp.dot(before, oh2.astype(BF16), preferred_element_type=F32)
    tot1 = jnp.sum(oh1.astype(F32), axis=0, keepdims=True)
    tot2 = jnp.sum(oh2.astype(F32), axis=0, keepdims=True)
    base = run[...]
    rank1 = jnp.sum(jnp.where(oh1, base + pre1, 0.0), axis=-1, keepdims=True)
    rank2 = jnp.sum(jnp.where(oh2, base + tot1 + pre2, 0.0), axis=-1, keepdims=True)
    run[...] = base + tot1 + tot2
    cnt_o[...] = base + tot1 + tot2
    route_o[...] = jnp.where(lane == 0, ex1, jnp.where(lane == 1, ex2, jnp.where(lane == 2, gate1, jnp.where(
        lane == 3, gate2, jnp.where(lane == 4, rank1, jnp.where(lane == 5, rank2, 0.0))))))


def _router_operands(g, w_group, b_group, w_expert, b_expert):
    d = g.shape[0]
    assert d == SUBLANES * LANES
    w_route = jnp.zeros((d, LANES), F32).at[:, :MOE_GROUPS].set(w_group)
    w_route = w_route.at[:, MOE_GROUPS:MOE_GROUPS + MOE_EXPERTS].set(w_expert)
    w_hi = w_route.astype(BF16)
    w_lo = (w_route - w_hi.astype(F32)).astype(BF16)
    bias = jnp.zeros((1, LANES), F32).at[0, :MOE_GROUPS].set(b_group)
    bias = bias.at[0, MOE_GROUPS:MOE_GROUPS + MOE_EXPERTS].set(b_expert)
    return g.reshape(1, d), w_hi, w_lo, bias


def _router_specs(tm, d):
    const = lambda shape: pl.BlockSpec(shape, lambda i: (0,) * len(shape))
    in_specs = [const((1, d)), const((d, LANES)), const((d, LANES)), const((1, LANES))]
    out_specs = (pl.BlockSpec((tm * TOKEN_ROWS, LANES), lambda i: (i, 0)), pl.BlockSpec((tm, LANES), lambda i: (i, 0)),
                 const((1, LANES)))
    out_shape = lambda n: (jax.ShapeDtypeStruct((n * TOKEN_ROWS, LANES), U32), jax.ShapeDtypeStruct((n, LANES), F32),
                           jax.ShapeDtypeStruct((1, LANES), F32))
    return in_specs, out_shape, out_specs, pltpu.VMEM((1, LANES), F32)


DMA_UNROLL = 16


def _tile_at(ref, tok):
    return ref.at[pl.ds(pl.multiple_of(tok * TOKEN_ROWS, TOKEN_ROWS), TOKEN_ROWS)]


def _dispatch_kernel(rows, dest_ref, zblk_ref, h_ref, xs_ref, zbuf, sem, zsem):
    @pl.when(pl.program_id(0) == 0)
    def _():
        zbuf[...] = jnp.zeros_like(zbuf)

        def zero_copy(j):
            start = pl.multiple_of(jnp.maximum(zblk_ref[0, j], 0) * (MOE_ROWS * TOKEN_ROWS), MOE_ROWS * TOKEN_ROWS)
            return pltpu.make_async_copy(zbuf, xs_ref.at[pl.ds(start, MOE_ROWS * TOKEN_ROWS)], zsem.at[0])

        def z_issue(j, c):
            pl.when(zblk_ref[0, j] >= 0)(lambda: zero_copy(j).start())
            return c

        def z_drain(j, c):
            pl.when(zblk_ref[0, j] >= 0)(lambda: zero_copy(j).wait())
            return c

        lax.fori_loop(0, zblk_ref.shape[1], z_issue, 0)
        lax.fori_loop(0, zblk_ref.shape[1], z_drain, 0)

    def issue(i, c):
        for choice in range(2):
            pltpu.make_async_copy(_tile_at(h_ref, i), _tile_at(xs_ref, dest_ref[0, 0, 2 * i + choice]),
                                  sem.at[choice]).start(priority=choice)
        return c

    lax.fori_loop(0, rows, issue, 0, unroll=DMA_UNROLL)
    for choice in range(2):
        pltpu.make_async_copy(h_ref, xs_ref.at[pl.ds(0, rows * TOKEN_ROWS)], sem.at[choice]).wait()


def _dispatch(h_tiles, dest, zero_blocks, n_pad, rows=1024):
    n = h_tiles.shape[0] // TOKEN_ROWS
    rows = min(rows, n)
    nb = n // rows
    return pl.pallas_call(
        functools.partial(_dispatch_kernel, rows),
        out_shape=jax.ShapeDtypeStruct((n_pad * TOKEN_ROWS, LANES), h_tiles.dtype),
        grid=(nb,),
        in_specs=[pl.BlockSpec((1, 1, 2 * rows), lambda i: (i, 0, 0), memory_space=pltpu.SMEM),
                  pl.BlockSpec((1, zero_blocks.shape[0]), lambda i: (0, 0), memory_space=pltpu.SMEM),
                  pl.BlockSpec((rows * TOKEN_ROWS, LANES), lambda i: (i, 0))],
        out_specs=pl.BlockSpec(memory_space=pl.ANY),
        scratch_shapes=[pltpu.VMEM((MOE_ROWS * TOKEN_ROWS, LANES), h_tiles.dtype),
                        pltpu.SemaphoreType.DMA((2,)), pltpu.SemaphoreType.DMA((1,))],
        compiler_params=_params(("arbitrary",), 32),
        name="moe_dispatch",
    )(dest.reshape(nb, 1, 2 * rows), zero_blocks.reshape(1, -1), h_tiles)


def _moe_mlp_kernel(be_ref, used_ref, x_ref, wg_ref, wu_ref, wd_ref, o_ref, wg_s, wu_s, wd_s):
    i = pl.program_id(0)

    @pl.when(i < used_ref[0])
    def _():
        @pl.when((i == 0) | (be_ref[i] != be_ref[jnp.maximum(i - 1, 0)]))
        def _():
            wg_s[...] = wg_ref[0, 0].astype(BF16)
            wu_s[...] = wu_ref[0, 0].astype(BF16)
            wd_s[...] = wd_ref[0, 0].astype(BF16)

        x = _load_token_tiles(x_ref, MOE_ROWS).astype(BF16)
        hg = jnp.dot(x, wg_s[...], preferred_element_type=F32)
        hu = jnp.dot(x, wu_s[...], preferred_element_type=F32)
        hid = hg * _sigmoid(hg) * hu
        _store_token_tiles(o_ref, jnp.dot(hid.astype(BF16), wd_s[...], preferred_element_type=F32), MOE_ROWS)

    @pl.when(i >= used_ref[0])
    def _():
        o_ref[...] = jnp.zeros_like(o_ref)


def _moe_mlp(xs, block_expert, n_used, w_gate, w_up, w_down, layer):
    n_pad = xs.shape[0] // TOKEN_ROWS
    _, _, d, hid = w_gate.shape
    nb = n_pad // MOE_ROWS
    live = lambda i, used: jnp.minimum(i, used[0] - 1)
    tiles = pl.BlockSpec((MOE_ROWS * TOKEN_ROWS, LANES), lambda i, be, used: (live(i, used), 0))
    weight = lambda shape: pl.BlockSpec(shape, lambda i, be, used: (layer, be[live(i, used)], 0, 0))
    grid_spec = pltpu.PrefetchScalarGridSpec(
        num_scalar_prefetch=2,
        grid=(nb,),
        in_specs=[tiles, weight((1, 1, d, hid)), weight((1, 1, d, hid)), weight((1, 1, hid, d))],
        out_specs=pl.BlockSpec((MOE_ROWS * TOKEN_ROWS, LANES), lambda i, be, used: (i, 0)),
        scratch_shapes=[pltpu.VMEM((d, hid), BF16), pltpu.VMEM((d, hid), BF16), pltpu.VMEM((hid, d), BF16)],
    )
    return pl.pallas_call(
        _moe_mlp_kernel,
        out_shape=jax.ShapeDtypeStruct(xs.shape, xs.dtype),
        grid_spec=grid_spec,
        compiler_params=_params(("arbitrary",), 48),
        name="moe_mlp",
    )(block_expert, n_used, xs, w_gate, w_up, w_down)


COMBINE_AHEAD = 2


def _combine_kernel(rows, *refs):
    idx_refs = refs[:COMBINE_AHEAD + 1]
    x_ref, route_ref, ys_ref, o_ref, bufs, sem = refs[COMBINE_AHEAD + 1:]
    ring = COMBINE_AHEAD + 1
    step = pl.program_id(0)
    n_steps = pl.num_programs(0)
    cur = step % ring
    group = SUBLANES

    def issue(idx_ref, dst, first):
        for i in range(group):
            for choice in range(2):
                pltpu.make_async_copy(_tile_at(ys_ref, idx_ref[0, 0, 2 * (first + i) + choice]),
                                      _tile_at(bufs.at[dst, choice], first + i),
                                      sem.at[dst, choice]).start(priority=choice)

    def combine(first):
        rows_g = pl.ds(pl.multiple_of(first, group), group)
        gate = route_ref[rows_g, :]
        tile0 = pl.multiple_of(first * TOKEN_ROWS, group * TOKEN_ROWS)
        half = TOKEN_ROWS * LANES
        for c in range(TOKEN_ROWS):
            picked = [_unpack_words(bufs[cur, choice, pl.ds(tile0 + c, group, stride=TOKEN_ROWS), :])
                      for choice in range(2)]
            for part in range(2):
                lanes = slice(part * half + c * LANES, part * half + (c + 1) * LANES)
                o_ref[rows_g, lanes] = (x_ref[rows_g, lanes] + gate[:, 2:3] * picked[0][part]
                                        + gate[:, 3:4] * picked[1][part])

    def trips(body):
        lax.fori_loop(0, rows // group, lambda t, c: (body(t * group), c)[1], 0, unroll=4)

    for ahead in range(COMBINE_AHEAD):
        pl.when((step == 0) & (ahead < n_steps))(
            lambda ahead=ahead: trips(lambda first: issue(idx_refs[ahead], ahead, first)))
    for choice in range(2):
        pltpu.make_async_copy(ys_ref.at[pl.ds(0, rows * TOKEN_ROWS)], bufs.at[cur, choice], sem.at[cur, choice]).wait()
    more = step + COMBINE_AHEAD < n_steps
    pl.when(more)(lambda: trips(lambda first: (combine(first),
                                               issue(idx_refs[COMBINE_AHEAD], (step + COMBINE_AHEAD) % ring, first))))
    pl.when(jnp.logical_not(more))(lambda: trips(combine))


def _combine(x, ys, route, pos, rows=1024):
    n, d = x.shape
    rows = min(rows, n)
    nb = n // rows
    idx = lambda ahead: pl.BlockSpec((1, 1, 2 * rows), lambda i: (jnp.minimum(i + ahead, nb - 1), 0, 0),
                                     memory_space=pltpu.SMEM)
    pos = pos.reshape(nb, 1, 2 * rows)
    ring = COMBINE_AHEAD + 1
    return pl.pallas_call(
        functools.partial(_combine_kernel, rows),
        out_shape=jax.ShapeDtypeStruct((n, d), F32),
        grid=(nb,),
        in_specs=[idx(ahead) for ahead in range(ring)]
                 + [pl.BlockSpec((rows, d), lambda i: (i, 0)),
                    pl.BlockSpec((rows, LANES), lambda i: (i, 0)),
                    pl.BlockSpec(memory_space=pl.ANY)],
        out_specs=pl.BlockSpec((rows, d), lambda i: (i, 0)),
        scratch_shapes=[pltpu.VMEM((ring, 2, rows * TOKEN_ROWS, LANES), ys.dtype), pltpu.SemaphoreType.DMA((ring, 2))],
        compiler_params=_params(("arbitrary",), 40),
        name="moe_combine",
    )(*([pos] * ring), x, route, ys)


def _hier_moe(x, h, route, counts, w_gate, w_up, w_down, layer):
    n, _ = x.shape
    counts = counts[0, :MOE_EXPERTS].astype(jnp.int32)
    padded = (counts + MOE_ROWS - 1) // MOE_ROWS * MOE_ROWS
    cum_padded = jnp.cumsum(padded)
    expert = route[:, 0:2].astype(jnp.int32)
    seg_start = jnp.sum(jnp.where(expert[..., None] == jnp.arange(MOE_EXPERTS, dtype=jnp.int32),
                                  cum_padded - padded, 0), axis=-1)
    dest = (seg_start + route[:, 4:6].astype(jnp.int32)).reshape(-1)
    n_blocks = 2 * n // MOE_ROWS + MOE_EXPERTS
    block_start = jnp.arange(n_blocks, dtype=jnp.int32) * MOE_ROWS
    block_expert = jnp.minimum(jnp.sum((block_start[:, None] >= cum_padded[None, :]).astype(jnp.int32), axis=1),
                               MOE_EXPERTS - 1)
    n_used = (cum_padded[-1:] // MOE_ROWS).astype(jnp.int32)

    spare = jnp.arange(n_blocks - MOE_EXPERTS, n_blocks, dtype=jnp.int32)
    zero_blocks = jnp.concatenate([jnp.where(padded > 0, cum_padded // MOE_ROWS - 1, -1).astype(jnp.int32),
                                   jnp.where(spare >= n_used[0], spare, -1)])
    xs = _dispatch(h, dest, zero_blocks, n_blocks * MOE_ROWS)
    ys = _moe_mlp(xs, block_expert, n_used, w_gate, w_up, w_down, layer)
    return _combine(x, ys, route, dest)


def _even_layer(x, bsz, t_len, norm_g, w_in, prm, v_first, router):
    r, lw, k, v, a, b, g, pool = _even_prep(x.reshape(bsz, t_len, -1), norm_g, w_in.astype(BF16), prm, v_first)
    y = _rwkv_chunk(r, lw, k, v, a, b)
    return _even_out(x, y, r, k, v, g, pool, prm, router), v


def _odd_layer(x, bsz, t_len, layer_idx, norm_g, w_in, prm, router):
    q, k, v, u_s5 = _qk_prep(x.reshape(bsz, t_len, -1), norm_g, w_in.astype(BF16), prm["q_norm"], prm["k_norm"])
    lam_init = 0.8 - 0.6 * math.exp(-0.3 * layer_idx)
    lam = (jnp.exp(jnp.sum(prm["lam_q1"] * prm["lam_k1"])) - jnp.exp(jnp.sum(prm["lam_q2"] * prm["lam_k2"]))
           + lam_init)
    o_attn = _diff_attn(q, k, v, lam, prm["subln"], 1.0 - lam_init)
    ng = bsz // S5_BATCH_GROUP
    u5 = u_s5.reshape(ng, S5_BATCH_GROUP, t_len, RW_WIDTH).transpose(0, 2, 1, 3)
    o_ssm = _s5(u5, prm).transpose(0, 2, 1, 3).reshape(bsz, t_len, RW_WIDTH)
    return _odd_out(x, o_attn, o_ssm, prm["w_out"], router)


def _s5_params(a_re, a_im, log_step, b_re, b_im, c_re, c_im, d_skip, w_glu):
    lam = lax.complex(jnp.minimum(a_re, -1e-4), a_im)
    lam_bar = jnp.exp(lam * jnp.exp(log_step))
    b_bar = ((lam_bar - 1.0) / lam)[..., None] * lax.complex(b_re, b_im)
    gpt = LANES // S5_GROUP_DIM
    eye = jnp.eye(gpt, dtype=F32)

    def in_map(part):
        p = part.reshape(S5_TILES, gpt, S5_STATE, S5_GROUP_DIM)
        return jnp.einsum("jgpc,gh->jgchp", p, eye).reshape(S5_TILES, LANES, gpt * S5_STATE)

    def out_map(part):
        p = part.reshape(S5_TILES, gpt, S5_GROUP_DIM, S5_STATE)
        return jnp.einsum("jgcp,gh->jgphc", p, eye).reshape(S5_TILES, gpt * S5_STATE, LANES)

    bt = jnp.concatenate([in_map(jnp.real(b_bar)), in_map(jnp.imag(b_bar))], axis=2).astype(BF16)
    ct = jnp.concatenate([out_map(c_re), -out_map(c_im)], axis=1).astype(BF16)
    return {"bt": bt, "ct": ct, "lam_re": jnp.real(lam_bar).reshape(1, -1), "lam_im": jnp.imag(lam_bar).reshape(1, -1),
            "d": d_skip, "w_glu": w_glu.astype(BF16)}


def _block_diag(blocks):
    g, r, c = blocks.shape
    return jnp.einsum("grc,gh->grhc", blocks, jnp.eye(g, dtype=blocks.dtype)).reshape(g * r, g * c)


def kernel(x, norm_mix_g, norm_ffn_g,
           even_w_in, rw_mu, rw_w0, rw_w2, rw_a0, rw_a2, rw_g2, rw_k_k, rw_k_a, rw_r_k,
           rw_ln_g, rw_ln_b, rw_v0, rw_v1, rw_v2, pool_w, pool_scale, even_w_out,
           odd_w_in, da_q_norm, da_k_norm, da_lam_q1, da_lam_k1, da_lam_q2, da_lam_k2, da_subln,
           s5_a_re, s5_a_im, s5_log_step, s5_b_re, s5_b_im, s5_c_re, s5_c_im, s5_d, s5_w_glu,
           odd_w_out,
           moe_w_group, moe_b_group, moe_w_expert, moe_b_expert, moe_w_gate, moe_w_up, moe_w_down):
    bsz, t_len, d = x.shape
    depth = norm_mix_g.shape[0]
    xf = x.reshape(bsz * t_len, d)
    v_first = None
    for layer in range(depth):
        i = layer // 2
        router = _router_operands(norm_ffn_g[layer], moe_w_group[layer], moe_b_group[layer], moe_w_expert[layer],
                                  moe_b_expert[layer])
        if layer % 2 == 0:
            rank = rw_w2.shape[1]
            wa = jnp.zeros((LANES, 2 * RW_WIDTH), F32)
            wa = wa.at[:rank, :RW_WIDTH].set(rw_w2[i]).at[rank:, RW_WIDTH:].set(rw_a2[i])
            prm = {"mu": rw_mu[i], "w0": rw_w0[i], "a0": rw_a0[i], "wa": wa.astype(BF16), "g2": rw_g2[i].astype(BF16),
                   "k_k": rw_k_k[i], "k_a": rw_k_a[i], "r_k": rw_r_k[i], "ln_g": rw_ln_g[i], "ln_b": rw_ln_b[i],
                   "pool_bd": _block_diag(pool_w[i]).astype(BF16), "pool_scale": pool_scale[i],
                   "w_out": even_w_out[i].astype(BF16)}
            if v_first is not None:
                vr = rw_v1.shape[2]
                prm["v0"] = rw_v0[i - 1]
                prm["v1"] = jnp.zeros((RW_WIDTH, LANES), F32).at[:, :vr].set(rw_v1[i - 1]).astype(BF16)
                prm["v2"] = jnp.zeros((LANES, RW_WIDTH), F32).at[:vr, :].set(rw_v2[i - 1]).astype(BF16)
            routed, v_new = _even_layer(xf, bsz, t_len, norm_mix_g[layer], even_w_in[i], prm, v_first, router)
            if v_first is None:
                v_first = v_new
        else:
            prm = _s5_params(s5_a_re[i], s5_a_im[i], s5_log_step[i], s5_b_re[i], s5_b_im[i], s5_c_re[i],
                             s5_c_im[i], s5_d[i].reshape(-1), s5_w_glu[i])
            prm.update({"q_norm": da_q_norm[i], "k_norm": da_k_norm[i], "lam_q1": da_lam_q1[i],
                        "lam_k1": da_lam_k1[i], "lam_q2": da_lam_q2[i], "lam_k2": da_lam_k2[i],
                        "subln": da_subln[i], "w_out": odd_w_out[i].astype(BF16)})
            routed = _odd_layer(xf, bsz, t_len, layer, norm_mix_g[layer], odd_w_in[i], prm, router)
        xf = _hier_moe(*routed, moe_w_gate, moe_w_up, moe_w_down, layer)
    return xf.reshape(bsz, t_len, d)
```
